```python
import jax, jax.numpy as jnp
from jax import lax
import numpy as np

D_MODEL = 2048
BATCH = 2
SEQ = 4096
DEPTH = 2

GRID_W = 64
CTX_LEN = 256
CONV_DIM = 1024
CONV_K = 3
GLA_HEADS = 4
GLA_DK = 128
GLA_DV = 256
GLA_RANK = 16
GLA_TAU = 16.0
GLA_CHUNK = 64
GLA_QK = GLA_HEADS * GLA_DK
GLA_V = GLA_HEADS * GLA_DV
NA_HEADS = 8
NA_DH = 128
NA_W = NA_HEADS * NA_DH
WIN_R = 8
WIN_C = 16
D_FF = 5632
N_EXPERTS = 8
TOP_K = 2
ROPE_BASE = 10000.0
LN_EPS = 1e-5
N_BRANCH = 3
DEEPNORM_ALPHA = (2 * DEPTH) ** 0.25
DEEPNORM_BETA = (8 * DEPTH) ** -0.25
SPLITS = [CONV_DIM, CONV_DIM, CONV_DIM, GLA_QK, GLA_QK, GLA_V, GLA_V, GLA_RANK, GLA_RANK,
          NA_W, NA_W, NA_W, N_BRANCH * D_MODEL]
IN_COLS = sum(SPLITS)

kernel_name = "hybrid_conv_gla_natten_moe_dit"


def split_cols(u):
    return jnp.split(u, np.cumsum(SPLITS)[:-1].tolist(), axis=-1)


def heads(a, h):
    return a.reshape(a.shape[0], a.shape[1], h, -1)


def layer_norm(x, g, b):
    xf = x.astype(jnp.float32)
    mu = jnp.mean(xf, -1, keepdims=True)
    var = jnp.mean(jnp.square(xf - mu), -1, keepdims=True)
    return ((xf - mu) * lax.rsqrt(var + LN_EPS)).astype(x.dtype) * g + b


def axial_rope(x):
    _, T, _, d = x.shape
    t = jnp.arange(T)
    rows = (t // GRID_W).astype(jnp.float32)
    cols = (t % GRID_W).astype(jnp.float32)
    n_freq = d // 4
    inv = 1.0 / (ROPE_BASE ** (jnp.arange(n_freq, dtype=jnp.float32) / n_freq))
    ang = jnp.concatenate([rows[:, None] * inv, cols[:, None] * inv], -1)
    cos = jnp.cos(ang)[None, :, None, :]
    sin = jnp.sin(ang)[None, :, None, :]
    xp = x.astype(jnp.float32).reshape(x.shape[:-1] + (d // 2, 2))
    x0, x1 = xp[..., 0], xp[..., 1]
    out = jnp.stack([x0 * cos - x1 * sin, x0 * sin + x1 * cos], -1)
    return out.reshape(x.shape).astype(x.dtype)


def short_conv(a_b, a_c, a_x, w):
    z = a_c * a_x
    y = lax.conv_general_dilated(z, w[:, None, :].astype(z.dtype), window_strides=(1,),
                                 padding=[(CONV_K // 2, CONV_K // 2)],
                                 dimension_numbers=('NWC', 'WIO', 'NWC'),
                                 feature_group_count=CONV_DIM)
    return a_b * y


def gla_heads(q, k, v, lr_f, lr_b, wg_f, bg_f, wg_b, bg_b):
    q = heads(q, GLA_HEADS) * (GLA_DK ** -0.5)
    k = heads(k, GLA_HEADS)
    v = heads(v, GLA_HEADS)
    g_f = heads(jax.nn.log_sigmoid((lr_f @ wg_f + bg_f).astype(jnp.float32)) / GLA_TAU, GLA_HEADS)
    g_b = heads(jax.nn.log_sigmoid((lr_b @ wg_b + bg_b).astype(jnp.float32)) / GLA_TAU, GLA_HEADS)
    return q, k, v, g_f, g_b


def gla_chunked(q, k, v, g, s0):
    Bn, T, H, _ = q.shape
    dv = v.shape[-1]
    n = T // GLA_CHUNK

    def chunks(a):
        return a.astype(jnp.float32).reshape(Bn, n, GLA_CHUNK, H, a.shape[-1]).transpose(1, 0, 3, 2, 4)

    q, k, v, g = chunks(q), chunks(k), chunks(v), chunks(g)
    b = jnp.cumsum(g, axis=3)
    b_last = b[:, :, :, -1:, :]
    q_dec = q * jnp.exp(b)
    k_inv = k * jnp.exp(-b)
    k_end = k * jnp.exp(b_last - b)
    mask = jnp.tril(jnp.ones((GLA_CHUNK, GLA_CHUNK), dtype=bool))
    att = jnp.where(mask, jnp.einsum('nbhtd,nbhsd->nbhts', q_dec, k_inv), 0.0)
    o_intra = jnp.einsum('nbhts,nbhse->nbhte', att, v)
    d_state = jnp.einsum('nbhcd,nbhce->nbhde', k_end, v)
    decay = jnp.exp(b_last[:, :, :, 0, :])[..., None]

    def step(s, inp):
        dec, ds = inp
        return dec * s + ds, s

    s_final, s_start = lax.scan(step, s0.astype(jnp.float32), (decay, d_state))
    o = o_intra + jnp.einsum('nbhtd,nbhde->nbhte', q_dec, s_start)
    return o.transpose(1, 0, 3, 2, 4).reshape(Bn, T, H, dv), s_final


def gla_output(o, r, norm_g):
    o = o * lax.rsqrt(jnp.mean(jnp.square(o), -1, keepdims=True) + LN_EPS)
    o = o.astype(r.dtype) * norm_g
    return o.reshape(r.shape) * jax.nn.silu(r)


def neighbourhood_attention(q, k, v, kc, vc, rpb):
    Bn, S, H, dh = q.shape
    rows = S // GRID_W
    wr = min(WIN_R, rows)
    qg = q.reshape(Bn, rows, GRID_W, H, dh)
    kg = k.reshape(Bn, rows, GRID_W, H, dh)
    vg = v.reshape(Bn, rows, GRID_W, H, dh)
    col = jnp.arange(GRID_W)
    col_start = jnp.clip(col - WIN_C // 2, 0, GRID_W - WIN_C)
    col_idx = col_start[:, None] + jnp.arange(WIN_C)[None, :]
    dc = col_idx - col[:, None] + (WIN_C - 1)
    n_loc = wr * WIN_C

    def row_block(r):
        rs = jnp.clip(r - wr // 2, 0, rows - wr)
        qb = lax.dynamic_index_in_dim(qg, r, axis=1, keepdims=False)
        kb = lax.dynamic_slice_in_dim(kg, rs, wr, axis=1)
        vb = lax.dynamic_slice_in_dim(vg, rs, wr, axis=1)
        kw = kb[:, :, col_idx]
        vw = vb[:, :, col_idx]
        s_loc = jnp.einsum('bqhd,brqjhd->bhqrj', qb, kw).astype(jnp.float32)
        dr = rs + jnp.arange(wr) - r + (WIN_R - 1)
        bias = rpb[:, dr][:, :, dc].transpose(0, 2, 1, 3)
        s_loc = s_loc + bias[None].astype(jnp.float32)
        s_ctx = jnp.einsum('bqhd,blhd->bhql', qb, kc).astype(jnp.float32)
        s = jnp.concatenate([s_loc.reshape(Bn, H, GRID_W, n_loc), s_ctx], -1)
        p = jax.nn.softmax(s, axis=-1).astype(v.dtype)
        p_loc = p[..., :n_loc].reshape(Bn, H, GRID_W, wr, WIN_C)
        p_ctx = p[..., n_loc:]
        return (jnp.einsum('bhqrj,brqjhd->bqhd', p_loc, vw)
                + jnp.einsum('bhql,blhd->bqhd', p_ctx, vc))

    out = lax.map(row_block, jnp.arange(rows))
    return out.transpose(1, 0, 2, 3, 4).reshape(Bn, S, H * dh)


def context_attention(q, k, v):
    s = jnp.einsum('bqhd,bkhd->bhqk', q, k).astype(jnp.float32)
    p = jax.nn.softmax(s, axis=-1).astype(v.dtype)
    o = jnp.einsum('bhqk,bkhd->bqhd', p, v)
    return o.reshape(o.shape[0], o.shape[1], -1)


def merge_branches(y_conv, y_gla, y_na, gates, w_br_conv, w_br_gla, w_br_na, gate_b, w_out):
    g = jax.nn.sigmoid(gates + gate_b)
    g_conv, g_gla, g_na = jnp.split(g, N_BRANCH, axis=-1)
    z = g_conv * (y_conv @ w_br_conv) + g_gla * (y_gla @ w_br_gla) + g_na * (y_na @ w_br_na)
    return z @ w_out


def token_mixer(hl, hc, w_in, conv_w, wg_f, bg_f, wg_b, bg_b, gla_norm_g, rpb,
                w_br_conv, w_br_gla, w_br_na, gate_b, w_out, ctx_out):
    (la_b, la_c, la_x, lq, lk, lv, lr, llf, llb, lnq, lnk, lnv, lgates) = split_cols(hl @ w_in)
    (ca_b, ca_c, ca_x, cq, ck, cv, cr, clf, clb, cnq, cnk, cnv, cgates) = split_cols(hc @ w_in)
    Bn = hl.shape[0]
    flip = lambda a: jnp.flip(a, axis=1)

    qc, kc, vc, gcf, gcb = gla_heads(cq, ck, cv, clf, clb, wg_f, bg_f, wg_b, bg_b)
    ql, kl, vl, glf, glb = gla_heads(lq, lk, lv, llf, llb, wg_f, bg_f, wg_b, bg_b)
    ql = axial_rope(ql)
    kl = axial_rope(kl)
    s0 = jnp.zeros((Bn, GLA_HEADS, GLA_DK, GLA_DV), jnp.float32)
    oc_f, st_f = gla_chunked(qc, kc, vc, gcf, s0)
    oc_b, st_b = gla_chunked(flip(qc), flip(kc), flip(vc), flip(gcb), s0)
    ol_f, _ = gla_chunked(ql, kl, vl, glf, st_f)
    ol_b, _ = gla_chunked(flip(ql), flip(kl), flip(vl), flip(glb), st_b)
    y_gla_l = gla_output(ol_f + flip(ol_b), lr, gla_norm_g)

    na_scale = NA_DH ** -0.5
    nkc = heads(cnk, NA_HEADS)
    nvc = heads(cnv, NA_HEADS)
    y_na_l = neighbourhood_attention(heads(lnq, NA_HEADS) * na_scale, heads(lnk, NA_HEADS),
                                     heads(lnv, NA_HEADS), nkc, nvc, rpb)

    y_conv_l = short_conv(la_b, la_c, la_x, conv_w)
    out_l = merge_branches(y_conv_l, y_gla_l, y_na_l, lgates, w_br_conv, w_br_gla, w_br_na, gate_b, w_out)
    if not ctx_out:
        return out_l, None

    y_gla_c = gla_output(oc_f + flip(oc_b), cr, gla_norm_g)
    y_na_c = context_attention(heads(cnq, NA_HEADS) * na_scale, nkc, nvc)
    y_conv_c = short_conv(ca_b, ca_c, ca_x, conv_w)
    out_c = merge_branches(y_conv_c, y_gla_c, y_na_c, cgates, w_br_conv, w_br_gla, w_br_na, gate_b, w_out)
    return out_l, out_c


def swiglu(h, wg, wu, wd):
    return (jax.nn.silu(h @ wg) * (h @ wu)) @ wd


def moe_swiglu(h, router, wg, wu, wd):
    shape = h.shape
    hf = h.reshape(-1, shape[-1])
    logits = (hf @ router).astype(jnp.float32)
    top_val, top_idx = lax.top_k(logits, TOP_K)
    top_w = jax.nn.softmax(top_val, axis=-1)
    combine = jnp.sum(jax.nn.one_hot(top_idx, N_EXPERTS, dtype=jnp.float32) * top_w[..., None], axis=1)
    combine = combine.astype(h.dtype)
    out = jnp.zeros_like(hf)
    for e in range(N_EXPERTS):
        out = out + combine[:, e:e + 1] * swiglu(hf, wg[e], wu[e], wd[e])
    return out.reshape(shape)


def setup_inputs(seed: int = 0) -> dict:
    key = jax.random.key(seed)
    ks = iter(jax.random.split(key, 32))
    nrm = lambda shape, scale: jax.random.normal(next(ks), shape, jnp.float32) * scale
    D = D_MODEL
    n_dense = (DEPTH + 1) // 2
    n_moe = DEPTH // 2
    beta = DEEPNORM_BETA
    return {
        "x": nrm((BATCH, SEQ, D), 1.0),
        "c": nrm((BATCH, D), 1.0),
        "ctx": nrm((BATCH, CTX_LEN, D), 1.0),
        "c_ctx": nrm((D,), 1.0),
        "ada_w": nrm((DEPTH, D, 6 * D), D ** -0.5),
        "ada_b": nrm((DEPTH, 6 * D), 0.02),
        "w_in": nrm((DEPTH, D, IN_COLS), D ** -0.5),
        "conv_w": nrm((DEPTH, CONV_K, CONV_DIM), CONV_K ** -0.5),
        "gla_wg_f": nrm((DEPTH, GLA_RANK, GLA_QK), GLA_RANK ** -0.5),
        "gla_bg_f": nrm((DEPTH, GLA_QK), 0.1),
        "gla_wg_b": nrm((DEPTH, GLA_RANK, GLA_QK), GLA_RANK ** -0.5),
        "gla_bg_b": nrm((DEPTH, GLA_QK), 0.1),
        "gla_norm_g": 1.0 + nrm((DEPTH, GLA_DV), 0.02),
        "na_rpb": nrm((DEPTH, NA_HEADS, 2 * WIN_R - 1, 2 * WIN_C - 1), 0.02),
        "w_br_conv": nrm((DEPTH, CONV_DIM, D), CONV_DIM ** -0.5 * beta),
        "w_br_gla": nrm((DEPTH, GLA_V, D), GLA_V ** -0.5 * beta),
        "w_br_na": nrm((DEPTH, NA_W, D), NA_W ** -0.5 * beta),
        "gate_b": nrm((DEPTH, N_BRANCH * D), 0.1),
        "w_out": nrm((DEPTH, D, D), D ** -0.5 * beta),
        "ln1_g": 1.0 + nrm((DEPTH, D), 0.02),
        "ln1_b": nrm((DEPTH, D), 0.02),
        "ln2_g": 1.0 + nrm((DEPTH, D), 0.02),
        "ln2_b": nrm((DEPTH, D), 0.02),
        "ffn_w_gate": nrm((n_dense, D, D_FF), D ** -0.5),
        "ffn_w_up": nrm((n_dense, D, D_FF), D ** -0.5),
        "ffn_w_down": nrm((n_dense, D_FF, D), D_FF ** -0.5 * beta),
        "moe_router": nrm((n_moe, D, N_EXPERTS), D ** -0.5),
        "moe_w_gate": nrm((n_moe, N_EXPERTS, D, D_FF), D ** -0.5),
        "moe_w_up": nrm((n_moe, N_EXPERTS, D, D_FF), D ** -0.5),
        "moe_w_down": nrm((n_moe, N_EXPERTS, D_FF, D), D_FF ** -0.5 * beta),
    }


def reference(x, c, ctx, c_ctx, ada_w, ada_b, w_in, conv_w, gla_wg_f, gla_bg_f, gla_wg_b, gla_bg_b,
              gla_norm_g, na_rpb, w_br_conv, w_br_gla, w_br_na, gate_b, w_out,
              ln1_g, ln1_b, ln2_g, ln2_b, ffn_w_gate, ffn_w_up, ffn_w_down,
              moe_router, moe_w_gate, moe_w_up, moe_w_down):
    xl, xc = x, ctx
    for i in range(DEPTH):
        last = i == DEPTH - 1
        mod_l = jax.nn.silu(c) @ ada_w[i] + ada_b[i]
        mod_c = jax.nn.silu(c_ctx) @ ada_w[i] + ada_b[i]
        sh1, sc1, g1, sh2, sc2, g2 = [m[:, None, :] for m in jnp.split(mod_l, 6, axis=-1)]
        csh1, csc1, cg1, csh2, csc2, cg2 = jnp.split(mod_c, 6, axis=-1)

        hl = xl * (1 + sc1) + sh1
        hc = xc * (1 + csc1) + csh1
        yl, yc = token_mixer(hl, hc, w_in[i], conv_w[i], gla_wg_f[i], gla_bg_f[i], gla_wg_b[i],
                             gla_bg_b[i], gla_norm_g[i], na_rpb[i], w_br_conv[i], w_br_gla[i],
                             w_br_na[i], gate_b[i], w_out[i], not last)
        xl = layer_norm(DEEPNORM_ALPHA * xl + g1 * yl, ln1_g[i], ln1_b[i])

        hl = xl * (1 + sc2) + sh2
        if i % 2 == 0:
            j = i // 2
            ffn = lambda h: swiglu(h, ffn_w_gate[j], ffn_w_up[j], ffn_w_down[j])
        else:
            j = i // 2
            ffn = lambda h: moe_swiglu(h, moe_router[j], moe_w_gate[j], moe_w_up[j], moe_w_down[j])
        xl = layer_norm(DEEPNORM_ALPHA * xl + g2 * ffn(hl), ln2_g[i], ln2_b[i])

        if not last:
            xc = layer_norm(DEEPNORM_ALPHA * xc + cg1 * yc, ln1_g[i], ln1_b[i])
            hc = xc * (1 + csc2) + csh2
            xc = layer_norm(DEEPNORM_ALPHA * xc + cg2 * ffn(hc), ln2_g[i], ln2_b[i])
    return xl
```

```python
import functools

import numpy as np
import jax
import jax.numpy as jnp
from jax import lax
from jax.experimental import pallas as pl
from jax.experimental.pallas import tpu as pltpu

D_MODEL = 2048
BATCH = 2
SEQ = 4096
DEPTH = 2
GRID_W = 64
CTX_LEN = 256
CONV_DIM = 1024
CONV_K = 3
GLA_HEADS = 4
GLA_DK = 128
GLA_DV = 256
GLA_RANK = 16
GLA_TAU = 16.0
GLA_CHUNK = 64
GLA_QK = GLA_HEADS * GLA_DK
GLA_V = GLA_HEADS * GLA_DV
NA_HEADS = 8
NA_DH = 128
NA_W = NA_HEADS * NA_DH
WIN_R = 8
WIN_C = 16
D_FF = 5632
N_EXPERTS = 8
TOP_K = 2
ROPE_BASE = 10000.0
LN_EPS = 1e-5
N_BRANCH = 3
DEEPNORM_ALPHA = (2 * DEPTH) ** 0.25

NL = BATCH * SEQ
NC = BATCH * CTX_LEN
NT = NL + NC
GRID_ROWS = SEQ // GRID_W

BF = jnp.bfloat16
F32 = jnp.float32

V7X_VMEM_BYTES = 64 * 1024 * 1024
VMEM_LIMIT = V7X_VMEM_BYTES - 8 * 1024 * 1024
NEG_INF = -1e30

U_CONV_B, U_CONV_C, U_CONV_X = 0, 1024, 2048
U_GLA_Q, U_GLA_K, U_GLA_V, U_GLA_R = 3072, 3584, 4096, 5120
U_GATES = 6144
U_NA_Q, U_NA_K, U_NA_V = 12288, 13312, 14336
U_COLS = 15360
LR_COLS = 128

NA_QROWS = 4
NA_KROWS = NA_QROWS + WIN_R - 1
NA_QBLK = NA_QROWS * GRID_W
NA_KBLK = NA_KROWS * GRID_W
NA_NBLK = GRID_ROWS // NA_QROWS

MOE_TM = 512
MOE_GB = 256
MOE_NSLOT = NL * TOP_K + N_EXPERTS * MOE_TM
MOE_NBLK = MOE_NSLOT // MOE_TM
MOE_NGB = MOE_NSLOT // MOE_GB
MOE_NTT = NL // MOE_GB
MOE_NPAIR = MOE_NGB + N_EXPERTS * (MOE_NTT - 1)
ROUTER_LANES = 128


def _params(*sem):
    return pltpu.CompilerParams(dimension_semantics=sem, vmem_limit_bytes=VMEM_LIMIT)


def _dot(a, b):
    return jnp.dot(a, b, preferred_element_type=F32)


def _dot_nt(a, b):
    return lax.dot_general(a, b, (((1,), (1,)), ((), ())), preferred_element_type=F32)


def _dot_tn(a, b):
    return lax.dot_general(a, b, (((0,), (0,)), ((), ())), preferred_element_type=F32)


def _silu(x):
    return x * jax.nn.sigmoid(x)


def _layer_norm(x, g, b):
    mu = jnp.mean(x, axis=-1, keepdims=True)
    xc = x - mu
    var = jnp.mean(xc * xc, axis=-1, keepdims=True)
    return xc * lax.rsqrt(var + LN_EPS) * g + b


def _mod_row_of_tile(tile_rows):
    per_batch = SEQ // tile_rows
    return lambda i: jnp.minimum(i // per_batch, BATCH)


def _mod_spec(layer, k, row_of_tile):
    return pl.BlockSpec((None, 1, D_MODEL),
                        lambda i, *_: (layer * 48 + row_of_tile(i) * 6 + k, 0, 0))


def _ada_kernel(c_ref, w_ref, b_ref, o_ref):
    a = _silu(c_ref[...]).astype(BF)
    o_ref[...] = _dot(a, w_ref[...].astype(BF)) + b_ref[...]


def _ada_table(cvec, ada_w, ada_b):
    tn = 1024
    out = pl.pallas_call(
        _ada_kernel,
        out_shape=jax.ShapeDtypeStruct((DEPTH, 8, 6 * D_MODEL), F32),
        grid=(DEPTH, 6 * D_MODEL // tn),
        in_specs=[pl.BlockSpec((8, D_MODEL), lambda l, j: (0, 0)),
                  pl.BlockSpec((None, D_MODEL, tn), lambda l, j: (l, 0, j)),
                  pl.BlockSpec((None, 1, tn), lambda l, j: (l, 0, j))],
        out_specs=pl.BlockSpec((None, 8, tn), lambda l, j: (l, 0, j)),
        compiler_params=_params("parallel", "parallel"),
        name="ada_table",
    )(cvec, ada_w, ada_b.reshape(DEPTH, 1, 6 * D_MODEL))
    return out.reshape(DEPTH * 8 * 6, 1, D_MODEL)


def _modulate_kernel(x_ref, sh_ref, sc_ref, o_ref):
    o_ref[...] = (x_ref[...] * (1.0 + sc_ref[...]) + sh_ref[...]).astype(BF)


def _modulate(x_all, mod, layer):
    tm = 512
    rot = _mod_row_of_tile(tm)
    return pl.pallas_call(
        _modulate_kernel,
        out_shape=jax.ShapeDtypeStruct((NT, D_MODEL), BF),
        grid=(NT // tm,),
        in_specs=[pl.BlockSpec((tm, D_MODEL), lambda i: (i, 0)),
                  _mod_spec(layer, 0, rot), _mod_spec(layer, 1, rot)],
        out_specs=pl.BlockSpec((tm, D_MODEL), lambda i: (i, 0)),
        compiler_params=_params("parallel"),
        name="modulate",
    )(x_all, mod, mod)


def _mm_kernel(a_ref, w_ref, o_ref):
    o_ref[...] = _dot(a_ref[...], w_ref[...])


def _matmul(a, w, tm, tn, name):
    m, k = a.shape
    n = w.shape[1]
    return pl.pallas_call(
        _mm_kernel,
        out_shape=jax.ShapeDtypeStruct((m, n), F32),
        grid=(n // tn, m // tm),
        in_specs=[pl.BlockSpec((tm, k), lambda j, i: (i, 0)),
                  pl.BlockSpec((k, tn), lambda j, i: (0, j))],
        out_specs=pl.BlockSpec((tm, tn), lambda j, i: (i, j)),
        compiler_params=_params("parallel", "parallel"),
        name=name,
    )(a, w)


CONV_TM = 256


def _conv_kernel(b_ref, c_ref, x_ref, cp_ref, xp_ref, cn_ref, xn_ref, w_ref, o_ref):
    i = pl.program_id(0)
    tiles_per_seq = SEQ // CONV_TM
    is_ctx = i >= NL // CONV_TM
    is_start = jnp.logical_or(i % tiles_per_seq == 0, is_ctx)
    is_end = jnp.logical_or(i % tiles_per_seq == tiles_per_seq - 1, is_ctx)
    z = c_ref[...] * x_ref[...]
    zp = cp_ref[...] * xp_ref[...]
    zn = cn_ref[...] * xn_ref[...]
    prev_row = jnp.where(is_start, 0.0, zp[7:8, :])
    next_row = jnp.where(is_end, 0.0, zn[0:1, :])
    rows = lax.broadcasted_iota(jnp.int32, z.shape, 0)
    z_prev = jnp.where(rows == 0, prev_row, pltpu.roll(z, 1, 0))
    z_next = jnp.where(rows == CONV_TM - 1, next_row, pltpu.roll(z, CONV_TM - 1, 0))
    w = w_ref[...]
    y = w[0:1, :] * z_prev + w[1:2, :] * z + w[2:3, :] * z_next
    o_ref[...] = (b_ref[...] * y).astype(BF)


def _short_conv(u, conv_w, layer):
    tm = CONV_TM
    n8 = tm // 8
    last8 = NT // 8 - 1
    cb = CONV_DIM // CONV_DIM
    return pl.pallas_call(
        _conv_kernel,
        out_shape=jax.ShapeDtypeStruct((NT, CONV_DIM), BF),
        grid=(NT // tm,),
        in_specs=[pl.BlockSpec((tm, CONV_DIM), lambda i: (i, U_CONV_B // CONV_DIM)),
                  pl.BlockSpec((tm, CONV_DIM), lambda i: (i, U_CONV_C // CONV_DIM)),
                  pl.BlockSpec((tm, CONV_DIM), lambda i: (i, U_CONV_X // CONV_DIM)),
                  pl.BlockSpec((8, CONV_DIM), lambda i: (jnp.maximum(i * n8 - 1, 0), U_CONV_C // CONV_DIM)),
                  pl.BlockSpec((8, CONV_DIM), lambda i: (jnp.maximum(i * n8 - 1, 0), U_CONV_X // CONV_DIM)),
                  pl.BlockSpec((8, CONV_DIM), lambda i: (jnp.minimum(i * n8 + n8, last8), U_CONV_C // CONV_DIM)),
                  pl.BlockSpec((8, CONV_DIM), lambda i: (jnp.minimum(i * n8 + n8, last8), U_CONV_X // CONV_DIM)),
                  pl.BlockSpec((None, CONV_K, CONV_DIM), lambda i: (layer, 0, 0))],
        out_specs=pl.BlockSpec((tm, CONV_DIM), lambda i: (i, 0)),
        compiler_params=_params("parallel"),
        name="short_conv",
    )(u, u, u, u, u, u, u, conv_w)


GLA_STEPS = (CTX_LEN + SEQ) // GLA_CHUNK
GLA_CTX_STEPS = CTX_LEN // GLA_CHUNK


def _log_sigmoid(z):
    return -(jnp.maximum(-z, 0.0) + jnp.log1p(jnp.exp(-jnp.abs(z))))


def _rope(x, cs, sn):
    lane = lax.broadcasted_iota(jnp.int32, x.shape, 1)
    swapped = jnp.where(lane % 2 == 0, pltpu.roll(x, GLA_DK - 1, 1), pltpu.roll(x, 1, 1))
    return x * cs + swapped * sn


def _gla_direction(q, k, v, lr, cs, sn, wpad, bg, st_ref, reverse):
    c = GLA_CHUNK
    z = _dot(lr.astype(BF), wpad) + bg
    g = _log_sigmoid(z) * (1.0 / GLA_TAU)
    t_i = lax.broadcasted_iota(jnp.int32, (c, c), 0)
    s_i = lax.broadcasted_iota(jnp.int32, (c, c), 1)
    keep = (s_i >= t_i) if reverse else (s_i <= t_i)
    tri = jnp.where(keep, 1.0, 0.0).astype(BF)
    g1 = g.astype(BF)
    r1 = g - g1.astype(F32)
    g2 = r1.astype(BF)
    g3 = (r1 - g2.astype(F32)).astype(BF)
    b = _dot(tri, g1) + _dot(tri, g2) + _dot(tri, g3)
    total = b[0:1, :] if reverse else b[c - 1:c, :]
    qs = _rope(q * (GLA_DK ** -0.5), cs, sn)
    kr = _rope(k, cs, sn)
    q_dec = (qs * jnp.exp(b)).astype(BF)
    k_inv = (kr * jnp.exp(-b)).astype(BF)
    k_end = (kr * jnp.exp(total - b)).astype(BF)
    vb = v.astype(BF)
    att = jnp.where(keep, _dot_nt(q_dec, k_inv), 0.0)
    st = st_ref[...]
    o = _dot(att.astype(BF), vb) + _dot_nt(q_dec, st.astype(BF))
    st_ref[...] = st * jnp.exp(total) + _dot_tn(vb, k_end)
    return o


def _gla_kernel(qf_ref, kf_ref, vf_ref, lf_ref, csf_ref, snf_ref,
                qb_ref, kb_ref, vb_ref, lb_ref, csb_ref, snb_ref,
                wf_ref, bf_ref, wb_ref, bb_ref, of_ref, ob_ref, stf_ref, stb_ref):
    @pl.when(pl.program_id(2) == 0)
    def _():
        stf_ref[...] = jnp.zeros_like(stf_ref)
        stb_ref[...] = jnp.zeros_like(stb_ref)

    of_ref[...] = _gla_direction(qf_ref[...], kf_ref[...], vf_ref[...], lf_ref[...], csf_ref[...],
                                 snf_ref[...], wf_ref[...], bf_ref[...], stf_ref, False)
    ob_ref[...] = _gla_direction(qb_ref[...], kb_ref[...], vb_ref[...], lb_ref[...], csb_ref[...],
                                 snb_ref[...], wb_ref[...], bb_ref[...], stb_ref, True)


def _gla(u, lr, rope_cos, rope_sin, wpad_f, bg_f, wpad_b, bg_b):
    c = GLA_CHUNK
    lat_chunks = SEQ // c
    ctx_base = NL // c

    def row_f(b, s):
        return jnp.where(s < GLA_CTX_STEPS, ctx_base + b * GLA_CTX_STEPS + s, b * lat_chunks + s - GLA_CTX_STEPS)

    def row_b(b, s):
        return jnp.where(s < GLA_CTX_STEPS, ctx_base + b * GLA_CTX_STEPS + GLA_CTX_STEPS - 1 - s,
                         b * lat_chunks + GLA_STEPS - 1 - s)

    def rope_f(s):
        return jnp.where(s < GLA_CTX_STEPS, lat_chunks, s - GLA_CTX_STEPS)

    def rope_b(s):
        return jnp.where(s < GLA_CTX_STEPS, lat_chunks, GLA_STEPS - 1 - s)

    def stream(row, rope):
        return [pl.BlockSpec((c, GLA_DK), lambda b, h, s: (row(b, s), U_GLA_Q // GLA_DK + h)),
                pl.BlockSpec((c, GLA_DK), lambda b, h, s: (row(b, s), U_GLA_K // GLA_DK + h)),
                pl.BlockSpec((c, GLA_DV), lambda b, h, s: (row(b, s), U_GLA_V // GLA_DV + h)),
                pl.BlockSpec((c, LR_COLS), lambda b, h, s: (row(b, s), 0)),
                pl.BlockSpec((c, GLA_DK), lambda b, h, s: (rope(s), 0)),
                pl.BlockSpec((c, GLA_DK), lambda b, h, s: (rope(s), 0))]

    head_w = pl.BlockSpec((None, LR_COLS, GLA_DK), lambda b, h, s: (h, 0, 0))
    head_b = pl.BlockSpec((None, 1, GLA_DK), lambda b, h, s: (h, 0, 0))
    return pl.pallas_call(
        _gla_kernel,
        out_shape=(jax.ShapeDtypeStruct((NT, GLA_V), F32), jax.ShapeDtypeStruct((NT, GLA_V), F32)),
        grid=(BATCH, GLA_HEADS, GLA_STEPS),
        in_specs=stream(row_f, rope_f) + stream(row_b, rope_b) + [head_w, head_b, head_w, head_b],
        out_specs=(pl.BlockSpec((c, GLA_DV), lambda b, h, s: (row_f(b, s), h)),
                   pl.BlockSpec((c, GLA_DV), lambda b, h, s: (row_b(b, s), h))),
        scratch_shapes=[pltpu.VMEM((GLA_DV, GLA_DK), F32), pltpu.VMEM((GLA_DV, GLA_DK), F32)],
        compiler_params=_params("parallel", "parallel", "arbitrary"),
        name="gla",
    )(u, u, u, lr, rope_cos, rope_sin, u, u, u, lr, rope_cos, rope_sin, wpad_f, bg_f, wpad_b, bg_b)


def _rope_tables():
    t = jnp.arange(SEQ)
    rows = (t // GRID_W).astype(F32)
    cols = (t % GRID_W).astype(F32)
    n_freq = GLA_DK // 4
    inv = 1.0 / (ROPE_BASE ** (jnp.arange(n_freq, dtype=F32) / n_freq))
    ang = jnp.concatenate([rows[:, None] * inv, cols[:, None] * inv], -1)
    cos = jnp.repeat(jnp.cos(ang), 2, axis=-1)
    sin = jnp.repeat(jnp.sin(ang), 2, axis=-1)
    sign = jnp.tile(jnp.array([-1.0, 1.0], F32), GLA_DK // 2)
    cos = jnp.concatenate([cos, jnp.ones((GRID_W, GLA_DK), F32)], 0)
    sin = jnp.concatenate([sin * sign, jnp.zeros((GRID_W, GLA_DK), F32)], 0)
    return cos, sin


def _decay_weights(wg, bg, lane_offset):
    w = wg.reshape(GLA_RANK, GLA_HEADS, GLA_DK).transpose(1, 0, 2)
    w = jnp.pad(w, ((0, 0), (lane_offset, LR_COLS - GLA_RANK - lane_offset), (0, 0))).astype(BF)
    return w, bg.reshape(GLA_HEADS, 1, GLA_DK)


def _na_softmax_out(parts, o_ref):
    m = functools.reduce(jnp.maximum, [jnp.max(s, axis=-1, keepdims=True) for s, _ in parts])
    ps = [jnp.exp(s - m) for s, _ in parts]
    l = functools.reduce(jnp.add, [jnp.sum(p, axis=-1, keepdims=True) for p in ps])
    o = functools.reduce(jnp.add, [_dot(p.astype(BF), v) for p, (_, v) in zip(ps, parts)])
    o_ref[...] = (o / l).astype(o_ref.dtype)


def _na_kernel(q_ref, k_ref, v_ref, kc_ref, vc_ref, bias_ref, o_ref):
    j = pl.program_id(2)
    q = (q_ref[...] * (NA_DH ** -0.5)).astype(BF)
    kc = kc_ref[...].astype(BF)
    vc = vc_ref[...].astype(BF)
    s_ctx = _dot_nt(q, kc)

    @pl.when(j < NA_NBLK)
    def _():
        base = jnp.clip(NA_QROWS * j - WIN_R // 2, 0, GRID_ROWS - NA_KROWS)
        start = pl.multiple_of(base * GRID_W, GRID_W)
        kl = k_ref[pl.ds(start, NA_KBLK), :].astype(BF)
        vl = v_ref[pl.ds(start, NA_KBLK), :].astype(BF)
        s_loc = _dot_nt(q, kl) + bias_ref[...]
        _na_softmax_out([(s_loc, vl), (s_ctx, vc)], o_ref)

    @pl.when(j == NA_NBLK)
    def _():
        _na_softmax_out([(s_ctx, vc)], o_ref)


def _na_bias_tables(rpb):
    qc = np.arange(GRID_W)
    cs = np.clip(qc - WIN_C // 2, 0, GRID_W - WIN_C)
    kcol = np.arange(GRID_W)
    col_ok = (kcol[None, :] >= cs[:, None]) & (kcol[None, :] < cs[:, None] + WIN_C)
    dc = np.clip(kcol[None, :] - qc[:, None] + WIN_C - 1, 0, 2 * WIN_C - 2)
    tm = jnp.where(jnp.asarray(col_ok)[None, None], rpb[:, :, jnp.asarray(dc)], NEG_INF)
    neg = jnp.full((NA_HEADS, GRID_W, GRID_W), NEG_INF, F32)
    classes = []
    for j in (0, 1, NA_NBLK - 1):
        base = int(np.clip(NA_QROWS * j - WIN_R // 2, 0, GRID_ROWS - NA_KROWS))
        q_rows = []
        for qr in range(NA_QROWS):
            r = NA_QROWS * j + qr
            rs = int(np.clip(r - WIN_R // 2, 0, GRID_ROWS - WIN_R))
            k_blocks = []
            for kr in range(NA_KROWS):
                a = base + kr
                if rs <= a < rs + WIN_R:
                    k_blocks.append(tm[:, a - r + WIN_R - 1])
                else:
                    k_blocks.append(neg)
            q_rows.append(jnp.concatenate(k_blocks, axis=-1))
        classes.append(jnp.concatenate(q_rows, axis=1))
    return jnp.stack(classes, axis=1).reshape(NA_HEADS * 3, NA_QBLK, NA_KBLK)


def _neighbourhood_attention(u, bias, with_ctx):
    steps = NA_NBLK + (1 if with_ctx else 0)
    ctx_q = NL // NA_QBLK

    def q_row(b, j):
        return jnp.where(j < NA_NBLK, b * NA_NBLK + j, ctx_q + b)

    def bias_class(j):
        return jnp.where(j == 0, 0, jnp.where(j >= NA_NBLK - 1, 2, 1))

    ctx_blk = NL // CTX_LEN
    return pl.pallas_call(
        _na_kernel,
        out_shape=jax.ShapeDtypeStruct((NT, NA_W), BF),
        grid=(BATCH, NA_HEADS, steps),
        in_specs=[pl.BlockSpec((NA_QBLK, NA_DH), lambda b, h, j: (q_row(b, j), U_NA_Q // NA_DH + h)),
                  pl.BlockSpec((SEQ, NA_DH), lambda b, h, j: (b, U_NA_K // NA_DH + h)),
                  pl.BlockSpec((SEQ, NA_DH), lambda b, h, j: (b, U_NA_V // NA_DH + h)),
                  pl.BlockSpec((CTX_LEN, NA_DH), lambda b, h, j: (ctx_blk + b, U_NA_K // NA_DH + h)),
                  pl.BlockSpec((CTX_LEN, NA_DH), lambda b, h, j: (ctx_blk + b, U_NA_V // NA_DH + h)),
                  pl.BlockSpec((None, NA_QBLK, NA_KBLK), lambda b, h, j: (h * 3 + bias_class(j), 0, 0))],
        out_specs=pl.BlockSpec((NA_QBLK, NA_DH), lambda b, h, j: (q_row(b, j), h)),
        compiler_params=_params("parallel", "parallel", "arbitrary"),
        name="neighbourhood_attention",
    )(u, u, u, u, u, bias)


MERGE_TM = 256


def _merge_kernel(yc_ref, of_ref, ob_ref, r_ref, yn_ref, gt_ref, x_ref,
                  ng_ref, gb_ref, wc_ref, wg_ref, wn_ref, wo_ref,
                  g1_ref, lg_ref, lb_ref, sh2_ref, sc2_ref, xo_ref, ho_ref):
    o = of_ref[...] + ob_ref[...]
    r = r_ref[...]
    ng = ng_ref[...]
    heads = []
    for h in range(GLA_HEADS):
        oh = o[:, h * GLA_DV:(h + 1) * GLA_DV]
        oh = oh * lax.rsqrt(jnp.mean(oh * oh, axis=-1, keepdims=True) + LN_EPS)
        heads.append(oh * ng * _silu(r[:, h * GLA_DV:(h + 1) * GLA_DV]))
    y_gla = jnp.concatenate(heads, axis=-1).astype(BF)
    g = jax.nn.sigmoid(gt_ref[...] + gb_ref[...])
    z = (g[:, :D_MODEL] * _dot(yc_ref[...], wc_ref[...])
         + g[:, D_MODEL:2 * D_MODEL] * _dot(y_gla, wg_ref[...])
         + g[:, 2 * D_MODEL:] * _dot(yn_ref[...], wn_ref[...]))
    y = _dot(z.astype(BF), wo_ref[...])
    xn = _layer_norm(DEEPNORM_ALPHA * x_ref[...] + g1_ref[...] * y, lg_ref[...], lb_ref[...])
    xo_ref[...] = xn
    ho_ref[...] = (xn * (1.0 + sc2_ref[...]) + sh2_ref[...]).astype(BF)


def _merge(y_conv, o_f, o_b, u, y_na, x_all, mod, layer, rows, norm_g, gate_b, w_conv, w_gla, w_na, w_out,
           ln_g, ln_b):
    tm = MERGE_TM
    rot = _mod_row_of_tile(tm)
    row = lambda width: pl.BlockSpec((tm, width), lambda i: (i, 0))
    const = lambda shape: pl.BlockSpec(shape, lambda i: (0,) * len(shape), pipeline_mode=pl.Buffered(1))
    return pl.pallas_call(
        _merge_kernel,
        out_shape=(jax.ShapeDtypeStruct((rows, D_MODEL), F32), jax.ShapeDtypeStruct((rows, D_MODEL), BF)),
        grid=(rows // tm,),
        in_specs=[row(CONV_DIM), row(GLA_V), row(GLA_V),
                  pl.BlockSpec((tm, GLA_V), lambda i: (i, U_GLA_R // GLA_V)),
                  row(NA_W),
                  pl.BlockSpec((tm, N_BRANCH * D_MODEL), lambda i: (i, U_GATES // (N_BRANCH * D_MODEL))),
                  row(D_MODEL),
                  const((1, GLA_DV)), const((1, N_BRANCH * D_MODEL)),
                  const((CONV_DIM, D_MODEL)), const((GLA_V, D_MODEL)), const((NA_W, D_MODEL)),
                  const((D_MODEL, D_MODEL)),
                  _mod_spec(layer, 2, rot), const((1, D_MODEL)), const((1, D_MODEL)),
                  _mod_spec(layer, 3, rot), _mod_spec(layer, 4, rot)],
        out_specs=(row(D_MODEL), row(D_MODEL)),
        compiler_params=_params("parallel"),
        name="merge",
    )(y_conv, o_f, o_b, u, y_na, u, x_all, norm_g, gate_b, w_conv, w_gla, w_na, w_out,
      mod, ln_g, ln_b, mod, mod)


FFN_TM = 512
FFN_TF = 512


def _ffn_kernel(h_ref, wg_ref, wu_ref, wd_ref, x_ref, g2_ref, lg_ref, lb_ref, shn_ref, scn_ref,
                xo_ref, ho_ref, acc_ref):
    j = pl.program_id(1)

    @pl.when(j == 0)
    def _():
        acc_ref[...] = jnp.zeros_like(acc_ref)

    h = h_ref[...]
    a = _silu(_dot(h, wg_ref[...])) * _dot(h, wu_ref[...])
    acc_ref[...] += _dot(a.astype(BF), wd_ref[...])

    @pl.when(j == pl.num_programs(1) - 1)
    def _():
        xn = _layer_norm(DEEPNORM_ALPHA * x_ref[...] + g2_ref[...] * acc_ref[...], lg_ref[...], lb_ref[...])
        xo_ref[...] = xn
        ho_ref[...] = (xn * (1.0 + scn_ref[...]) + shn_ref[...]).astype(BF)


def _dense_ffn(h2, x_all, mod, layer, wg, wu, wd, ln_g, ln_b):
    tm, tf = FFN_TM, FFN_TF
    rot = _mod_row_of_tile(tm)
    row = lambda: pl.BlockSpec((tm, D_MODEL), lambda i, j: (i, 0))
    const = lambda: pl.BlockSpec((1, D_MODEL), lambda i, j: (0, 0))
    return pl.pallas_call(
        _ffn_kernel,
        out_shape=(jax.ShapeDtypeStruct((NT, D_MODEL), F32), jax.ShapeDtypeStruct((NT, D_MODEL), BF)),
        grid=(NT // tm, D_FF // tf),
        in_specs=[row(),
                  pl.BlockSpec((D_MODEL, tf), lambda i, j: (0, j)),
                  pl.BlockSpec((D_MODEL, tf), lambda i, j: (0, j)),
                  pl.BlockSpec((tf, D_MODEL), lambda i, j: (j, 0)),
                  row(), _mod_spec(layer, 5, rot), const(), const(),
                  _mod_spec(layer + 1, 0, rot), _mod_spec(layer + 1, 1, rot)],
        out_specs=(row(), row()),
        scratch_shapes=[pltpu.VMEM((tm, D_MODEL), F32)],
        compiler_params=_params("parallel", "arbitrary"),
        name="dense_ffn",
    )(h2, wg, wu, wd, x_all, mod, ln_g, ln_b, mod, mod)


def _router_kernel(h_ref, w_ref, idx_ref, wt_ref):
    logits = _dot(h_ref[...], w_ref[...])
    lane = lax.broadcasted_iota(jnp.int32, logits.shape, 1)
    logits = jnp.where(lane < N_EXPERTS, logits, -jnp.inf)
    m1 = jnp.max(logits, axis=-1, keepdims=True)
    i1 = jnp.min(jnp.where(logits == m1, lane, ROUTER_LANES), axis=-1, keepdims=True)
    rest = jnp.where(lane == i1, -jnp.inf, logits)
    m2 = jnp.max(rest, axis=-1, keepdims=True)
    i2 = jnp.min(jnp.where(rest == m2, lane, ROUTER_LANES), axis=-1, keepdims=True)
    e2 = jnp.exp(m2 - m1)
    w1 = 1.0 / (1.0 + e2)
    w2 = e2 / (1.0 + e2)
    idx_ref[...] = jnp.where(lane == 0, i1, jnp.where(lane == 1, i2, 0))
    wt_ref[...] = jnp.where(lane == 0, w1, jnp.where(lane == 1, w2, 0.0))


def _router(h2, router_w):
    tm = 512
    return pl.pallas_call(
        _router_kernel,
        out_shape=(jax.ShapeDtypeStruct((NL, ROUTER_LANES), jnp.int32),
                   jax.ShapeDtypeStruct((NL, ROUTER_LANES), F32)),
        grid=(NL // tm,),
        in_specs=[pl.BlockSpec((tm, D_MODEL), lambda i: (i, 0)),
                  pl.BlockSpec((D_MODEL, ROUTER_LANES), lambda i: (0, 0))],
        out_specs=(pl.BlockSpec((tm, ROUTER_LANES), lambda i: (i, 0)),
                   pl.BlockSpec((tm, ROUTER_LANES), lambda i: (i, 0))),
        compiler_params=_params("parallel"),
        name="moe_router",
    )(h2, router_w)


def _dispatch_plan(idx):
    i32 = jnp.int32
    experts = jnp.arange(N_EXPERTS, dtype=i32)
    oh_a = (idx[:, 0:1] == experts).astype(i32)
    oh_b = (idx[:, 1:2] == experts).astype(i32)
    cum = jnp.cumsum(oh_a + oh_b, axis=0)
    counts = cum[-1]
    region = ((counts + MOE_TM - 1) // MOE_TM) * MOE_TM
    g_end = jnp.cumsum(region)
    g_start = g_end - region
    slot_table = g_start[None, :] + cum - 1
    slot_a = jnp.sum(oh_a * slot_table, axis=1)
    slot_b = jnp.sum(oh_b * slot_table, axis=1)

    tile_start = jnp.arange(MOE_NBLK, dtype=i32) * MOE_TM
    tile_expert = jnp.minimum(jnp.sum((tile_start[:, None] >= g_end[None, :]).astype(i32), axis=1),
                              N_EXPERTS - 1)

    pair_ids = jnp.arange(MOE_NPAIR, dtype=i32)

    blk_start = jnp.arange(MOE_NGB, dtype=i32) * MOE_GB
    blk_e = jnp.minimum(jnp.sum((blk_start[:, None] >= g_end[None, :]).astype(i32), axis=1), N_EXPERTS - 1)
    r0 = blk_start - g_start[blk_e]
    r1 = jnp.minimum(r0 + MOE_GB, counts[blk_e])
    has = r1 > r0
    cum_e = cum[:, blk_e]
    t_lo = jnp.sum((cum_e < (r0 + 1)[None, :]).astype(i32), axis=0) // MOE_GB
    t_hi = jnp.sum((cum_e < r1[None, :]).astype(i32), axis=0) // MOE_GB
    t_lo = jnp.where(has, t_lo, 0)
    n_d = jnp.where(has, t_hi - t_lo + 1, 1)
    d_end = jnp.cumsum(n_d)
    d_total = d_end[-1]
    d_blk = jnp.minimum(jnp.sum((pair_ids[:, None] >= d_end[None, :]).astype(i32), axis=1), MOE_NGB - 1)
    d_tile = jnp.clip(t_lo[d_blk] + pair_ids - (d_end[d_blk] - n_d[d_blk]), 0, MOE_NTT - 1)
    d_valid = pair_ids < d_total
    d_prev = jnp.concatenate([jnp.full((1,), -1, i32), d_blk[:-1]])
    d_next = jnp.concatenate([d_blk[1:], jnp.full((1,), -1, i32)])
    d_first = d_valid & (d_blk != d_prev)
    d_last = d_valid & ((d_blk != d_next) | (pair_ids == d_total - 1))
    d_tile = jnp.where(d_valid, d_tile, d_tile[jnp.maximum(d_total - 1, 0)])
    dispatch = (d_blk, d_tile, d_first.astype(i32), d_last.astype(i32), d_valid.astype(i32))

    tile_last = jnp.arange(MOE_NTT, dtype=i32) * MOE_GB + MOE_GB - 1
    hi = cum[tile_last]
    lo = jnp.concatenate([jnp.zeros((1, N_EXPERTS), i32), hi[:-1]], axis=0)
    b_lo = (g_start[None, :] + lo) // MOE_GB
    b_hi = (g_start[None, :] + hi - 1) // MOE_GB
    n_c = jnp.where(hi > lo, b_hi - b_lo + 1, 0).reshape(-1)
    b_lo = b_lo.reshape(-1)
    c_end = jnp.cumsum(n_c)
    c_total = c_end[-1]
    c_idx = jnp.minimum(jnp.sum((pair_ids[:, None] >= c_end[None, :]).astype(i32), axis=1),
                        MOE_NTT * N_EXPERTS - 1)
    c_blk = jnp.clip(b_lo[c_idx] + pair_ids - (c_end[c_idx] - n_c[c_idx]), 0, MOE_NGB - 1)
    c_tile = c_idx // N_EXPERTS
    c_valid = pair_ids < c_total
    c_tile = jnp.where(c_valid, c_tile, MOE_NTT - 1)
    c_blk = jnp.where(c_valid, c_blk, c_blk[jnp.maximum(c_total - 1, 0)])
    c_prev = jnp.concatenate([jnp.full((1,), -1, i32), c_tile[:-1]])
    c_next = jnp.concatenate([c_tile[1:], jnp.full((1,), -1, i32)])
    c_first = c_valid & (c_tile != c_prev)
    c_last = c_valid & ((c_tile != c_next) | (pair_ids == c_total - 1))
    combine = (c_tile, c_blk, c_first.astype(i32), c_last.astype(i32), c_valid.astype(i32))
    return slot_a, slot_b, tile_expert, dispatch, combine


def _dispatch_kernel(blk_ref, tile_ref, first_ref, last_ref, valid_ref,
                     h_ref, sa_ref, sb_ref, wa_ref, wb_ref, hs_ref, ws_ref, acc_ref, wacc_ref):
    p = pl.program_id(0)

    @pl.when(first_ref[p] == 1)
    def _():
        acc_ref[...] = jnp.zeros_like(acc_ref)
        wacc_ref[...] = jnp.zeros_like(wacc_ref)

    @pl.when(valid_ref[p] == 1)
    def _():
        slots = blk_ref[p] * MOE_GB + lax.broadcasted_iota(jnp.int32, (MOE_GB, MOE_GB), 0)
        hit_a = slots == sa_ref[...]
        hit_b = slots == sb_ref[...]
        onehot = jnp.where(hit_a, 1.0, jnp.where(hit_b, 1.0, 0.0)).astype(BF)
        acc_ref[...] += _dot(onehot, h_ref[...])
        w = jnp.where(hit_a, wa_ref[...], jnp.where(hit_b, wb_ref[...], 0.0))
        wacc_ref[...] += jnp.broadcast_to(jnp.sum(w, axis=1, keepdims=True), wacc_ref.shape)

    @pl.when(last_ref[p] == 1)
    def _():
        hs_ref[...] = acc_ref[...].astype(BF)
        ws_ref[...] = wacc_ref[...]


def _dispatch(h2, slot_a, slot_b, w_a, w_b, plan):
    tok = lambda p, blk, tile, *_: (tile[p], 0, 0)
    grid_spec = pltpu.PrefetchScalarGridSpec(
        num_scalar_prefetch=5,
        grid=(MOE_NPAIR,),
        in_specs=[pl.BlockSpec((MOE_GB, D_MODEL), lambda p, blk, tile, *_: (tile[p], 0)),
                  pl.BlockSpec((None, 1, MOE_GB), tok), pl.BlockSpec((None, 1, MOE_GB), tok),
                  pl.BlockSpec((None, 1, MOE_GB), tok), pl.BlockSpec((None, 1, MOE_GB), tok)],
        out_specs=(pl.BlockSpec((MOE_GB, D_MODEL), lambda p, blk, *_: (blk[p], 0)),
                   pl.BlockSpec((MOE_GB, ROUTER_LANES), lambda p, blk, *_: (blk[p], 0))),
        scratch_shapes=[pltpu.VMEM((MOE_GB, D_MODEL), F32), pltpu.VMEM((MOE_GB, ROUTER_LANES), F32)])
    shape3 = (MOE_NTT, 1, MOE_GB)
    return pl.pallas_call(
        _dispatch_kernel,
        out_shape=(jax.ShapeDtypeStruct((MOE_NSLOT, D_MODEL), BF),
                   jax.ShapeDtypeStruct((MOE_NSLOT, ROUTER_LANES), F32)),
        grid_spec=grid_spec,
        compiler_params=_params("arbitrary"),
        name="moe_dispatch",
    )(*plan, h2, slot_a.reshape(shape3), slot_b.reshape(shape3), w_a.reshape(shape3), w_b.reshape(shape3))


def _expert_up_kernel(e_ref, hs_ref, wg_ref, wu_ref, a_ref, wgs_ref, wus_ref):
    i = pl.program_id(1)
    changed = jnp.logical_or(i == 0, e_ref[i] != e_ref[jnp.maximum(i - 1, 0)])

    @pl.when(changed)
    def _():
        wgs_ref[...] = wg_ref[...].astype(BF)
        wus_ref[...] = wu_ref[...].astype(BF)

    h = hs_ref[...]
    a_ref[...] = (_silu(_dot(h, wgs_ref[...])) * _dot(h, wus_ref[...])).astype(BF)


def _expert_up(hs, tile_expert, wg, wu, moe_layer):
    tf = 512
    grid_spec = pltpu.PrefetchScalarGridSpec(
        num_scalar_prefetch=1,
        grid=(D_FF // tf, MOE_NBLK),
        in_specs=[pl.BlockSpec((MOE_TM, D_MODEL), lambda j, i, e: (i, 0)),
                  pl.BlockSpec((None, None, D_MODEL, tf), lambda j, i, e: (moe_layer, e[i], 0, j)),
                  pl.BlockSpec((None, None, D_MODEL, tf), lambda j, i, e: (moe_layer, e[i], 0, j))],
        out_specs=pl.BlockSpec((MOE_TM, tf), lambda j, i, e: (i, j)),
        scratch_shapes=[pltpu.VMEM((D_MODEL, tf), BF), pltpu.VMEM((D_MODEL, tf), BF)])
    return pl.pallas_call(
        _expert_up_kernel,
        out_shape=jax.ShapeDtypeStruct((MOE_NSLOT, D_FF), BF),
        grid_spec=grid_spec,
        compiler_params=_params("arbitrary", "arbitrary"),
        name="moe_expert_up",
    )(tile_expert, hs, wg, wu)


def _expert_down_kernel(e_ref, a_ref, wd_ref, ws_ref, yh_ref, yl_ref, wds_ref):
    i = pl.program_id(1)
    changed = jnp.logical_or(i == 0, e_ref[i] != e_ref[jnp.maximum(i - 1, 0)])

    @pl.when(changed)
    def _():
        wds_ref[...] = wd_ref[...].astype(BF)

    y = _dot(a_ref[...], wds_ref[...]) * ws_ref[:, 0:1]
    hi = y.astype(BF)
    yh_ref[...] = hi
    yl_ref[...] = (y - hi.astype(F32)).astype(BF)


def _expert_down(a, tile_expert, wd, ws, moe_layer):
    tn = 512
    grid_spec = pltpu.PrefetchScalarGridSpec(
        num_scalar_prefetch=1,
        grid=(D_MODEL // tn, MOE_NBLK),
        in_specs=[pl.BlockSpec((MOE_TM, D_FF), lambda j, i, e: (i, 0)),
                  pl.BlockSpec((None, None, D_FF, tn), lambda j, i, e: (moe_layer, e[i], 0, j)),
                  pl.BlockSpec((MOE_TM, ROUTER_LANES), lambda j, i, e: (i, 0))],
        out_specs=(pl.BlockSpec((MOE_TM, tn), lambda j, i, e: (i, j)),
                   pl.BlockSpec((MOE_TM, tn), lambda j, i, e: (i, j))),
        scratch_shapes=[pltpu.VMEM((D_FF, tn), BF)])
    return pl.pallas_call(
        _expert_down_kernel,
        out_shape=(jax.ShapeDtypeStruct((MOE_NSLOT, D_MODEL), BF),
                   jax.ShapeDtypeStruct((MOE_NSLOT, D_MODEL), BF)),
        grid_spec=grid_spec,
        compiler_params=_params("arbitrary", "arbitrary"),
        name="moe_expert_down",
    )(tile_expert, a, wd, ws)


def _combine_kernel(tile_ref, blk_ref, first_ref, last_ref, valid_ref,
                    yh_ref, yl_ref, sa_ref, sb_ref, x_ref, g2_ref, lg_ref, lb_ref, o_ref, acc_ref):
    p = pl.program_id(0)

    @pl.when(first_ref[p] == 1)
    def _():
        acc_ref[...] = jnp.zeros_like(acc_ref)

    @pl.when(valid_ref[p] == 1)
    def _():
        slots = blk_ref[p] * MOE_GB + lax.broadcasted_iota(jnp.int32, (MOE_GB, MOE_GB), 1)
        onehot = jnp.where(slots == sa_ref[...], 1.0, jnp.where(slots == sb_ref[...], 1.0, 0.0)).astype(BF)
        acc_ref[...] += _dot(onehot, yh_ref[...]) + _dot(onehot, yl_ref[...])

    @pl.when(last_ref[p] == 1)
    def _():
        o_ref[...] = _layer_norm(DEEPNORM_ALPHA * x_ref[...] + g2_ref[...] * acc_ref[...],
                                 lg_ref[...], lb_ref[...])


def _combine(y_hi, y_lo, slot_a, slot_b, x_lat, mod, layer, ln_g, ln_b, plan):
    per_batch = SEQ // MOE_GB
    slot_blk = lambda p, tile, blk, *_: (blk[p], 0)
    tok_blk = lambda p, tile, *_: (tile[p], 0)
    grid_spec = pltpu.PrefetchScalarGridSpec(
        num_scalar_prefetch=5,
        grid=(MOE_NPAIR,),
        in_specs=[pl.BlockSpec((MOE_GB, D_MODEL), slot_blk), pl.BlockSpec((MOE_GB, D_MODEL), slot_blk),
                  pl.BlockSpec((MOE_GB, 1), tok_blk), pl.BlockSpec((MOE_GB, 1), tok_blk),
                  pl.BlockSpec((MOE_GB, D_MODEL), tok_blk),
                  pl.BlockSpec((None, 1, D_MODEL),
                               lambda p, tile, *_: (layer * 48 + (tile[p] // per_batch) * 6 + 5, 0, 0)),
                  pl.BlockSpec((1, D_MODEL), lambda p, *_: (0, 0)),
                  pl.BlockSpec((1, D_MODEL), lambda p, *_: (0, 0))],
        out_specs=pl.BlockSpec((MOE_GB, D_MODEL), tok_blk),
        scratch_shapes=[pltpu.VMEM((MOE_GB, D_MODEL), F32)])
    return pl.pallas_call(
        _combine_kernel,
        out_shape=jax.ShapeDtypeStruct((NL, D_MODEL), F32),
        grid_spec=grid_spec,
        compiler_params=_params("arbitrary"),
        name="moe_combine",
    )(*plan, y_hi, y_lo, slot_a.reshape(NL, 1), slot_b.reshape(NL, 1), x_lat, mod, ln_g, ln_b)


def _moe_ffn(h2, x_lat, mod, layer, router_w, wg, wu, wd, moe_layer, ln_g, ln_b):
    router_pad = jnp.pad(router_w, ((0, 0), (0, ROUTER_LANES - N_EXPERTS))).astype(BF)
    idx, wts = _router(h2, router_pad)
    slot_a, slot_b, tile_expert, d_plan, c_plan = _dispatch_plan(idx[:, :TOP_K])
    hs, ws = _dispatch(h2, slot_a, slot_b, wts[:, 0], wts[:, 1], d_plan)
    a = _expert_up(hs, tile_expert, wg, wu, moe_layer)
    y_hi, y_lo = _expert_down(a, tile_expert, wd, ws, moe_layer)
    return _combine(y_hi, y_lo, slot_a, slot_b, x_lat, mod, layer, ln_g, ln_b, c_plan)


def _reorder_w_in(w):
    gla_end = 3 * CONV_DIM + 2 * GLA_QK + 2 * GLA_V
    lr_end = gla_end + 2 * GLA_RANK
    na_end = lr_end + 3 * NA_W
    main = jnp.concatenate([w[:, :gla_end], w[:, na_end:], w[:, lr_end:na_end]], axis=1).astype(BF)
    lr = jnp.pad(w[:, gla_end:lr_end], ((0, 0), (0, LR_COLS - 2 * GLA_RANK))).astype(BF)
    return main, lr


def kernel(x, c, ctx, c_ctx, ada_w, ada_b, w_in, conv_w, gla_wg_f, gla_bg_f, gla_wg_b, gla_bg_b, gla_norm_g, na_rpb, w_br_conv, w_br_gla, w_br_na, gate_b, w_out, ln1_g, ln1_b, ln2_g, ln2_b, ffn_w_gate, ffn_w_up, ffn_w_down, moe_router, moe_w_gate, moe_w_up, moe_w_down):
    assert DEPTH == 2 and x.shape == (BATCH, SEQ, D_MODEL) and ctx.shape == (BATCH, CTX_LEN, D_MODEL)
    cvec = jnp.concatenate([c, c_ctx[None, :], jnp.zeros((8 - BATCH - 1, D_MODEL), F32)], axis=0)
    mod = _ada_table(cvec, ada_w, ada_b)
    rope_cos, rope_sin = _rope_tables()
    x_all = jnp.concatenate([x.reshape(NL, D_MODEL), ctx.reshape(NC, D_MODEL)], axis=0)
    h1 = _modulate(x_all, mod, 0)

    for layer in range(DEPTH):
        last = layer == DEPTH - 1
        w_main, w_lr = _reorder_w_in(w_in[layer])
        u = _matmul(h1, w_main, 1088, 1024, "proj_in")
        lr = _matmul(h1, w_lr, 1088, LR_COLS, "proj_decay")

        y_conv = _short_conv(u, conv_w, layer)
        wpad_f, bg_f = _decay_weights(gla_wg_f[layer], gla_bg_f[layer], 0)
        wpad_b, bg_b = _decay_weights(gla_wg_b[layer], gla_bg_b[layer], GLA_RANK)
        o_f, o_b = _gla(u, lr, rope_cos, rope_sin, wpad_f, bg_f, wpad_b, bg_b)
        y_na = _neighbourhood_attention(u, _na_bias_tables(na_rpb[layer]), with_ctx=not last)

        rows = NL if last else NT
        x_mid, h2 = _merge(y_conv, o_f, o_b, u, y_na, x_all, mod, layer, rows,
                           gla_norm_g[layer].reshape(1, GLA_DV), gate_b[layer].reshape(1, N_BRANCH * D_MODEL),
                           w_br_conv[layer].astype(BF), w_br_gla[layer].astype(BF), w_br_na[layer].astype(BF),
                           w_out[layer].astype(BF), ln1_g[layer].reshape(1, D_MODEL),
                           ln1_b[layer].reshape(1, D_MODEL))
        ln_g = ln2_g[layer].reshape(1, D_MODEL)
        ln_b = ln2_b[layer].reshape(1, D_MODEL)
        if layer % 2 == 0:
            j = layer // 2
            x_all, h1 = _dense_ffn(h2, x_mid, mod, layer, ffn_w_gate[j].astype(BF), ffn_w_up[j].astype(BF),
                                   ffn_w_down[j].astype(BF), ln_g, ln_b)
        else:
            x_all = _moe_ffn(h2, x_mid, mod, layer, moe_router[layer // 2], moe_w_gate, moe_w_up, moe_w_down,
                             layer // 2, ln_g, ln_b)
    return x_all.reshape(BATCH, SEQ, D_MODEL)
```

```python
import functools

import numpy as np
import jax
import jax.numpy as jnp
from jax import lax
from jax.experimental import pallas as pl
from jax.experimental.pallas import tpu as pltpu

D_MODEL = 2048
BATCH = 2
SEQ = 4096
DEPTH = 2
GRID_W = 64
CTX_LEN = 256
CONV_DIM = 1024
CONV_K = 3
GLA_HEADS = 4
GLA_DK = 128
GLA_DV = 256
GLA_RANK = 16
GLA_TAU = 16.0
GLA_CHUNK = 64
GLA_QK = GLA_HEADS * GLA_DK
GLA_V = GLA_HEADS * GLA_DV
NA_HEADS = 8
NA_DH = 128
NA_W = NA_HEADS * NA_DH
WIN_R = 8
WIN_C = 16
D_FF = 5632
N_EXPERTS = 8
TOP_K = 2
ROPE_BASE = 10000.0
LN_EPS = 1e-5
N_BRANCH = 3
DEEPNORM_ALPHA = (2 * DEPTH) ** 0.25

NL = BATCH * SEQ
NC = BATCH * CTX_LEN
NT = NL + NC
GRID_ROWS = SEQ // GRID_W

BF = jnp.bfloat16
F32 = jnp.float32

V7X_VMEM_BYTES = 64 * 1024 * 1024
VMEM_LIMIT = V7X_VMEM_BYTES - 8 * 1024 * 1024
NEG_INF = -1e30

U_CONV_B, U_CONV_C, U_CONV_X = 0, 1024, 2048
U_GLA_Q, U_GLA_K, U_GLA_V, U_GLA_R = 3072, 3584, 4096, 5120
U_GATES = 6144
U_COLS = 12288
UN_Q, UN_K, UN_V = 0, 1024, 2048
UN_COLS = 3072
LR_COLS = 128

NA_QROWS = 4
NA_KROWS = NA_QROWS + WIN_R - 1
NA_QBLK = NA_QROWS * GRID_W
NA_KBLK = NA_KROWS * GRID_W
NA_NBLK = GRID_ROWS // NA_QROWS

MOE_TM = 512
MOE_GB = 256
MOE_NSLOT = NL * TOP_K + N_EXPERTS * MOE_TM
MOE_NBLK = MOE_NSLOT // MOE_TM
MOE_NGB = MOE_NSLOT // MOE_GB
MOE_NTT = NL // MOE_GB
MOE_NPAIR = MOE_NGB + N_EXPERTS * (MOE_NTT - 1)
ROUTER_LANES = 128


def _params(*sem):
    return pltpu.CompilerParams(dimension_semantics=sem, vmem_limit_bytes=VMEM_LIMIT)


def _dot(a, b):
    return jnp.dot(a, b, preferred_element_type=F32)


def _dot_nt(a, b):
    return lax.dot_general(a, b, (((1,), (1,)), ((), ())), preferred_element_type=F32)


def _dot_tn(a, b):
    return lax.dot_general(a, b, (((0,), (0,)), ((), ())), preferred_element_type=F32)


def _silu(x):
    return x * jax.nn.sigmoid(x)


def _layer_norm(x, g, b):
    mu = jnp.mean(x, axis=-1, keepdims=True)
    xc = x - mu
    var = jnp.mean(xc * xc, axis=-1, keepdims=True)
    return xc * lax.rsqrt(var + LN_EPS) * g + b


def _mod_row_of_tile(tile_rows):
    per_batch = SEQ // tile_rows
    return lambda i: jnp.minimum(i // per_batch, BATCH)


def _mod_spec(layer, k, row_of_tile):
    return pl.BlockSpec((None, 1, D_MODEL),
                        lambda i, *_: (layer * 48 + row_of_tile(i) * 6 + k, 0, 0))


def _ada_kernel(c_ref, w_ref, b_ref, o_ref):
    a = _silu(c_ref[...]).astype(BF)
    o_ref[...] = _dot(a, w_ref[...].astype(BF)) + b_ref[...]


def _ada_table(cvec, ada_w, ada_b):
    tn = 1024
    out = pl.pallas_call(
        _ada_kernel,
        out_shape=jax.ShapeDtypeStruct((DEPTH, 8, 6 * D_MODEL), F32),
        grid=(DEPTH, 6 * D_MODEL // tn),
        in_specs=[pl.BlockSpec((8, D_MODEL), lambda l, j: (0, 0)),
                  pl.BlockSpec((None, D_MODEL, tn), lambda l, j: (l, 0, j)),
                  pl.BlockSpec((None, 1, tn), lambda l, j: (l, 0, j))],
        out_specs=pl.BlockSpec((None, 8, tn), lambda l, j: (l, 0, j)),
        compiler_params=_params("parallel", "parallel"),
        name="ada_table",
    )(cvec, ada_w, ada_b.reshape(DEPTH, 1, 6 * D_MODEL))
    return out.reshape(DEPTH * 8 * 6, 1, D_MODEL)


def _modulate_kernel(x_ref, sh_ref, sc_ref, o_ref):
    o_ref[...] = (x_ref[...] * (1.0 + sc_ref[...]) + sh_ref[...]).astype(BF)


def _modulate(x_all, mod, layer):
    tm = 512
    rot = _mod_row_of_tile(tm)
    return pl.pallas_call(
        _modulate_kernel,
        out_shape=jax.ShapeDtypeStruct((NT, D_MODEL), BF),
        grid=(NT // tm,),
        in_specs=[pl.BlockSpec((tm, D_MODEL), lambda i: (i, 0)),
                  _mod_spec(layer, 0, rot), _mod_spec(layer, 1, rot)],
        out_specs=pl.BlockSpec((tm, D_MODEL), lambda i: (i, 0)),
        compiler_params=_params("parallel"),
        name="modulate",
    )(x_all, mod, mod)


def _mm_kernel(a_ref, w_ref, o_ref, *, first_tile_scale):
    acc = _dot(a_ref[...], w_ref[...])
    if first_tile_scale is not None:
        acc = acc * jnp.where(pl.program_id(0) == 0, first_tile_scale, 1.0)
    o_ref[...] = acc.astype(o_ref.dtype)


def _matmul(a, w, tm, tn, name, out_dtype=F32, first_tile_scale=None):
    m, k = a.shape
    n = w.shape[1]
    return pl.pallas_call(
        functools.partial(_mm_kernel, first_tile_scale=first_tile_scale),
        out_shape=jax.ShapeDtypeStruct((m, n), out_dtype),
        grid=(n // tn, m // tm),
        in_specs=[pl.BlockSpec((tm, k), lambda j, i: (i, 0)),
                  pl.BlockSpec((k, tn), lambda j, i: (0, j))],
        out_specs=pl.BlockSpec((tm, tn), lambda j, i: (i, j)),
        compiler_params=_params("parallel", "parallel"),
        name=name,
    )(a, w)


CONV_TM = 256


def _conv_kernel(b_ref, c_ref, x_ref, cp_ref, xp_ref, cn_ref, xn_ref, w_ref, o_ref):
    i = pl.program_id(0)
    tiles_per_seq = SEQ // CONV_TM
    is_ctx = i >= NL // CONV_TM
    is_start = jnp.logical_or(i % tiles_per_seq == 0, is_ctx)
    is_end = jnp.logical_or(i % tiles_per_seq == tiles_per_seq - 1, is_ctx)
    z = c_ref[...] * x_ref[...]
    zp = cp_ref[...] * xp_ref[...]
    zn = cn_ref[...] * xn_ref[...]
    prev_row = jnp.where(is_start, 0.0, zp[7:8, :])
    next_row = jnp.where(is_end, 0.0, zn[0:1, :])
    rows = lax.broadcasted_iota(jnp.int32, z.shape, 0)
    z_prev = jnp.where(rows == 0, prev_row, pltpu.roll(z, 1, 0))
    z_next = jnp.where(rows == CONV_TM - 1, next_row, pltpu.roll(z, CONV_TM - 1, 0))
    w = w_ref[...]
    y = w[0:1, :] * z_prev + w[1:2, :] * z + w[2:3, :] * z_next
    o_ref[...] = (b_ref[...] * y).astype(BF)


def _short_conv(u, conv_w, layer):
    tm = CONV_TM
    n8 = tm // 8
    last8 = NT // 8 - 1
    return pl.pallas_call(
        _conv_kernel,
        out_shape=jax.ShapeDtypeStruct((NT, CONV_DIM), BF),
        grid=(NT // tm,),
        in_specs=[pl.BlockSpec((tm, CONV_DIM), lambda i: (i, U_CONV_B // CONV_DIM)),
                  pl.BlockSpec((tm, CONV_DIM), lambda i: (i, U_CONV_C // CONV_DIM)),
                  pl.BlockSpec((tm, CONV_DIM), lambda i: (i, U_CONV_X // CONV_DIM)),
                  pl.BlockSpec((8, CONV_DIM), lambda i: (jnp.maximum(i * n8 - 1, 0), U_CONV_C // CONV_DIM)),
                  pl.BlockSpec((8, CONV_DIM), lambda i: (jnp.maximum(i * n8 - 1, 0), U_CONV_X // CONV_DIM)),
                  pl.BlockSpec((8, CONV_DIM), lambda i: (jnp.minimum(i * n8 + n8, last8), U_CONV_C // CONV_DIM)),
                  pl.BlockSpec((8, CONV_DIM), lambda i: (jnp.minimum(i * n8 + n8, last8), U_CONV_X // CONV_DIM)),
                  pl.BlockSpec((None, CONV_K, CONV_DIM), lambda i: (layer, 0, 0))],
        out_specs=pl.BlockSpec((tm, CONV_DIM), lambda i: (i, 0)),
        compiler_params=_params("parallel"),
        name="short_conv",
    )(u, u, u, u, u, u, u, conv_w)


GLA_STEPS = (CTX_LEN + SEQ) // GLA_CHUNK
GLA_CTX_STEPS = CTX_LEN // GLA_CHUNK


def _log_sigmoid(z):
    return -(jnp.maximum(-z, 0.0) + jnp.log1p(jnp.exp(-jnp.abs(z))))


def _rope(x, cs, sn):
    lane = lax.broadcasted_iota(jnp.int32, x.shape, 1)
    swapped = jnp.where(lane % 2 == 0, pltpu.roll(x, GLA_DK - 1, 1), pltpu.roll(x, 1, 1))
    return x * cs + swapped * sn


def _gla_decayed_operands(q_ref, k_ref, l_ref, r_ref, w_ref, b_ref, reverse):
    c = GLA_CHUNK
    z = _dot(l_ref[...].astype(BF), w_ref[...]) + b_ref[...]
    g = _log_sigmoid(z) * (1.0 / GLA_TAU)
    t_i = lax.broadcasted_iota(jnp.int32, (c, c), 0)
    s_i = lax.broadcasted_iota(jnp.int32, (c, c), 1)
    keep = (s_i >= t_i) if reverse else (s_i <= t_i)
    tri = jnp.where(keep, 1.0, 0.0).astype(BF)
    g1 = g.astype(BF)
    r1 = g - g1.astype(F32)
    g2 = r1.astype(BF)
    g3 = (r1 - g2.astype(F32)).astype(BF)
    b = _dot(tri, g1) + _dot(tri, g2) + _dot(tri, g3)
    total = b[0:1, :] if reverse else b[c - 1:c, :]
    cs = r_ref[:, :GLA_DK]
    sn = r_ref[:, GLA_DK:]
    heads = [slice(h * GLA_DK, (h + 1) * GLA_DK) for h in range(GLA_HEADS)]
    qs = jnp.concatenate([_rope(q_ref[:, hd] * (GLA_DK ** -0.5), cs, sn) for hd in heads], axis=1)
    kr = jnp.concatenate([_rope(k_ref[:, hd], cs, sn) for hd in heads], axis=1)
    q_dec = (qs * jnp.exp(b)).astype(BF)
    k_inv = (kr * jnp.exp(-b)).astype(BF)
    k_end = (kr * jnp.exp(total - b)).astype(BF)
    return keep, q_dec, k_inv, k_end, jnp.exp(total)


def _gla_kernel(qf_ref, kf_ref, vf_ref, lf_ref, rf_ref, qb_ref, kb_ref, vb_ref, lb_ref, rb_ref,
                wf_ref, bf_ref, wb_ref, bb_ref, of_ref, ob_ref, stf_ref, stb_ref):
    @pl.when(pl.program_id(1) == 0)
    def _():
        stf_ref[...] = jnp.zeros_like(stf_ref)
        stb_ref[...] = jnp.zeros_like(stb_ref)

    pre = [_gla_decayed_operands(qf_ref, kf_ref, lf_ref, rf_ref, wf_ref, bf_ref, False),
           _gla_decayed_operands(qb_ref, kb_ref, lb_ref, rb_ref, wb_ref, bb_ref, True)]
    chains = [(d, h) for d in range(2) for h in range(GLA_HEADS)]
    v_refs, o_refs, st_refs = (vf_ref, vb_ref), (of_ref, ob_ref), (stf_ref, stb_ref)

    def head(x, h):
        return x[:, h * GLA_DK:(h + 1) * GLA_DK]

    v = [v_refs[d][:, h * GLA_DV:(h + 1) * GLA_DV].astype(BF) for d, h in chains]
    st = [st_refs[d][h] for d, h in chains]
    att = [jnp.where(pre[d][0], _dot_nt(head(pre[d][1], h), head(pre[d][2], h)), 0.0).astype(BF)
           for d, h in chains]
    o = [_dot(att[i], v[i]) + _dot_nt(head(pre[d][1], h), st[i].astype(BF)) for i, (d, h) in enumerate(chains)]
    for d in range(2):
        o_refs[d][...] = jnp.concatenate(o[d * GLA_HEADS:(d + 1) * GLA_HEADS], axis=1)
    for i, (d, h) in enumerate(chains):
        st_refs[d][h] = st[i] * head(pre[d][4], h) + _dot_tn(v[i], head(pre[d][3], h))


def _gla(u, lr, rope_tab, wpad_f, bg_f, wpad_b, bg_b):
    c = GLA_CHUNK
    lat_chunks = SEQ // c
    ctx_base = NL // c

    def row_f(b, s):
        return jnp.where(s < GLA_CTX_STEPS, ctx_base + b * GLA_CTX_STEPS + s, b * lat_chunks + s - GLA_CTX_STEPS)

    def row_b(b, s):
        return jnp.where(s < GLA_CTX_STEPS, ctx_base + b * GLA_CTX_STEPS + GLA_CTX_STEPS - 1 - s,
                         b * lat_chunks + GLA_STEPS - 1 - s)

    def rope_f(s):
        return jnp.where(s < GLA_CTX_STEPS, lat_chunks, s - GLA_CTX_STEPS)

    def rope_b(s):
        return jnp.where(s < GLA_CTX_STEPS, lat_chunks, GLA_STEPS - 1 - s)

    def stream(row, rope):
        return [pl.BlockSpec((c, GLA_QK), lambda b, s: (row(b, s), U_GLA_Q // GLA_QK)),
                pl.BlockSpec((c, GLA_QK), lambda b, s: (row(b, s), U_GLA_K // GLA_QK)),
                pl.BlockSpec((c, GLA_V), lambda b, s: (row(b, s), U_GLA_V // GLA_V)),
                pl.BlockSpec((c, LR_COLS), lambda b, s: (row(b, s), 0)),
                pl.BlockSpec((c, 2 * GLA_DK), lambda b, s: (rope(s), 0))]

    head_w = pl.BlockSpec((LR_COLS, GLA_QK), lambda b, s: (0, 0))
    head_b = pl.BlockSpec((1, GLA_QK), lambda b, s: (0, 0))
    state = pltpu.VMEM((GLA_HEADS, GLA_DV, GLA_DK), F32)
    return pl.pallas_call(
        _gla_kernel,
        out_shape=(jax.ShapeDtypeStruct((NT, GLA_V), F32), jax.ShapeDtypeStruct((NT, GLA_V), F32)),
        grid=(BATCH, GLA_STEPS),
        in_specs=stream(row_f, rope_f) + stream(row_b, rope_b) + [head_w, head_b, head_w, head_b],
        out_specs=(pl.BlockSpec((c, GLA_V), lambda b, s: (row_f(b, s), 0)),
                   pl.BlockSpec((c, GLA_V), lambda b, s: (row_b(b, s), 0))),
        scratch_shapes=[state, state],
        compiler_params=_params("parallel", "arbitrary"),
        name="gla",
    )(u, u, u, lr, rope_tab, u, u, u, lr, rope_tab, wpad_f, bg_f, wpad_b, bg_b)


def _rope_table():
    t = jnp.arange(SEQ)
    rows = (t // GRID_W).astype(F32)
    cols = (t % GRID_W).astype(F32)
    n_freq = GLA_DK // 4
    inv = 1.0 / (ROPE_BASE ** (jnp.arange(n_freq, dtype=F32) / n_freq))
    ang = jnp.concatenate([rows[:, None] * inv, cols[:, None] * inv], -1)
    cos = jnp.repeat(jnp.cos(ang), 2, axis=-1)
    sin = jnp.repeat(jnp.sin(ang), 2, axis=-1)
    sign = jnp.tile(jnp.array([-1.0, 1.0], F32), GLA_DK // 2)
    cos = jnp.concatenate([cos, jnp.ones((GRID_W, GLA_DK), F32)], 0)
    sin = jnp.concatenate([sin * sign, jnp.zeros((GRID_W, GLA_DK), F32)], 0)
    return jnp.concatenate([cos, sin], axis=1)


def _decay_weights(wg, bg, lane_offset):
    w = jnp.pad(wg, ((lane_offset, LR_COLS - GLA_RANK - lane_offset), (0, 0))).astype(BF)
    return w, bg.reshape(1, GLA_QK)


NA_HPS = 2


def _na_softmax_out(parts):
    m = functools.reduce(jnp.maximum, [jnp.max(s, axis=-1, keepdims=True) for s, _ in parts])
    ps = [jnp.exp(s - m) for s, _ in parts]
    l = functools.reduce(jnp.add, [jnp.sum(p, axis=-1, keepdims=True) for p in ps])
    o = functools.reduce(jnp.add, [_dot(p.astype(BF), v) for p, (_, v) in zip(ps, parts)])
    return (o / l).astype(BF)


def _na_kernel(q_ref, k_ref, v_ref, kc_ref, vc_ref, bias_ref, o_ref):
    j = pl.program_id(2)
    head_cols = [slice(h * NA_DH, (h + 1) * NA_DH) for h in range(NA_HPS)]

    @pl.when(j < NA_NBLK)
    def _():
        base = jnp.clip(NA_QROWS * j - WIN_R // 2, 0, GRID_ROWS - NA_KROWS)
        start = pl.multiple_of(base * GRID_W, GRID_W)
        for h, hc in enumerate(head_cols):
            q = q_ref[:, hc]
            vl = v_ref[pl.ds(start, NA_KBLK), hc]
            s_loc = _dot_nt(q, k_ref[pl.ds(start, NA_KBLK), hc]) + bias_ref[h]
            s_ctx = _dot_nt(q, kc_ref[:, hc])
            o_ref[:, hc] = _na_softmax_out([(s_loc, vl), (s_ctx, vc_ref[:, hc])])

    @pl.when(j == NA_NBLK)
    def _():
        for hc in head_cols:
            s_ctx = _dot_nt(q_ref[:, hc], kc_ref[:, hc])
            o_ref[:, hc] = _na_softmax_out([(s_ctx, vc_ref[:, hc])])


def _na_bias_tables(rpb):
    qc = np.arange(GRID_W)
    cs = np.clip(qc - WIN_C // 2, 0, GRID_W - WIN_C)
    kcol = np.arange(GRID_W)
    col_ok = (kcol[None, :] >= cs[:, None]) & (kcol[None, :] < cs[:, None] + WIN_C)
    pad = GRID_W - WIN_C
    padded = jnp.pad(rpb, ((0, 0), (0, 0), (pad, pad)))
    tm = jnp.stack([padded[:, :, GRID_W - 1 - q:2 * GRID_W - 1 - q] for q in range(GRID_W)], axis=2)
    tm = jnp.where(jnp.asarray(col_ok)[None, None], tm, NEG_INF)
    neg = jnp.full((NA_HEADS, GRID_W, GRID_W), NEG_INF, F32)
    classes = []
    for j in (0, 1, NA_NBLK - 1):
        base = int(np.clip(NA_QROWS * j - WIN_R // 2, 0, GRID_ROWS - NA_KROWS))
        q_rows = []
        for qr in range(NA_QROWS):
            r = NA_QROWS * j + qr
            rs = int(np.clip(r - WIN_R // 2, 0, GRID_ROWS - WIN_R))
            k_blocks = []
            for kr in range(NA_KROWS):
                a = base + kr
                if rs <= a < rs + WIN_R:
                    k_blocks.append(tm[:, a - r + WIN_R - 1])
                else:
                    k_blocks.append(neg)
            q_rows.append(jnp.concatenate(k_blocks, axis=-1))
        classes.append(jnp.concatenate(q_rows, axis=1))
    return jnp.stack(classes, axis=1)


def _neighbourhood_attention(un, bias, with_ctx):
    steps = NA_NBLK + (1 if with_ctx else 0)
    ctx_q = NL // NA_QBLK
    w = NA_HPS * NA_DH

    def q_row(b, j):
        return jnp.where(j < NA_NBLK, b * NA_NBLK + j, ctx_q + b)

    def bias_class(j):
        return jnp.where(j == 0, 0, jnp.where(j >= NA_NBLK - 1, 2, 1))

    ctx_blk = NL // CTX_LEN
    return pl.pallas_call(
        _na_kernel,
        out_shape=jax.ShapeDtypeStruct((NT, NA_W), BF),
        grid=(BATCH, NA_HEADS // NA_HPS, steps),
        in_specs=[pl.BlockSpec((NA_QBLK, w), lambda b, h, j: (q_row(b, j), UN_Q // w + h)),
                  pl.BlockSpec((SEQ, w), lambda b, h, j: (b, UN_K // w + h)),
                  pl.BlockSpec((SEQ, w), lambda b, h, j: (b, UN_V // w + h)),
                  pl.BlockSpec((CTX_LEN, w), lambda b, h, j: (ctx_blk + b, UN_K // w + h)),
                  pl.BlockSpec((CTX_LEN, w), lambda b, h, j: (ctx_blk + b, UN_V // w + h)),
                  pl.BlockSpec((NA_HPS, None, NA_QBLK, NA_KBLK), lambda b, h, j: (h, bias_class(j), 0, 0))],
        out_specs=pl.BlockSpec((NA_QBLK, w), lambda b, h, j: (q_row(b, j), h)),
        compiler_params=_params("parallel", "parallel", "arbitrary"),
        name="neighbourhood_attention",
    )(un, un, un, un, un, bias)


MERGE_TM = 256


def _merge_kernel(yc_ref, of_ref, ob_ref, r_ref, yn_ref, gt_ref, x_ref,
                  ng_ref, gb_ref, wc_ref, wg_ref, wn_ref, wo_ref,
                  g1_ref, lg_ref, lb_ref, sh2_ref, sc2_ref, xo_ref, ho_ref):
    o = of_ref[...] + ob_ref[...]
    r = r_ref[...]
    ng = ng_ref[...]
    heads = []
    for h in range(GLA_HEADS):
        oh = o[:, h * GLA_DV:(h + 1) * GLA_DV]
        oh = oh * lax.rsqrt(jnp.mean(oh * oh, axis=-1, keepdims=True) + LN_EPS)
        heads.append(oh * ng * _silu(r[:, h * GLA_DV:(h + 1) * GLA_DV]))
    y_gla = jnp.concatenate(heads, axis=-1).astype(BF)
    g = jax.nn.sigmoid(gt_ref[...] + gb_ref[...])
    z = (g[:, :D_MODEL] * _dot(yc_ref[...], wc_ref[...])
         + g[:, D_MODEL:2 * D_MODEL] * _dot(y_gla, wg_ref[...])
         + g[:, 2 * D_MODEL:] * _dot(yn_ref[...], wn_ref[...]))
    y = _dot(z.astype(BF), wo_ref[...])
    xn = _layer_norm(DEEPNORM_ALPHA * x_ref[...] + g1_ref[...] * y, lg_ref[...], lb_ref[...])
    xo_ref[...] = xn
    ho_ref[...] = (xn * (1.0 + sc2_ref[...]) + sh2_ref[...]).astype(BF)


def _merge(y_conv, o_f, o_b, u, y_na, x_all, mod, layer, rows, norm_g, gate_b, w_conv, w_gla, w_na, w_out,
           ln_g, ln_b):
    tm = MERGE_TM
    rot = _mod_row_of_tile(tm)
    row = lambda width: pl.BlockSpec((tm, width), lambda i: (i, 0))
    const = lambda shape: pl.BlockSpec(shape, lambda i: (0,) * len(shape), pipeline_mode=pl.Buffered(1))
    return pl.pallas_call(
        _merge_kernel,
        out_shape=(jax.ShapeDtypeStruct((rows, D_MODEL), F32), jax.ShapeDtypeStruct((rows, D_MODEL), BF)),
        grid=(rows // tm,),
        in_specs=[row(CONV_DIM), row(GLA_V), row(GLA_V),
                  pl.BlockSpec((tm, GLA_V), lambda i: (i, U_GLA_R // GLA_V)),
                  row(NA_W),
                  pl.BlockSpec((tm, N_BRANCH * D_MODEL), lambda i: (i, U_GATES // (N_BRANCH * D_MODEL))),
                  row(D_MODEL),
                  const((1, GLA_DV)), const((1, N_BRANCH * D_MODEL)),
                  const((CONV_DIM, D_MODEL)), const((GLA_V, D_MODEL)), const((NA_W, D_MODEL)),
                  const((D_MODEL, D_MODEL)),
                  _mod_spec(layer, 2, rot), const((1, D_MODEL)), const((1, D_MODEL)),
                  _mod_spec(layer, 3, rot), _mod_spec(layer, 4, rot)],
        out_specs=(row(D_MODEL), row(D_MODEL)),
        compiler_params=_params("parallel"),
        name="merge",
    )(y_conv, o_f, o_b, u, y_na, u, x_all, norm_g, gate_b, w_conv, w_gla, w_na, w_out,
      mod, ln_g, ln_b, mod, mod)


FFN_TM = 512
FFN_TF = 512


def _ffn_kernel(h_ref, wg_ref, wu_ref, wd_ref, x_ref, g2_ref, lg_ref, lb_ref, shn_ref, scn_ref,
                xo_ref, ho_ref, acc_ref):
    j = pl.program_id(1)

    @pl.when(j == 0)
    def _():
        acc_ref[...] = jnp.zeros_like(acc_ref)

    h = h_ref[...]
    a = _silu(_dot(h, wg_ref[...])) * _dot(h, wu_ref[...])
    acc_ref[...] += _dot(a.astype(BF), wd_ref[...])

    @pl.when(j == pl.num_programs(1) - 1)
    def _():
        xn = _layer_norm(DEEPNORM_ALPHA * x_ref[...] + g2_ref[...] * acc_ref[...], lg_ref[...], lb_ref[...])
        xo_ref[...] = xn
        ho_ref[...] = (xn * (1.0 + scn_ref[...]) + shn_ref[...]).astype(BF)


def _dense_ffn(h2, x_all, mod, layer, wg, wu, wd, ln_g, ln_b):
    tm, tf = FFN_TM, FFN_TF
    rot = _mod_row_of_tile(tm)
    row = lambda: pl.BlockSpec((tm, D_MODEL), lambda i, j: (i, 0))
    const = lambda: pl.BlockSpec((1, D_MODEL), lambda i, j: (0, 0))
    return pl.pallas_call(
        _ffn_kernel,
        out_shape=(jax.ShapeDtypeStruct((NT, D_MODEL), F32), jax.ShapeDtypeStruct((NT, D_MODEL), BF)),
        grid=(NT // tm, D_FF // tf),
        in_specs=[row(),
                  pl.BlockSpec((D_MODEL, tf), lambda i, j: (0, j)),
                  pl.BlockSpec((D_MODEL, tf), lambda i, j: (0, j)),
                  pl.BlockSpec((tf, D_MODEL), lambda i, j: (j, 0)),
                  row(), _mod_spec(layer, 5, rot), const(), const(),
                  _mod_spec(layer + 1, 0, rot), _mod_spec(layer + 1, 1, rot)],
        out_specs=(row(), row()),
        scratch_shapes=[pltpu.VMEM((tm, D_MODEL), F32)],
        compiler_params=_params("parallel", "arbitrary"),
        name="dense_ffn",
    )(h2, wg, wu, wd, x_all, mod, ln_g, ln_b, mod, mod)


def _router_kernel(h_ref, w_ref, idx_ref, wt_ref):
    logits = _dot(h_ref[...], w_ref[...])
    lane = lax.broadcasted_iota(jnp.int32, logits.shape, 1)
    logits = jnp.where(lane < N_EXPERTS, logits, -jnp.inf)
    m1 = jnp.max(logits, axis=-1, keepdims=True)
    i1 = jnp.min(jnp.where(logits == m1, lane, ROUTER_LANES), axis=-1, keepdims=True)
    rest = jnp.where(lane == i1, -jnp.inf, logits)
    m2 = jnp.max(rest, axis=-1, keepdims=True)
    i2 = jnp.min(jnp.where(rest == m2, lane, ROUTER_LANES), axis=-1, keepdims=True)
    e2 = jnp.exp(m2 - m1)
    w1 = 1.0 / (1.0 + e2)
    w2 = e2 / (1.0 + e2)
    idx_ref[...] = jnp.where(lane == 0, i1, jnp.where(lane == 1, i2, 0))
    wt_ref[...] = jnp.where(lane == 0, w1, jnp.where(lane == 1, w2, 0.0))


def _router(h2, router_w):
    tm = 512
    return pl.pallas_call(
        _router_kernel,
        out_shape=(jax.ShapeDtypeStruct((NL, ROUTER_LANES), jnp.int32),
                   jax.ShapeDtypeStruct((NL, ROUTER_LANES), F32)),
        grid=(NL // tm,),
        in_specs=[pl.BlockSpec((tm, D_MODEL), lambda i: (i, 0)),
                  pl.BlockSpec((D_MODEL, ROUTER_LANES), lambda i: (0, 0))],
        out_specs=(pl.BlockSpec((tm, ROUTER_LANES), lambda i: (i, 0)),
                   pl.BlockSpec((tm, ROUTER_LANES), lambda i: (i, 0))),
        compiler_params=_params("parallel"),
        name="moe_router",
    )(h2, router_w)


def _dispatch_plan(idx):
    i32 = jnp.int32
    experts = jnp.arange(N_EXPERTS, dtype=i32)
    oh_a = (idx[:, 0:1] == experts).astype(i32)
    oh_b = (idx[:, 1:2] == experts).astype(i32)
    cum = jnp.cumsum(oh_a + oh_b, axis=0)
    counts = cum[-1]
    region = ((counts + MOE_TM - 1) // MOE_TM) * MOE_TM
    g_end = jnp.cumsum(region)
    g_start = g_end - region
    slot_table = g_start[None, :] + cum - 1
    slot_a = jnp.sum(oh_a * slot_table, axis=1)
    slot_b = jnp.sum(oh_b * slot_table, axis=1)

    tile_start = jnp.arange(MOE_NBLK, dtype=i32) * MOE_TM
    tile_expert = jnp.minimum(jnp.sum((tile_start[:, None] >= g_end[None, :]).astype(i32), axis=1),
                              N_EXPERTS - 1)
    n_tiles = jnp.maximum(g_end[-1] // MOE_TM, 1).reshape(1)
    tile_expert = (tile_expert, n_tiles)

    pair_ids = jnp.arange(MOE_NPAIR, dtype=i32)
    tile_last = jnp.arange(MOE_NTT, dtype=i32) * MOE_GB + MOE_GB - 1
    hi = cum[tile_last]
    lo = jnp.concatenate([jnp.zeros((1, N_EXPERTS), i32), hi[:-1]], axis=0)

    blk_start = jnp.arange(MOE_NGB, dtype=i32) * MOE_GB
    blk_e = jnp.minimum(jnp.sum((blk_start[:, None] >= g_end[None, :]).astype(i32), axis=1), N_EXPERTS - 1)
    r0 = blk_start - g_start[blk_e]
    r1 = jnp.minimum(r0 + MOE_GB, counts[blk_e])
    has = r1 > r0
    hi_e = hi[:, blk_e]
    t_lo = jnp.sum((hi_e <= r0[None, :]).astype(i32), axis=0)
    t_hi = jnp.sum((hi_e < r1[None, :]).astype(i32), axis=0)
    t_lo = jnp.where(has, t_lo, 0)
    n_d = jnp.where(has, t_hi - t_lo + 1, 1)
    d_end = jnp.cumsum(n_d)
    d_total = d_end[-1]
    d_blk = jnp.minimum(jnp.sum((pair_ids[:, None] >= d_end[None, :]).astype(i32), axis=1), MOE_NGB - 1)
    d_tile = jnp.clip(t_lo[d_blk] + pair_ids - (d_end[d_blk] - n_d[d_blk]), 0, MOE_NTT - 1)
    d_valid = pair_ids < d_total
    d_prev = jnp.concatenate([jnp.full((1,), -1, i32), d_blk[:-1]])
    d_next = jnp.concatenate([d_blk[1:], jnp.full((1,), -1, i32)])
    d_first = d_valid & (d_blk != d_prev)
    d_last = d_valid & ((d_blk != d_next) | (pair_ids == d_total - 1))
    d_tile = jnp.where(d_valid, d_tile, d_tile[jnp.maximum(d_total - 1, 0)])
    dispatch = (d_blk, d_tile, d_first.astype(i32), d_last.astype(i32), d_valid.astype(i32))

    b_lo = (g_start[None, :] + lo) // MOE_GB
    b_hi = (g_start[None, :] + hi - 1) // MOE_GB
    n_c = jnp.where(hi > lo, b_hi - b_lo + 1, 0).reshape(-1)
    b_lo = b_lo.reshape(-1)
    c_end = jnp.cumsum(n_c)
    c_total = c_end[-1]
    c_idx = jnp.minimum(jnp.sum((pair_ids[:, None] >= c_end[None, :]).astype(i32), axis=1),
                        MOE_NTT * N_EXPERTS - 1)
    c_blk = jnp.clip(b_lo[c_idx] + pair_ids - (c_end[c_idx] - n_c[c_idx]), 0, MOE_NGB - 1)
    c_tile = c_idx // N_EXPERTS
    c_valid = pair_ids < c_total
    c_tile = jnp.where(c_valid, c_tile, MOE_NTT - 1)
    c_blk = jnp.where(c_valid, c_blk, c_blk[jnp.maximum(c_total - 1, 0)])
    c_prev = jnp.concatenate([jnp.full((1,), -1, i32), c_tile[:-1]])
    c_next = jnp.concatenate([c_tile[1:], jnp.full((1,), -1, i32)])
    c_first = c_valid & (c_tile != c_prev)
    c_last = c_valid & ((c_tile != c_next) | (pair_ids == c_total - 1))
    combine = (c_tile, c_blk, c_first.astype(i32), c_last.astype(i32), c_valid.astype(i32))
    return slot_a, slot_b, tile_expert, dispatch, combine


def _dispatch_kernel(blk_ref, tile_ref, first_ref, last_ref, valid_ref,
                     h_ref, sa_ref, sb_ref, wa_ref, wb_ref, hs_ref, ws_ref, acc_ref, wacc_ref):
    p = pl.program_id(0)

    @pl.when(first_ref[p] == 1)
    def _():
        acc_ref[...] = jnp.zeros_like(acc_ref)
        wacc_ref[...] = jnp.zeros_like(wacc_ref)

    @pl.when(valid_ref[p] == 1)
    def _():
        slots = blk_ref[p] * MOE_GB + lax.broadcasted_iota(jnp.int32, (MOE_GB, MOE_GB), 0)
        hit_a = slots == sa_ref[...]
        hit_b = slots == sb_ref[...]
        onehot = jnp.where(hit_a, 1.0, jnp.where(hit_b, 1.0, 0.0)).astype(BF)
        acc_ref[...] += _dot(onehot, h_ref[...])
        w = jnp.where(hit_a, wa_ref[...], jnp.where(hit_b, wb_ref[...], 0.0))
        wacc_ref[...] += jnp.broadcast_to(jnp.sum(w, axis=1, keepdims=True), wacc_ref.shape)

    @pl.when(last_ref[p] == 1)
    def _():
        hs_ref[...] = acc_ref[...].astype(BF)
        ws_ref[...] = wacc_ref[...]


def _dispatch(h2, slot_a, slot_b, w_a, w_b, plan):
    tok = lambda p, blk, tile, *_: (tile[p], 0, 0)
    grid_spec = pltpu.PrefetchScalarGridSpec(
        num_scalar_prefetch=5,
        grid=(MOE_NPAIR,),
        in_specs=[pl.BlockSpec((MOE_GB, D_MODEL), lambda p, blk, tile, *_: (tile[p], 0)),
                  pl.BlockSpec((None, 1, MOE_GB), tok), pl.BlockSpec((None, 1, MOE_GB), tok),
                  pl.BlockSpec((None, 1, MOE_GB), tok), pl.BlockSpec((None, 1, MOE_GB), tok)],
        out_specs=(pl.BlockSpec((MOE_GB, D_MODEL), lambda p, blk, *_: (blk[p], 0)),
                   pl.BlockSpec((MOE_GB, ROUTER_LANES), lambda p, blk, *_: (blk[p], 0))),
        scratch_shapes=[pltpu.VMEM((MOE_GB, D_MODEL), F32), pltpu.VMEM((MOE_GB, ROUTER_LANES), F32)])
    shape3 = (MOE_NTT, 1, MOE_GB)
    return pl.pallas_call(
        _dispatch_kernel,
        out_shape=(jax.ShapeDtypeStruct((MOE_NSLOT, D_MODEL), BF),
                   jax.ShapeDtypeStruct((MOE_NSLOT, ROUTER_LANES), F32)),
        grid_spec=grid_spec,
        compiler_params=_params("arbitrary"),
        name="moe_dispatch",
    )(*plan, h2, slot_a.reshape(shape3), slot_b.reshape(shape3), w_a.reshape(shape3), w_b.reshape(shape3))


def _tile_flags(e_ref, n_ref):
    i = pl.program_id(1)
    valid = i < n_ref[0]
    changed = jnp.logical_or(i == 0, e_ref[i] != e_ref[jnp.maximum(i - 1, 0)])
    return valid, jnp.logical_and(valid, changed)


def _used_tile(i, n_ref):
    return jnp.minimum(i, n_ref[0] - 1)


def _expert_up_kernel(e_ref, n_ref, hs_ref, wg_ref, wu_ref, a_ref, wgs_ref, wus_ref):
    valid, changed = _tile_flags(e_ref, n_ref)

    @pl.when(changed)
    def _():
        wgs_ref[...] = wg_ref[...].astype(BF)
        wus_ref[...] = wu_ref[...].astype(BF)

    @pl.when(valid)
    def _():
        h = hs_ref[...]
        a_ref[...] = (_silu(_dot(h, wgs_ref[...])) * _dot(h, wus_ref[...])).astype(BF)


def _expert_up(hs, tile_plan, wg, wu, moe_layer):
    tf = 512
    w_spec = pl.BlockSpec((None, None, D_MODEL, tf), lambda j, i, e, n: (moe_layer, e[_used_tile(i, n)], 0, j))
    grid_spec = pltpu.PrefetchScalarGridSpec(
        num_scalar_prefetch=2,
        grid=(D_FF // tf, MOE_NBLK),
        in_specs=[pl.BlockSpec((MOE_TM, D_MODEL), lambda j, i, e, n: (_used_tile(i, n), 0)), w_spec, w_spec],
        out_specs=pl.BlockSpec((MOE_TM, tf), lambda j, i, e, n: (_used_tile(i, n), j)),
        scratch_shapes=[pltpu.VMEM((D_MODEL, tf), BF), pltpu.VMEM((D_MODEL, tf), BF)])
    return pl.pallas_call(
        _expert_up_kernel,
        out_shape=jax.ShapeDtypeStruct((MOE_NSLOT, D_FF), BF),
        grid_spec=grid_spec,
        compiler_params=_params("arbitrary", "arbitrary"),
        name="moe_expert_up",
    )(*tile_plan, hs, wg, wu)


def _expert_down_kernel(e_ref, n_ref, a_ref, wd_ref, ws_ref, yh_ref, yl_ref, wds_ref):
    valid, changed = _tile_flags(e_ref, n_ref)

    @pl.when(changed)
    def _():
        wds_ref[...] = wd_ref[...].astype(BF)

    @pl.when(valid)
    def _():
        y = _dot(a_ref[...], wds_ref[...]) * ws_ref[:, 0:1]
        hi = y.astype(BF)
        yh_ref[...] = hi
        yl_ref[...] = (y - hi.astype(F32)).astype(BF)


def _expert_down(a, tile_plan, wd, ws, moe_layer):
    tn = 512
    out_spec = pl.BlockSpec((MOE_TM, tn), lambda j, i, e, n: (_used_tile(i, n), j))
    grid_spec = pltpu.PrefetchScalarGridSpec(
        num_scalar_prefetch=2,
        grid=(D_MODEL // tn, MOE_NBLK),
        in_specs=[pl.BlockSpec((MOE_TM, D_FF), lambda j, i, e, n: (_used_tile(i, n), 0)),
                  pl.BlockSpec((None, None, D_FF, tn),
                               lambda j, i, e, n: (moe_layer, e[_used_tile(i, n)], 0, j)),
                  pl.BlockSpec((MOE_TM, ROUTER_LANES), lambda j, i, e, n: (_used_tile(i, n), 0))],
        out_specs=(out_spec, out_spec),
        scratch_shapes=[pltpu.VMEM((D_FF, tn), BF)])
    return pl.pallas_call(
        _expert_down_kernel,
        out_shape=(jax.ShapeDtypeStruct((MOE_NSLOT, D_MODEL), BF),
                   jax.ShapeDtypeStruct((MOE_NSLOT, D_MODEL), BF)),
        grid_spec=grid_spec,
        compiler_params=_params("arbitrary", "arbitrary"),
        name="moe_expert_down",
    )(*tile_plan, a, wd, ws)


def _combine_kernel(tile_ref, blk_ref, first_ref, last_ref, valid_ref,
                    yh_ref, yl_ref, sa_ref, sb_ref, x_ref, g2_ref, lg_ref, lb_ref, o_ref, acc_ref):
    p = pl.program_id(0)

    @pl.when(first_ref[p] == 1)
    def _():
        acc_ref[...] = jnp.zeros_like(acc_ref)

    @pl.when(valid_ref[p] == 1)
    def _():
        slots = blk_ref[p] * MOE_GB + lax.broadcasted_iota(jnp.int32, (MOE_GB, MOE_GB), 1)
        onehot = jnp.where(slots == sa_ref[...], 1.0, jnp.where(slots == sb_ref[...], 1.0, 0.0)).astype(BF)
        acc_ref[...] += _dot(onehot, yh_ref[...]) + _dot(onehot, yl_ref[...])

    @pl.when(last_ref[p] == 1)
    def _():
        o_ref[...] = _layer_norm(DEEPNORM_ALPHA * x_ref[...] + g2_ref[...] * acc_ref[...],
                                 lg_ref[...], lb_ref[...])


def _combine(y_hi, y_lo, slot_a, slot_b, x_lat, mod, layer, ln_g, ln_b, plan):
    per_batch = SEQ // MOE_GB
    slot_blk = lambda p, tile, blk, *_: (blk[p], 0)
    tok_blk = lambda p, tile, *_: (tile[p], 0)
    grid_spec = pltpu.PrefetchScalarGridSpec(
        num_scalar_prefetch=5,
        grid=(MOE_NPAIR,),
        in_specs=[pl.BlockSpec((MOE_GB, D_MODEL), slot_blk), pl.BlockSpec((MOE_GB, D_MODEL), slot_blk),
                  pl.BlockSpec((MOE_GB, 1), tok_blk), pl.BlockSpec((MOE_GB, 1), tok_blk),
                  pl.BlockSpec((MOE_GB, D_MODEL), tok_blk),
                  pl.BlockSpec((None, 1, D_MODEL),
                               lambda p, tile, *_: (layer * 48 + (tile[p] // per_batch) * 6 + 5, 0, 0)),
                  pl.BlockSpec((1, D_MODEL), lambda p, *_: (0, 0)),
                  pl.BlockSpec((1, D_MODEL), lambda p, *_: (0, 0))],
        out_specs=pl.BlockSpec((MOE_GB, D_MODEL), tok_blk),
        scratch_shapes=[pltpu.VMEM((MOE_GB, D_MODEL), F32)])
    return pl.pallas_call(
        _combine_kernel,
        out_shape=jax.ShapeDtypeStruct((NL, D_MODEL), F32),
        grid_spec=grid_spec,
        compiler_params=_params("arbitrary"),
        name="moe_combine",
    )(*plan, y_hi, y_lo, slot_a.reshape(NL, 1), slot_b.reshape(NL, 1), x_lat, mod, ln_g, ln_b)


def _moe_ffn(h2, x_lat, mod, layer, router_w, wg, wu, wd, moe_layer, ln_g, ln_b):
    router_pad = jnp.pad(router_w, ((0, 0), (0, ROUTER_LANES - N_EXPERTS))).astype(BF)
    idx, wts = _router(h2, router_pad)
    slot_a, slot_b, tile_plan, d_plan, c_plan = _dispatch_plan(idx[:, :TOP_K])
    hs, ws = _dispatch(h2, slot_a, slot_b, wts[:, 0], wts[:, 1], d_plan)
    a = _expert_up(hs, tile_plan, wg, wu, moe_layer)
    y_hi, y_lo = _expert_down(a, tile_plan, wd, ws, moe_layer)
    return _combine(y_hi, y_lo, slot_a, slot_b, x_lat, mod, layer, ln_g, ln_b, c_plan)


def _reorder_w_in(w):
    gla_end = 3 * CONV_DIM + 2 * GLA_QK + 2 * GLA_V
    lr_end = gla_end + 2 * GLA_RANK
    na_end = lr_end + 3 * NA_W
    main = jnp.concatenate([w[:, :gla_end], w[:, na_end:]], axis=1).astype(BF)
    na = w[:, lr_end:na_end].astype(BF)
    lr = jnp.pad(w[:, gla_end:lr_end], ((0, 0), (0, LR_COLS - 2 * GLA_RANK))).astype(BF)
    return main, na, lr


def kernel(x, c, ctx, c_ctx, ada_w, ada_b, w_in, conv_w, gla_wg_f, gla_bg_f, gla_wg_b, gla_bg_b, gla_norm_g, na_rpb, w_br_conv, w_br_gla, w_br_na, gate_b, w_out, ln1_g, ln1_b, ln2_g, ln2_b, ffn_w_gate, ffn_w_up, ffn_w_down, moe_router, moe_w_gate, moe_w_up, moe_w_down):
    assert DEPTH == 2 and x.shape == (BATCH, SEQ, D_MODEL) and ctx.shape == (BATCH, CTX_LEN, D_MODEL)
    cvec = jnp.concatenate([c, c_ctx[None, :], jnp.zeros((8 - BATCH - 1, D_MODEL), F32)], axis=0)
    mod = _ada_table(cvec, ada_w, ada_b)
    rope_tab = _rope_table()
    x_all = jnp.concatenate([x.reshape(NL, D_MODEL), ctx.reshape(NC, D_MODEL)], axis=0)
    h1 = _modulate(x_all, mod, 0)

    for layer in range(DEPTH):
        last = layer == DEPTH - 1
        w_main, w_na, w_lr = _reorder_w_in(w_in[layer])
        u = _matmul(h1, w_main, 1088, 1024, "proj_in")
        un = _matmul(h1, w_na, 1088, NA_W, "proj_na", out_dtype=BF, first_tile_scale=NA_DH ** -0.5)
        lr = _matmul(h1, w_lr, 1088, LR_COLS, "proj_decay")

        y_conv = _short_conv(u, conv_w, layer)
        wpad_f, bg_f = _decay_weights(gla_wg_f[layer], gla_bg_f[layer], 0)
        wpad_b, bg_b = _decay_weights(gla_wg_b[layer], gla_bg_b[layer], GLA_RANK)
        o_f, o_b = _gla(u, lr, rope_tab, wpad_f, bg_f, wpad_b, bg_b)
        y_na = _neighbourhood_attention(un, _na_bias_tables(na_rpb[layer]), with_ctx=not last)

        rows = NL if last else NT
        x_mid, h2 = _merge(y_conv, o_f, o_b, u, y_na, x_all, mod, layer, rows,
                           gla_norm_g[layer].reshape(1, GLA_DV), gate_b[layer].reshape(1, N_BRANCH * D_MODEL),
                           w_br_conv[layer].astype(BF), w_br_gla[layer].astype(BF), w_br_na[layer].astype(BF),
                           w_out[layer].astype(BF), ln1_g[layer].reshape(1, D_MODEL),
                           ln1_b[layer].reshape(1, D_MODEL))
        ln_g = ln2_g[layer].reshape(1, D_MODEL)
        ln_b = ln2_b[layer].reshape(1, D_MODEL)
        if layer % 2 == 0:
            j = layer // 2
            x_all, h1 = _dense_ffn(h2, x_mid, mod, layer, ffn_w_gate[j].astype(BF), ffn_w_up[j].astype(BF),
                                   ffn_w_down[j].astype(BF), ln_g, ln_b)
        else:
            x_all = _moe_ffn(h2, x_mid, mod, layer, moe_router[layer // 2], moe_w_gate, moe_w_up, moe_w_down,
                             layer // 2, ln_g, ln_b)
    return x_all.reshape(BATCH, SEQ, D_MODEL)
```

```python
import functools

import numpy as np
import jax
import jax.numpy as jnp
from jax import lax
from jax.experimental import pallas as pl
from jax.experimental.pallas import tpu as pltpu

D_MODEL = 2048
BATCH = 2
SEQ = 4096
DEPTH = 2
GRID_W = 64
CTX_LEN = 256
CONV_DIM = 1024
CONV_K = 3
GLA_HEADS = 4
GLA_DK = 128
GLA_DV = 256
GLA_RANK = 16
GLA_TAU = 16.0
GLA_CHUNK = 64
GLA_QK = GLA_HEADS * GLA_DK
GLA_V = GLA_HEADS * GLA_DV
NA_HEADS = 8
NA_DH = 128
NA_W = NA_HEADS * NA_DH
WIN_R = 8
WIN_C = 16
D_FF = 5632
N_EXPERTS = 8
TOP_K = 2
ROPE_BASE = 10000.0
LN_EPS = 1e-5
N_BRANCH = 3
DEEPNORM_ALPHA = (2 * DEPTH) ** 0.25

NL = BATCH * SEQ
NC = BATCH * CTX_LEN
NT = NL + NC
GRID_ROWS = SEQ // GRID_W

BF = jnp.bfloat16
F32 = jnp.float32

V7X_VMEM_BYTES = 64 * 1024 * 1024
VMEM_LIMIT = V7X_VMEM_BYTES - 4 * 1024 * 1024
NEG_INF = -1e30

U_CONV_B, U_CONV_C, U_CONV_X = 0, 1024, 2048
U_GLA_Q, U_GLA_K, U_GLA_V, U_GLA_R = 3072, 3584, 4096, 5120
U_COLS = 6144
UN_Q, UN_K, UN_V = 0, 1024, 2048
UN_COLS = 3072
LR_COLS = 128

NA_QROWS = 4
NA_KROWS = NA_QROWS + WIN_R - 1
NA_QBLK = NA_QROWS * GRID_W
NA_KBLK = NA_KROWS * GRID_W
NA_NBLK = GRID_ROWS // NA_QROWS

MOE_TM = 512
MOE_GB = 256
MOE_NSLOT = NL * TOP_K + N_EXPERTS * MOE_TM
MOE_NBLK = MOE_NSLOT // MOE_TM
MOE_NGB = MOE_NSLOT // MOE_GB
MOE_NTT = NL // MOE_GB
MOE_NPAIR = MOE_NGB + N_EXPERTS * (MOE_NTT - 1)
ROUTER_LANES = 128


def _params(*sem):
    return pltpu.CompilerParams(dimension_semantics=sem, vmem_limit_bytes=VMEM_LIMIT)


def _dot(a, b):
    return jnp.dot(a, b, preferred_element_type=F32)


def _dot_nt(a, b):
    return lax.dot_general(a, b, (((1,), (1,)), ((), ())), preferred_element_type=F32)


def _dot_tn(a, b):
    return lax.dot_general(a, b, (((0,), (0,)), ((), ())), preferred_element_type=F32)


def _silu(x):
    return x * jax.nn.sigmoid(x)


def _layer_norm(x, g, b):
    mu = jnp.mean(x, axis=-1, keepdims=True)
    xc = x - mu
    var = jnp.mean(xc * xc, axis=-1, keepdims=True)
    return xc * lax.rsqrt(var + LN_EPS) * g + b


def _mod_row_of_tile(tile_rows):
    per_batch = SEQ // tile_rows
    return lambda i: jnp.minimum(i // per_batch, BATCH)


def _mod_spec(layer, k, row_of_tile):
    return pl.BlockSpec((None, 1, D_MODEL),
                        lambda i, *_: (layer * 48 + row_of_tile(i) * 6 + k, 0, 0))


def _ada_kernel(c_ref, w_ref, b_ref, o_ref):
    a = _silu(c_ref[...]).astype(BF)
    o_ref[...] = _dot(a, w_ref[...].astype(BF)) + b_ref[...]


def _ada_table(cvec, ada_w, ada_b):
    tn = 1024
    out = pl.pallas_call(
        _ada_kernel,
        out_shape=jax.ShapeDtypeStruct((DEPTH, 8, 6 * D_MODEL), F32),
        grid=(DEPTH, 6 * D_MODEL // tn),
        in_specs=[pl.BlockSpec((8, D_MODEL), lambda l, j: (0, 0)),
                  pl.BlockSpec((None, D_MODEL, tn), lambda l, j: (l, 0, j)),
                  pl.BlockSpec((None, 1, tn), lambda l, j: (l, 0, j))],
        out_specs=pl.BlockSpec((None, 8, tn), lambda l, j: (l, 0, j)),
        compiler_params=_params("parallel", "parallel"),
        name="ada_table",
    )(cvec, ada_w, ada_b.reshape(DEPTH, 1, 6 * D_MODEL))
    return out.reshape(DEPTH * 8 * 6, 1, D_MODEL)


def _select_rows(i, n_lat_tiles, xl_ref, xc_ref):
    return jnp.where(i < n_lat_tiles, xl_ref[...], xc_ref[...])


def _split_row_specs(tm):
    n_lat = NL // tm
    return (pl.BlockSpec((tm, D_MODEL), lambda i: (jnp.minimum(i, n_lat - 1), 0)),
            pl.BlockSpec((tm, D_MODEL), lambda i: (jnp.maximum(i - n_lat, 0), 0)))


def _modulate_kernel(xl_ref, xc_ref, sh_ref, sc_ref, o_ref, *, n_lat_tiles):
    x = _select_rows(pl.program_id(0), n_lat_tiles, xl_ref, xc_ref)
    o_ref[...] = (x * (1.0 + sc_ref[...]) + sh_ref[...]).astype(BF)


def _modulate(x_lat, x_ctx, mod, layer):
    tm = 512
    rot = _mod_row_of_tile(tm)
    return pl.pallas_call(
        functools.partial(_modulate_kernel, n_lat_tiles=NL // tm),
        out_shape=jax.ShapeDtypeStruct((NT, D_MODEL), BF),
        grid=(NT // tm,),
        in_specs=[*_split_row_specs(tm), _mod_spec(layer, 0, rot), _mod_spec(layer, 1, rot)],
        out_specs=pl.BlockSpec((tm, D_MODEL), lambda i: (i, 0)),
        compiler_params=_params("parallel"),
        name="modulate",
    )(x_lat, x_ctx, mod, mod)


def _mm_kernel(a_ref, w_ref, o_ref, *, first_tile_scale):
    acc = _dot(a_ref[...], w_ref[...])
    if first_tile_scale is not None:
        acc = acc * jnp.where(pl.program_id(0) == 0, first_tile_scale, 1.0)
    o_ref[...] = acc.astype(o_ref.dtype)


def _matmul(a, w, tm, tn, name, out_dtype=F32, first_tile_scale=None):
    m, k = a.shape
    n = w.shape[1]
    return pl.pallas_call(
        functools.partial(_mm_kernel, first_tile_scale=first_tile_scale),
        out_shape=jax.ShapeDtypeStruct((m, n), out_dtype),
        grid=(n // tn, m // tm),
        in_specs=[pl.BlockSpec((tm, k), lambda j, i: (i, 0)),
                  pl.BlockSpec((k, tn), lambda j, i: (0, j))],
        out_specs=pl.BlockSpec((tm, tn), lambda j, i: (i, j)),
        compiler_params=_params("parallel", "parallel"),
        name=name,
    )(a, w)


def _mm_f32w_kernel(a_ref, w_ref, o_ref, wb_ref):
    @pl.when(pl.program_id(1) == 0)
    def _():
        wb_ref[...] = w_ref[...].astype(BF)

    o_ref[...] = _dot(a_ref[...], wb_ref[...])


def _matmul_layer_weight(a, w_stack, layer, n, tm, tn, name):
    m, k = a.shape
    return pl.pallas_call(
        _mm_f32w_kernel,
        out_shape=jax.ShapeDtypeStruct((m, n), F32),
        grid=(n // tn, m // tm),
        in_specs=[pl.BlockSpec((tm, k), lambda j, i: (i, 0)),
                  pl.BlockSpec((None, k, tn), lambda j, i: (layer, 0, j))],
        out_specs=pl.BlockSpec((tm, tn), lambda j, i: (i, j)),
        scratch_shapes=[pltpu.VMEM((k, tn), BF)],
        compiler_params=_params("arbitrary", "arbitrary"),
        name=name,
    )(a, w_stack)


CONV_TM = 256


def _conv_kernel(b_ref, c_ref, x_ref, cp_ref, xp_ref, cn_ref, xn_ref, w_ref, o_ref):
    i = pl.program_id(0)
    tiles_per_seq = SEQ // CONV_TM
    is_ctx = i >= NL // CONV_TM
    is_start = jnp.logical_or(i % tiles_per_seq == 0, is_ctx)
    is_end = jnp.logical_or(i % tiles_per_seq == tiles_per_seq - 1, is_ctx)
    z = c_ref[...] * x_ref[...]
    zp = cp_ref[...] * xp_ref[...]
    zn = cn_ref[...] * xn_ref[...]
    prev_row = jnp.where(is_start, 0.0, zp[7:8, :])
    next_row = jnp.where(is_end, 0.0, zn[0:1, :])
    rows = lax.broadcasted_iota(jnp.int32, z.shape, 0)
    z_prev = jnp.where(rows == 0, prev_row, pltpu.roll(z, 1, 0))
    z_next = jnp.where(rows == CONV_TM - 1, next_row, pltpu.roll(z, CONV_TM - 1, 0))
    w = w_ref[...]
    y = w[0:1, :] * z_prev + w[1:2, :] * z + w[2:3, :] * z_next
    o_ref[...] = (b_ref[...] * y).astype(BF)


def _short_conv(u, conv_w, layer):
    tm = CONV_TM
    n8 = tm // 8
    last8 = NT // 8 - 1
    return pl.pallas_call(
        _conv_kernel,
        out_shape=jax.ShapeDtypeStruct((NT, CONV_DIM), BF),
        grid=(NT // tm,),
        in_specs=[pl.BlockSpec((tm, CONV_DIM), lambda i: (i, U_CONV_B // CONV_DIM)),
                  pl.BlockSpec((tm, CONV_DIM), lambda i: (i, U_CONV_C // CONV_DIM)),
                  pl.BlockSpec((tm, CONV_DIM), lambda i: (i, U_CONV_X // CONV_DIM)),
                  pl.BlockSpec((8, CONV_DIM), lambda i: (jnp.maximum(i * n8 - 1, 0), U_CONV_C // CONV_DIM)),
                  pl.BlockSpec((8, CONV_DIM), lambda i: (jnp.maximum(i * n8 - 1, 0), U_CONV_X // CONV_DIM)),
                  pl.BlockSpec((8, CONV_DIM), lambda i: (jnp.minimum(i * n8 + n8, last8), U_CONV_C // CONV_DIM)),
                  pl.BlockSpec((8, CONV_DIM), lambda i: (jnp.minimum(i * n8 + n8, last8), U_CONV_X // CONV_DIM)),
                  pl.BlockSpec((None, CONV_K, CONV_DIM), lambda i: (layer, 0, 0))],
        out_specs=pl.BlockSpec((tm, CONV_DIM), lambda i: (i, 0)),
        compiler_params=_params("parallel"),
        name="short_conv",
    )(u, u, u, u, u, u, u, conv_w)


GLA_STEPS = (CTX_LEN + SEQ) // GLA_CHUNK
GLA_CTX_STEPS = CTX_LEN // GLA_CHUNK


def _log_sigmoid(z):
    return -(jnp.maximum(-z, 0.0) + jnp.log1p(jnp.exp(-jnp.abs(z))))


def _rope(x, cs, sn):
    lane = lax.broadcasted_iota(jnp.int32, x.shape, 1)
    swapped = jnp.where(lane % 2 == 0, pltpu.roll(x, GLA_DK - 1, 1), pltpu.roll(x, 1, 1))
    return x * cs + swapped * sn


def _gla_decayed_operands(q_ref, k_ref, l_ref, r_ref, w_ref, b_ref, reverse):
    c = GLA_CHUNK
    z = _dot(l_ref[...].astype(BF), w_ref[...]) + b_ref[...]
    g = _log_sigmoid(z) * (1.0 / GLA_TAU)
    t_i = lax.broadcasted_iota(jnp.int32, (c, c), 0)
    s_i = lax.broadcasted_iota(jnp.int32, (c, c), 1)
    keep = (s_i >= t_i) if reverse else (s_i <= t_i)
    tri = jnp.where(keep, 1.0, 0.0).astype(BF)
    g1 = g.astype(BF)
    r1 = g - g1.astype(F32)
    g2 = r1.astype(BF)
    g3 = (r1 - g2.astype(F32)).astype(BF)
    b = _dot(tri, g1) + _dot(tri, g2) + _dot(tri, g3)
    total = b[0:1, :] if reverse else b[c - 1:c, :]
    cs = r_ref[:, :GLA_DK]
    sn = r_ref[:, GLA_DK:]
    heads = [slice(h * GLA_DK, (h + 1) * GLA_DK) for h in range(GLA_HEADS)]
    qs = jnp.concatenate([_rope(q_ref[:, hd] * (GLA_DK ** -0.5), cs, sn) for hd in heads], axis=1)
    kr = jnp.concatenate([_rope(k_ref[:, hd], cs, sn) for hd in heads], axis=1)
    q_dec = (qs * jnp.exp(b)).astype(BF)
    k_inv = (kr * jnp.exp(-b)).astype(BF)
    k_end = (kr * jnp.exp(total - b)).astype(BF)
    return keep, q_dec, k_inv, k_end, jnp.exp(total)


def _gla_kernel(qf_ref, kf_ref, vf_ref, lf_ref, rf_ref, qb_ref, kb_ref, vb_ref, lb_ref, rb_ref,
                wf_ref, bf_ref, wb_ref, bb_ref, of_ref, ob_ref, stf_ref, stb_ref):
    @pl.when(pl.program_id(1) == 0)
    def _():
        stf_ref[...] = jnp.zeros_like(stf_ref)
        stb_ref[...] = jnp.zeros_like(stb_ref)

    pre = [_gla_decayed_operands(qf_ref, kf_ref, lf_ref, rf_ref, wf_ref, bf_ref, False),
           _gla_decayed_operands(qb_ref, kb_ref, lb_ref, rb_ref, wb_ref, bb_ref, True)]
    chains = [(d, h) for d in range(2) for h in range(GLA_HEADS)]
    v_refs, o_refs, st_refs = (vf_ref, vb_ref), (of_ref, ob_ref), (stf_ref, stb_ref)

    def head(x, h):
        return x[:, h * GLA_DK:(h + 1) * GLA_DK]

    v = [v_refs[d][:, h * GLA_DV:(h + 1) * GLA_DV].astype(BF) for d, h in chains]
    st = [st_refs[d][h] for d, h in chains]
    att = [jnp.where(pre[d][0], _dot_nt(head(pre[d][1], h), head(pre[d][2], h)), 0.0).astype(BF)
           for d, h in chains]
    o = [_dot(att[i], v[i]) + _dot_nt(head(pre[d][1], h), st[i].astype(BF)) for i, (d, h) in enumerate(chains)]
    for d in range(2):
        o_refs[d][...] = jnp.concatenate(o[d * GLA_HEADS:(d + 1) * GLA_HEADS], axis=1)
    for i, (d, h) in enumerate(chains):
        st_refs[d][h] = st[i] * head(pre[d][4], h) + _dot_tn(v[i], head(pre[d][3], h))


def _gla(u, lr, rope_tab, wpad_f, bg_f, wpad_b, bg_b):
    c = GLA_CHUNK
    lat_chunks = SEQ // c
    ctx_base = NL // c

    def row_f(b, s):
        return jnp.where(s < GLA_CTX_STEPS, ctx_base + b * GLA_CTX_STEPS + s, b * lat_chunks + s - GLA_CTX_STEPS)

    def row_b(b, s):
        return jnp.where(s < GLA_CTX_STEPS, ctx_base + b * GLA_CTX_STEPS + GLA_CTX_STEPS - 1 - s,
                         b * lat_chunks + GLA_STEPS - 1 - s)

    def rope_f(s):
        return jnp.where(s < GLA_CTX_STEPS, lat_chunks, s - GLA_CTX_STEPS)

    def rope_b(s):
        return jnp.where(s < GLA_CTX_STEPS, lat_chunks, GLA_STEPS - 1 - s)

    def stream(row, rope):
        return [pl.BlockSpec((c, GLA_QK), lambda b, s: (row(b, s), U_GLA_Q // GLA_QK)),
                pl.BlockSpec((c, GLA_QK), lambda b, s: (row(b, s), U_GLA_K // GLA_QK)),
                pl.BlockSpec((c, GLA_V), lambda b, s: (row(b, s), U_GLA_V // GLA_V)),
                pl.BlockSpec((c, LR_COLS), lambda b, s: (row(b, s), 0)),
                pl.BlockSpec((c, 2 * GLA_DK), lambda b, s: (rope(s), 0))]

    head_w = pl.BlockSpec((LR_COLS, GLA_QK), lambda b, s: (0, 0))
    head_b = pl.BlockSpec((1, GLA_QK), lambda b, s: (0, 0))
    state = pltpu.VMEM((GLA_HEADS, GLA_DV, GLA_DK), F32)
    return pl.pallas_call(
        _gla_kernel,
        out_shape=(jax.ShapeDtypeStruct((NT, GLA_V), F32), jax.ShapeDtypeStruct((NT, GLA_V), F32)),
        grid=(BATCH, GLA_STEPS),
        in_specs=stream(row_f, rope_f) + stream(row_b, rope_b) + [head_w, head_b, head_w, head_b],
        out_specs=(pl.BlockSpec((c, GLA_V), lambda b, s: (row_f(b, s), 0)),
                   pl.BlockSpec((c, GLA_V), lambda b, s: (row_b(b, s), 0))),
        scratch_shapes=[state, state],
        compiler_params=_params("parallel", "arbitrary"),
        name="gla",
    )(u, u, u, lr, rope_tab, u, u, u, lr, rope_tab, wpad_f, bg_f, wpad_b, bg_b)


def _rope_table():
    t = jnp.arange(SEQ)
    rows = (t // GRID_W).astype(F32)
    cols = (t % GRID_W).astype(F32)
    n_freq = GLA_DK // 4
    inv = 1.0 / (ROPE_BASE ** (jnp.arange(n_freq, dtype=F32) / n_freq))
    ang = jnp.concatenate([rows[:, None] * inv, cols[:, None] * inv], -1)
    cos = jnp.repeat(jnp.cos(ang), 2, axis=-1)
    sin = jnp.repeat(jnp.sin(ang), 2, axis=-1)
    sign = jnp.tile(jnp.array([-1.0, 1.0], F32), GLA_DK // 2)
    cos = jnp.concatenate([cos, jnp.ones((GRID_W, GLA_DK), F32)], 0)
    sin = jnp.concatenate([sin * sign, jnp.zeros((GRID_W, GLA_DK), F32)], 0)
    return jnp.concatenate([cos, sin], axis=1)


def _decay_weights(wg, bg, lane_offset):
    w = jnp.pad(wg, ((lane_offset, LR_COLS - GLA_RANK - lane_offset), (0, 0))).astype(BF)
    return w, bg.reshape(1, GLA_QK)


NA_HPS = 2


def _na_softmax_out(parts):
    m = functools.reduce(jnp.maximum, [jnp.max(s, axis=-1, keepdims=True) for s, _ in parts])
    ps = [jnp.exp(s - m) for s, _ in parts]
    l = functools.reduce(jnp.add, [jnp.sum(p, axis=-1, keepdims=True) for p in ps])
    o = functools.reduce(jnp.add, [_dot(p.astype(BF), v) for p, (_, v) in zip(ps, parts)])
    return (o / l).astype(BF)


def _na_kernel(q_ref, k_ref, v_ref, kc_ref, vc_ref, bias_ref, o_ref):
    j = pl.program_id(2)
    head_cols = [slice(h * NA_DH, (h + 1) * NA_DH) for h in range(NA_HPS)]

    @pl.when(j < NA_NBLK)
    def _():
        base = jnp.clip(NA_QROWS * j - WIN_R // 2, 0, GRID_ROWS - NA_KROWS)
        start = pl.multiple_of(base * GRID_W, GRID_W)
        for h, hc in enumerate(head_cols):
            q = q_ref[:, hc]
            vl = v_ref[pl.ds(start, NA_KBLK), hc]
            s_loc = _dot_nt(q, k_ref[pl.ds(start, NA_KBLK), hc]) + bias_ref[h]
            s_ctx = _dot_nt(q, kc_ref[:, hc])
            o_ref[:, hc] = _na_softmax_out([(s_loc, vl), (s_ctx, vc_ref[:, hc])])

    @pl.when(j == NA_NBLK)
    def _():
        for hc in head_cols:
            s_ctx = _dot_nt(q_ref[:, hc], kc_ref[:, hc])
            o_ref[:, hc] = _na_softmax_out([(s_ctx, vc_ref[:, hc])])


def _na_bias_geometry():
    qc = np.arange(GRID_W)
    cs = np.clip(qc - WIN_C // 2, 0, GRID_W - WIN_C)
    kc = np.arange(GRID_W)
    col_ok = (kc[None, :] >= cs[:, None]) & (kc[None, :] < cs[:, None] + WIN_C)
    dc = kc[None, :] - qc[:, None] + WIN_C - 1
    col_sel = ((dc[None] == np.arange(2 * WIN_C - 1)[:, None, None]) & col_ok[None]).astype(np.float32)
    row_sel = np.zeros((3, NA_QROWS, NA_KROWS, 2 * WIN_R - 1), np.float32)
    for c, j in enumerate((0, 1, NA_NBLK - 1)):
        base = int(np.clip(NA_QROWS * j - WIN_R // 2, 0, GRID_ROWS - NA_KROWS))
        for qr in range(NA_QROWS):
            r = NA_QROWS * j + qr
            rs = int(np.clip(r - WIN_R // 2, 0, GRID_ROWS - WIN_R))
            for kr in range(NA_KROWS):
                a = base + kr
                if rs <= a < rs + WIN_R:
                    row_sel[c, qr, kr, a - r + WIN_R - 1] = 1.0
    row_ok = row_sel.sum(-1) > 0
    mask = row_ok[:, :, None, :, None] & col_ok[None, None, :, None, :]
    return row_sel, col_sel, mask.reshape(3, NA_QBLK, NA_KBLK)


def _na_bias_tables(rpb):
    row_sel, col_sel, mask = _na_bias_geometry()
    hi = lax.Precision.HIGHEST
    t = jnp.einsum('hrd,dxy->hrxy', rpb, jnp.asarray(col_sel), precision=hi)
    b = jnp.einsum('cqkr,hrxy->hcqxky', jnp.asarray(row_sel), t, precision=hi)
    b = b.reshape(NA_HEADS, 3, NA_QBLK, NA_KBLK)
    return jnp.where(jnp.asarray(mask)[None], b, NEG_INF)


def _neighbourhood_attention(un, bias, with_ctx):
    steps = NA_NBLK + (1 if with_ctx else 0)
    ctx_q = NL // NA_QBLK
    w = NA_HPS * NA_DH

    def q_row(b, j):
        return jnp.where(j < NA_NBLK, b * NA_NBLK + j, ctx_q + b)

    def bias_class(j):
        return jnp.where(j == 0, 0, jnp.where(j >= NA_NBLK - 1, 2, 1))

    ctx_blk = NL // CTX_LEN
    return pl.pallas_call(
        _na_kernel,
        out_shape=jax.ShapeDtypeStruct((NT, NA_W), BF),
        grid=(BATCH, NA_HEADS // NA_HPS, steps),
        in_specs=[pl.BlockSpec((NA_QBLK, w), lambda b, h, j: (q_row(b, j), UN_Q // w + h)),
                  pl.BlockSpec((SEQ, w), lambda b, h, j: (b, UN_K // w + h)),
                  pl.BlockSpec((SEQ, w), lambda b, h, j: (b, UN_V // w + h)),
                  pl.BlockSpec((CTX_LEN, w), lambda b, h, j: (ctx_blk + b, UN_K // w + h)),
                  pl.BlockSpec((CTX_LEN, w), lambda b, h, j: (ctx_blk + b, UN_V // w + h)),
                  pl.BlockSpec((NA_HPS, None, NA_QBLK, NA_KBLK), lambda b, h, j: (h, bias_class(j), 0, 0))],
        out_specs=pl.BlockSpec((NA_QBLK, w), lambda b, h, j: (q_row(b, j), h)),
        compiler_params=_params("parallel", "parallel", "arbitrary"),
        name="neighbourhood_attention",
    )(un, un, un, un, un, bias)


MERGE_TM = 256


def _merge_kernel(yc_ref, of_ref, ob_ref, r_ref, yn_ref, gt_ref, xl_ref, xc_ref,
                  ng_ref, gb_ref, wc_ref, wg_ref, wn_ref, wo_ref,
                  g1_ref, lg_ref, lb_ref, sh2_ref, sc2_ref, xo_ref, ho_ref):
    x = _select_rows(pl.program_id(0), NL // MERGE_TM, xl_ref, xc_ref)
    o = of_ref[...] + ob_ref[...]
    r = r_ref[...]
    ng = ng_ref[...]
    heads = []
    for h in range(GLA_HEADS):
        oh = o[:, h * GLA_DV:(h + 1) * GLA_DV]
        oh = oh * lax.rsqrt(jnp.mean(oh * oh, axis=-1, keepdims=True) + LN_EPS)
        heads.append(oh * ng * _silu(r[:, h * GLA_DV:(h + 1) * GLA_DV]))
    y_gla = jnp.concatenate(heads, axis=-1).astype(BF)
    g = jax.nn.sigmoid(gt_ref[...] + gb_ref[...])
    z = (g[:, :D_MODEL] * _dot(yc_ref[...], wc_ref[...])
         + g[:, D_MODEL:2 * D_MODEL] * _dot(y_gla, wg_ref[...])
         + g[:, 2 * D_MODEL:] * _dot(yn_ref[...], wn_ref[...]))
    y = _dot(z.astype(BF), wo_ref[...])
    xn = _layer_norm(DEEPNORM_ALPHA * x + g1_ref[...] * y, lg_ref[...], lb_ref[...])
    xo_ref[...] = xn
    ho_ref[...] = (xn * (1.0 + sc2_ref[...]) + sh2_ref[...]).astype(BF)


def _merge(y_conv, o_f, o_b, u, ug, y_na, x_lat, x_ctx, mod, layer, rows, norm_g, gate_b, w_conv, w_gla, w_na,
           w_out, ln_g, ln_b):
    tm = MERGE_TM
    rot = _mod_row_of_tile(tm)
    row = lambda width: pl.BlockSpec((tm, width), lambda i: (i, 0))
    const = lambda shape: pl.BlockSpec(shape, lambda i: (0,) * len(shape), pipeline_mode=pl.Buffered(1))
    return pl.pallas_call(
        _merge_kernel,
        out_shape=(jax.ShapeDtypeStruct((rows, D_MODEL), F32), jax.ShapeDtypeStruct((rows, D_MODEL), BF)),
        grid=(rows // tm,),
        in_specs=[row(CONV_DIM), row(GLA_V), row(GLA_V),
                  pl.BlockSpec((tm, GLA_V), lambda i: (i, U_GLA_R // GLA_V)),
                  row(NA_W), row(N_BRANCH * D_MODEL), *_split_row_specs(tm),
                  const((1, GLA_DV)), const((1, N_BRANCH * D_MODEL)),
                  const((CONV_DIM, D_MODEL)), const((GLA_V, D_MODEL)), const((NA_W, D_MODEL)),
                  const((D_MODEL, D_MODEL)),
                  _mod_spec(layer, 2, rot), const((1, D_MODEL)), const((1, D_MODEL)),
                  _mod_spec(layer, 3, rot), _mod_spec(layer, 4, rot)],
        out_specs=(row(D_MODEL), row(D_MODEL)),
        compiler_params=_params("parallel"),
        name="merge",
    )(y_conv, o_f, o_b, u, y_na, ug, x_lat, x_ctx, norm_g, gate_b, w_conv, w_gla, w_na, w_out,
      mod, ln_g, ln_b, mod, mod)


FFN_TM = 512
FFN_TF = 512


def _ffn_kernel(h_ref, wg_ref, wu_ref, wd_ref, x_ref, g2_ref, lg_ref, lb_ref, shn_ref, scn_ref,
                xo_ref, ho_ref, acc_ref):
    j = pl.program_id(1)

    @pl.when(j == 0)
    def _():
        acc_ref[...] = jnp.zeros_like(acc_ref)

    h = h_ref[...]
    a = _silu(_dot(h, wg_ref[...])) * _dot(h, wu_ref[...])
    acc_ref[...] += _dot(a.astype(BF), wd_ref[...])

    @pl.when(j == pl.num_programs(1) - 1)
    def _():
        xn = _layer_norm(DEEPNORM_ALPHA * x_ref[...] + g2_ref[...] * acc_ref[...], lg_ref[...], lb_ref[...])
        xo_ref[...] = xn
        ho_ref[...] = (xn * (1.0 + scn_ref[...]) + shn_ref[...]).astype(BF)


def _dense_ffn(h2, x_all, mod, layer, wg, wu, wd, ln_g, ln_b):
    tm, tf = FFN_TM, FFN_TF
    rot = _mod_row_of_tile(tm)
    row = lambda: pl.BlockSpec((tm, D_MODEL), lambda i, j: (i, 0))
    const = lambda: pl.BlockSpec((1, D_MODEL), lambda i, j: (0, 0))
    return pl.pallas_call(
        _ffn_kernel,
        out_shape=(jax.ShapeDtypeStruct((NT, D_MODEL), F32), jax.ShapeDtypeStruct((NT, D_MODEL), BF)),
        grid=(NT // tm, D_FF // tf),
        in_specs=[row(),
                  pl.BlockSpec((D_MODEL, tf), lambda i, j: (0, j)),
                  pl.BlockSpec((D_MODEL, tf), lambda i, j: (0, j)),
                  pl.BlockSpec((tf, D_MODEL), lambda i, j: (j, 0)),
                  row(), _mod_spec(layer, 5, rot), const(), const(),
                  _mod_spec(layer + 1, 0, rot), _mod_spec(layer + 1, 1, rot)],
        out_specs=(row(), row()),
        scratch_shapes=[pltpu.VMEM((tm, D_MODEL), F32)],
        compiler_params=_params("parallel", "arbitrary"),
        name="dense_ffn",
    )(h2, wg, wu, wd, x_all, mod, ln_g, ln_b, mod, mod)


def _router_kernel(h_ref, w_ref, idx_ref, wt_ref):
    logits = _dot(h_ref[...], w_ref[...])
    lane = lax.broadcasted_iota(jnp.int32, logits.shape, 1)
    logits = jnp.where(lane < N_EXPERTS, logits, -jnp.inf)
    m1 = jnp.max(logits, axis=-1, keepdims=True)
    i1 = jnp.min(jnp.where(logits == m1, lane, ROUTER_LANES), axis=-1, keepdims=True)
    rest = jnp.where(lane == i1, -jnp.inf, logits)
    m2 = jnp.max(rest, axis=-1, keepdims=True)
    i2 = jnp.min(jnp.where(rest == m2, lane, ROUTER_LANES), axis=-1, keepdims=True)
    e2 = jnp.exp(m2 - m1)
    w1 = 1.0 / (1.0 + e2)
    w2 = e2 / (1.0 + e2)
    idx_ref[...] = jnp.where(lane == 0, i1, jnp.where(lane == 1, i2, 0))
    wt_ref[...] = jnp.where(lane == 0, w1, jnp.where(lane == 1, w2, 0.0))


def _router(h2, router_w):
    tm = 512
    return pl.pallas_call(
        _router_kernel,
        out_shape=(jax.ShapeDtypeStruct((NL, ROUTER_LANES), jnp.int32),
                   jax.ShapeDtypeStruct((NL, ROUTER_LANES), F32)),
        grid=(NL // tm,),
        in_specs=[pl.BlockSpec((tm, D_MODEL), lambda i: (i, 0)),
                  pl.BlockSpec((D_MODEL, ROUTER_LANES), lambda i: (0, 0))],
        out_specs=(pl.BlockSpec((tm, ROUTER_LANES), lambda i: (i, 0)),
                   pl.BlockSpec((tm, ROUTER_LANES), lambda i: (i, 0))),
        compiler_params=_params("parallel"),
        name="moe_router",
    )(h2, router_w)


def _dispatch_plan(idx):
    i32 = jnp.int32
    experts = jnp.arange(N_EXPERTS, dtype=i32)
    oh_a = (idx[:, 0:1] == experts).astype(i32)
    oh_b = (idx[:, 1:2] == experts).astype(i32)
    cum = jnp.cumsum(oh_a + oh_b, axis=0)
    counts = cum[-1]
    region = ((counts + MOE_TM - 1) // MOE_TM) * MOE_TM
    g_end = jnp.cumsum(region)
    g_start = g_end - region
    slot_table = g_start[None, :] + cum - 1
    slot_a = jnp.sum(oh_a * slot_table, axis=1)
    slot_b = jnp.sum(oh_b * slot_table, axis=1)

    tile_start = jnp.arange(MOE_NBLK, dtype=i32) * MOE_TM
    tile_expert = jnp.minimum(jnp.sum((tile_start[:, None] >= g_end[None, :]).astype(i32), axis=1),
                              N_EXPERTS - 1)
    n_tiles = jnp.maximum(g_end[-1] // MOE_TM, 1).reshape(1)
    tile_expert = (tile_expert, n_tiles)

    pair_ids = jnp.arange(MOE_NPAIR, dtype=i32)
    tile_last = jnp.arange(MOE_NTT, dtype=i32) * MOE_GB + MOE_GB - 1
    hi = cum[tile_last]
    lo = jnp.concatenate([jnp.zeros((1, N_EXPERTS), i32), hi[:-1]], axis=0)

    blk_start = jnp.arange(MOE_NGB, dtype=i32) * MOE_GB
    blk_e = jnp.minimum(jnp.sum((blk_start[:, None] >= g_end[None, :]).astype(i32), axis=1), N_EXPERTS - 1)
    r0 = blk_start - g_start[blk_e]
    r1 = jnp.minimum(r0 + MOE_GB, counts[blk_e])
    has = r1 > r0
    hi_e = hi[:, blk_e]
    t_lo = jnp.sum((hi_e <= r0[None, :]).astype(i32), axis=0)
    t_hi = jnp.sum((hi_e < r1[None, :]).astype(i32), axis=0)
    t_lo = jnp.where(has, t_lo, 0)
    t_end = jnp.where(has, t_hi + 1, 0)
    dispatch = (t_lo, t_end, (n_tiles * (MOE_TM // MOE_GB)).astype(i32))

    b_lo = (g_start[None, :] + lo) // MOE_GB
    b_hi = (g_start[None, :] + hi - 1) // MOE_GB
    n_c = jnp.where(hi > lo, b_hi - b_lo + 1, 0).reshape(-1)
    b_lo = b_lo.reshape(-1)
    c_end = jnp.cumsum(n_c)
    c_total = c_end[-1]
    c_idx = jnp.minimum(jnp.sum((pair_ids[:, None] >= c_end[None, :]).astype(i32), axis=1),
                        MOE_NTT * N_EXPERTS - 1)
    c_blk = jnp.clip(b_lo[c_idx] + pair_ids - (c_end[c_idx] - n_c[c_idx]), 0, MOE_NGB - 1)
    c_tile = c_idx // N_EXPERTS
    c_valid = pair_ids < c_total
    c_tile = jnp.where(c_valid, c_tile, MOE_NTT - 1)
    c_blk = jnp.where(c_valid, c_blk, c_blk[jnp.maximum(c_total - 1, 0)])
    c_prev = jnp.concatenate([jnp.full((1,), -1, i32), c_tile[:-1]])
    c_next = jnp.concatenate([c_tile[1:], jnp.full((1,), -1, i32)])
    c_first = c_valid & (c_tile != c_prev)
    c_last = c_valid & ((c_tile != c_next) | (pair_ids == c_total - 1))
    combine = (c_tile, c_blk, c_first.astype(i32), c_last.astype(i32), c_valid.astype(i32))
    return slot_a, slot_b, tile_expert, dispatch, combine


def _dispatch_kernel(lo_ref, end_ref, n_ref, h_ref, sa_ref, sb_ref, wa_ref, wb_ref, hs_ref, ws_ref,
                     acc_ref, wacc_ref):
    blk = pl.program_id(0)

    @pl.when(blk < n_ref[0])
    def _():
        acc_ref[...] = jnp.zeros_like(acc_ref)
        wacc_ref[...] = jnp.zeros_like(wacc_ref)
        slots = blk * MOE_GB + lax.broadcasted_iota(jnp.int32, (MOE_GB, MOE_GB), 0)

        def scan_tile(t, carry):
            rows = pl.ds(pl.multiple_of(t * MOE_GB, MOE_GB), MOE_GB)
            hit_a = slots == sa_ref[t]
            hit_b = slots == sb_ref[t]
            onehot = jnp.where(hit_a, 1.0, jnp.where(hit_b, 1.0, 0.0)).astype(BF)
            acc_ref[...] += _dot(onehot, h_ref[rows, :])
            w = jnp.where(hit_a, wa_ref[t], jnp.where(hit_b, wb_ref[t], 0.0))
            wacc_ref[...] += jnp.broadcast_to(jnp.sum(w, axis=1, keepdims=True), wacc_ref.shape)
            return carry

        lax.fori_loop(lo_ref[blk], end_ref[blk], scan_tile, 0)
        hs_ref[...] = acc_ref[...].astype(BF)
        ws_ref[...] = wacc_ref[...]


def _dispatch(h2, slot_a, slot_b, w_a, w_b, plan):
    used = lambda i, lo, end, n: (jnp.minimum(i, n[0] - 1), 0)
    resident = lambda shape: pl.BlockSpec(shape, lambda i, *_: (0,) * len(shape), pipeline_mode=pl.Buffered(1))
    shape3 = (MOE_NTT, 1, MOE_GB)
    grid_spec = pltpu.PrefetchScalarGridSpec(
        num_scalar_prefetch=3,
        grid=(MOE_NGB,),
        in_specs=[resident((NL, D_MODEL)), resident(shape3), resident(shape3), resident(shape3),
                  resident(shape3)],
        out_specs=(pl.BlockSpec((MOE_GB, D_MODEL), used), pl.BlockSpec((MOE_GB, ROUTER_LANES), used)),
        scratch_shapes=[pltpu.VMEM((MOE_GB, D_MODEL), F32), pltpu.VMEM((MOE_GB, ROUTER_LANES), F32)])
    return pl.pallas_call(
        _dispatch_kernel,
        out_shape=(jax.ShapeDtypeStruct((MOE_NSLOT, D_MODEL), BF),
                   jax.ShapeDtypeStruct((MOE_NSLOT, ROUTER_LANES), F32)),
        grid_spec=grid_spec,
        compiler_params=_params("arbitrary"),
        name="moe_dispatch",
    )(*plan, h2, slot_a.reshape(shape3), slot_b.reshape(shape3), w_a.reshape(shape3), w_b.reshape(shape3))


def _tile_flags(e_ref, n_ref):
    i = pl.program_id(1)
    valid = i < n_ref[0]
    changed = jnp.logical_or(i == 0, e_ref[i] != e_ref[jnp.maximum(i - 1, 0)])
    return valid, jnp.logical_and(valid, changed)


def _used_tile(i, n_ref):
    return jnp.minimum(i, n_ref[0] - 1)


def _expert_up_kernel(e_ref, n_ref, hs_ref, wg_ref, wu_ref, a_ref, wgs_ref, wus_ref):
    valid, changed = _tile_flags(e_ref, n_ref)

    @pl.when(changed)
    def _():
        wgs_ref[...] = wg_ref[...].astype(BF)
        wus_ref[...] = wu_ref[...].astype(BF)

    @pl.when(valid)
    def _():
        h = hs_ref[...]
        a_ref[...] = (_silu(_dot(h, wgs_ref[...])) * _dot(h, wus_ref[...])).astype(BF)


def _expert_up(hs, tile_plan, wg, wu, moe_layer):
    tf = 512
    w_spec = pl.BlockSpec((None, None, D_MODEL, tf), lambda j, i, e, n: (moe_layer, e[_used_tile(i, n)], 0, j))
    grid_spec = pltpu.PrefetchScalarGridSpec(
        num_scalar_prefetch=2,
        grid=(D_FF // tf, MOE_NBLK),
        in_specs=[pl.BlockSpec((MOE_TM, D_MODEL), lambda j, i, e, n: (_used_tile(i, n), 0)), w_spec, w_spec],
        out_specs=pl.BlockSpec((MOE_TM, tf), lambda j, i, e, n: (_used_tile(i, n), j)),
        scratch_shapes=[pltpu.VMEM((D_MODEL, tf), BF), pltpu.VMEM((D_MODEL, tf), BF)])
    return pl.pallas_call(
        _expert_up_kernel,
        out_shape=jax.ShapeDtypeStruct((MOE_NSLOT, D_FF), BF),
        grid_spec=grid_spec,
        compiler_params=_params("arbitrary", "arbitrary"),
        name="moe_expert_up",
    )(*tile_plan, hs, wg, wu)


def _expert_down_kernel(e_ref, n_ref, a_ref, wd_ref, ws_ref, y_ref, wds_ref):
    valid, changed = _tile_flags(e_ref, n_ref)

    @pl.when(changed)
    def _():
        wds_ref[...] = wd_ref[...].astype(BF)

    @pl.when(valid)
    def _():
        y_ref[...] = (_dot(a_ref[...], wds_ref[...]) * ws_ref[:, 0:1]).astype(BF)


def _expert_down(a, tile_plan, wd, ws, moe_layer):
    tn = 512
    out_spec = pl.BlockSpec((MOE_TM, tn), lambda j, i, e, n: (_used_tile(i, n), j))
    grid_spec = pltpu.PrefetchScalarGridSpec(
        num_scalar_prefetch=2,
        grid=(D_MODEL // tn, MOE_NBLK),
        in_specs=[pl.BlockSpec((MOE_TM, D_FF), lambda j, i, e, n: (_used_tile(i, n), 0)),
                  pl.BlockSpec((None, None, D_FF, tn),
                               lambda j, i, e, n: (moe_layer, e[_used_tile(i, n)], 0, j)),
                  pl.BlockSpec((MOE_TM, ROUTER_LANES), lambda j, i, e, n: (_used_tile(i, n), 0))],
        out_specs=out_spec,
        scratch_shapes=[pltpu.VMEM((D_FF, tn), BF)])
    return pl.pallas_call(
        _expert_down_kernel,
        out_shape=jax.ShapeDtypeStruct((MOE_NSLOT, D_MODEL), BF),
        grid_spec=grid_spec,
        compiler_params=_params("arbitrary", "arbitrary"),
        name="moe_expert_down",
    )(*tile_plan, a, wd, ws)


def _combine_kernel(tile_ref, blk_ref, first_ref, last_ref, valid_ref,
                    y_ref, sa_ref, sb_ref, x_ref, g2_ref, lg_ref, lb_ref, o_ref, acc_ref):
    p = pl.program_id(0)

    @pl.when(first_ref[p] == 1)
    def _():
        acc_ref[...] = jnp.zeros_like(acc_ref)

    @pl.when(valid_ref[p] == 1)
    def _():
        slots = blk_ref[p] * MOE_GB + lax.broadcasted_iota(jnp.int32, (MOE_GB, MOE_GB), 1)
        onehot = jnp.where(slots == sa_ref[...], 1.0, jnp.where(slots == sb_ref[...], 1.0, 0.0)).astype(BF)
        acc_ref[...] += _dot(onehot, y_ref[...])

    @pl.when(last_ref[p] == 1)
    def _():
        o_ref[...] = _layer_norm(DEEPNORM_ALPHA * x_ref[...] + g2_ref[...] * acc_ref[...],
                                 lg_ref[...], lb_ref[...])


def _combine(y, slot_a, slot_b, x_lat, mod, layer, ln_g, ln_b, plan):
    per_batch = SEQ // MOE_GB
    slot_blk = lambda p, tile, blk, *_: (blk[p], 0)
    tok_blk = lambda p, tile, *_: (tile[p], 0)
    grid_spec = pltpu.PrefetchScalarGridSpec(
        num_scalar_prefetch=5,
        grid=(MOE_NPAIR,),
        in_specs=[pl.BlockSpec((MOE_GB, D_MODEL), slot_blk),
                  pl.BlockSpec((MOE_GB, 1), tok_blk), pl.BlockSpec((MOE_GB, 1), tok_blk),
                  pl.BlockSpec((MOE_GB, D_MODEL), tok_blk),
                  pl.BlockSpec((None, 1, D_MODEL),
                               lambda p, tile, *_: (layer * 48 + (tile[p] // per_batch) * 6 + 5, 0, 0)),
                  pl.BlockSpec((1, D_MODEL), lambda p, *_: (0, 0)),
                  pl.BlockSpec((1, D_MODEL), lambda p, *_: (0, 0))],
        out_specs=pl.BlockSpec((MOE_GB, D_MODEL), tok_blk),
        scratch_shapes=[pltpu.VMEM((MOE_GB, D_MODEL), F32)])
    return pl.pallas_call(
        _combine_kernel,
        out_shape=jax.ShapeDtypeStruct((NL, D_MODEL), F32),
        grid_spec=grid_spec,
        compiler_params=_params("arbitrary"),
        name="moe_combine",
    )(*plan, y, slot_a.reshape(NL, 1), slot_b.reshape(NL, 1), x_lat, mod, ln_g, ln_b)


def _moe_ffn(h2, x_lat, mod, layer, router_w, wg, wu, wd, moe_layer, ln_g, ln_b):
    router_pad = jnp.pad(router_w, ((0, 0), (0, ROUTER_LANES - N_EXPERTS))).astype(BF)
    idx, wts = _router(h2, router_pad)
    slot_a, slot_b, tile_plan, d_plan, c_plan = _dispatch_plan(idx[:, :TOP_K])
    hs, ws = _dispatch(h2, slot_a, slot_b, wts[:, 0], wts[:, 1], d_plan)
    a = _expert_up(hs, tile_plan, wg, wu, moe_layer)
    y = _expert_down(a, tile_plan, wd, ws, moe_layer)
    return _combine(y, slot_a, slot_b, x_lat, mod, layer, ln_g, ln_b, c_plan)


def _unaligned_w_in(w):
    gla_end = U_COLS
    lr_end = gla_end + 2 * GLA_RANK
    na_end = lr_end + 3 * NA_W
    gates = w[:, na_end:].astype(BF)
    na = w[:, lr_end:na_end].astype(BF)
    lr = jnp.pad(w[:, gla_end:lr_end], ((0, 0), (0, LR_COLS - 2 * GLA_RANK))).astype(BF)
    return gates, na, lr


def kernel(x, c, ctx, c_ctx, ada_w, ada_b, w_in, conv_w, gla_wg_f, gla_bg_f, gla_wg_b, gla_bg_b, gla_norm_g, na_rpb, w_br_conv, w_br_gla, w_br_na, gate_b, w_out, ln1_g, ln1_b, ln2_g, ln2_b, ffn_w_gate, ffn_w_up, ffn_w_down, moe_router, moe_w_gate, moe_w_up, moe_w_down):
    assert DEPTH == 2 and x.shape == (BATCH, SEQ, D_MODEL) and ctx.shape == (BATCH, CTX_LEN, D_MODEL)
    cvec = jnp.concatenate([c, c_ctx[None, :], jnp.zeros((8 - BATCH - 1, D_MODEL), F32)], axis=0)
    mod = _ada_table(cvec, ada_w, ada_b)
    rope_tab = _rope_table()
    x_lat, x_ctx = x.reshape(NL, D_MODEL), ctx.reshape(NC, D_MODEL)
    h1 = _modulate(x_lat, x_ctx, mod, 0)

    for layer in range(DEPTH):
        last = layer == DEPTH - 1
        w_gates, w_na, w_lr = _unaligned_w_in(w_in[layer])
        u = _matmul_layer_weight(h1, w_in, layer, U_COLS, 1088, 1024, "proj_in")
        ug = _matmul(h1, w_gates, 1088, 1024, "proj_gates")
        un = _matmul(h1, w_na, 1088, NA_W, "proj_na", out_dtype=BF, first_tile_scale=NA_DH ** -0.5)
        lr = _matmul(h1, w_lr, 1088, LR_COLS, "proj_decay")

        y_conv = _short_conv(u, conv_w, layer)
        wpad_f, bg_f = _decay_weights(gla_wg_f[layer], gla_bg_f[layer], 0)
        wpad_b, bg_b = _decay_weights(gla_wg_b[layer], gla_bg_b[layer], GLA_RANK)
        o_f, o_b = _gla(u, lr, rope_tab, wpad_f, bg_f, wpad_b, bg_b)
        y_na = _neighbourhood_attention(un, _na_bias_tables(na_rpb[layer]), with_ctx=not last)

        rows = NL if last else NT
        x_mid, h2 = _merge(y_conv, o_f, o_b, u, ug, y_na, x_lat, x_ctx, mod, layer, rows,
                           gla_norm_g[layer].reshape(1, GLA_DV), gate_b[layer].reshape(1, N_BRANCH * D_MODEL),
                           w_br_conv[layer].astype(BF), w_br_gla[layer].astype(BF), w_br_na[layer].astype(BF),
                           w_out[layer].astype(BF), ln1_g[layer].reshape(1, D_MODEL),
                           ln1_b[layer].reshape(1, D_MODEL))
        ln_g = ln2_g[layer].reshape(1, D_MODEL)
        ln_b = ln2_b[layer].reshape(1, D_MODEL)
        if layer % 2 == 0:
            j = layer // 2
            x_lat, h1 = _dense_ffn(h2, x_mid, mod, layer, ffn_w_gate[j].astype(BF), ffn_w_up[j].astype(BF),
                                   ffn_w_down[j].astype(BF), ln_g, ln_b)
            x_ctx = x_lat
        else:
            x_lat = _moe_ffn(h2, x_mid, mod, layer, moe_router[layer // 2], moe_w_gate, moe_w_up, moe_w_down,
                             layer // 2, ln_g, ln_b)
    return x_lat.reshape(BATCH, SEQ, D_MODEL)
```

```python
import functools

import numpy as np
import jax
import jax.numpy as jnp
from jax import lax
from jax.experimental import pallas as pl
from jax.experimental.pallas import tpu as pltpu

D_MODEL = 2048
BATCH = 2
SEQ = 4096
DEPTH = 2
GRID_W = 64
CTX_LEN = 256
CONV_DIM = 1024
CONV_K = 3
GLA_HEADS = 4
GLA_DK = 128
GLA_DV = 256
GLA_RANK = 16
GLA_TAU = 16.0
GLA_CHUNK = 64
GLA_QK = GLA_HEADS * GLA_DK
GLA_V = GLA_HEADS * GLA_DV
NA_HEADS = 8
NA_DH = 128
NA_W = NA_HEADS * NA_DH
WIN_R = 8
WIN_C = 16
D_FF = 5632
N_EXPERTS = 8
TOP_K = 2
ROPE_BASE = 10000.0
LN_EPS = 1e-5
N_BRANCH = 3
DEEPNORM_ALPHA = (2 * DEPTH) ** 0.25

NL = BATCH * SEQ
NC = BATCH * CTX_LEN
NT = NL + NC
GRID_ROWS = SEQ // GRID_W

BF = jnp.bfloat16
F32 = jnp.float32

V7X_VMEM_BYTES = 64 * 1024 * 1024
VMEM_LIMIT = V7X_VMEM_BYTES - 4 * 1024 * 1024
NEG_INF = -1e30

U_CONV_B, U_CONV_C, U_CONV_X = 0, 1024, 2048
U_GLA_Q, U_GLA_K, U_GLA_V, U_GLA_R = 3072, 3584, 4096, 5120
U_COLS = 6144
UN_Q, UN_K, UN_V = 0, 1024, 2048
UN_COLS = 3072
W_IN_LR = U_COLS
W_IN_NA = W_IN_LR + 2 * GLA_RANK
W_IN_GATES = W_IN_NA + UN_COLS
LR_COLS = 128

NA_QROWS = 4
NA_KROWS = NA_QROWS + WIN_R
NA_QBLK = NA_QROWS * GRID_W
NA_KBLK = NA_KROWS * GRID_W
NA_KPAIRS = NA_KROWS // 2
NA_BIAS_DR = 2 * WIN_R - 1
NA_PAIR_BOTH = 0
NA_PAIR_LEFT = NA_PAIR_BOTH + NA_BIAS_DR - 1
NA_PAIR_RIGHT = NA_PAIR_LEFT + NA_BIAS_DR
NA_PAIR_NONE = NA_PAIR_RIGHT + NA_BIAS_DR
NA_PAIR_ENTRIES = NA_PAIR_NONE + 1
NA_NBLK = GRID_ROWS // NA_QROWS

MOE_TM = 512
MOE_GB = 256
MOE_NSLOT = NL * TOP_K + N_EXPERTS * MOE_TM
MOE_NBLK = MOE_NSLOT // MOE_TM
MOE_NGB = MOE_NSLOT // MOE_GB
MOE_NTT = NL // MOE_GB
MOE_NPAIR = MOE_NGB + N_EXPERTS * (MOE_NTT - 1)
ROUTER_LANES = 128


def _params(*sem):
    return pltpu.CompilerParams(dimension_semantics=sem, vmem_limit_bytes=VMEM_LIMIT)


def _dot(a, b):
    return jnp.dot(a, b, preferred_element_type=F32)


def _dot_nt(a, b):
    return lax.dot_general(a, b, (((1,), (1,)), ((), ())), preferred_element_type=F32)


def _dot_tn(a, b):
    return lax.dot_general(a, b, (((0,), (0,)), ((), ())), preferred_element_type=F32)


def _silu(x):
    return x * jax.nn.sigmoid(x)


def _layer_norm(x, g, b):
    mu = jnp.mean(x, axis=-1, keepdims=True)
    xc = x - mu
    var = jnp.mean(xc * xc, axis=-1, keepdims=True)
    return xc * lax.rsqrt(var + LN_EPS) * g + b


def _mod_row_of_tile(tile_rows):
    per_batch = SEQ // tile_rows
    return lambda i: jnp.minimum(i // per_batch, BATCH)


def _mod_spec(layer, k, row_of_tile):
    return pl.BlockSpec((None, 1, D_MODEL),
                        lambda i, *_: (layer * 48 + row_of_tile(i) * 6 + k, 0, 0))


def _ada_kernel(c_ref, w_ref, b_ref, o_ref):
    a = _silu(c_ref[...]).astype(BF)
    o_ref[...] = _dot(a, w_ref[...].astype(BF)) + b_ref[...]


def _ada_table(cvec, ada_w, ada_b):
    tn = 1024
    out = pl.pallas_call(
        _ada_kernel,
        out_shape=jax.ShapeDtypeStruct((DEPTH, 8, 6 * D_MODEL), F32),
        grid=(DEPTH, 6 * D_MODEL // tn),
        in_specs=[pl.BlockSpec((8, D_MODEL), lambda l, j: (0, 0)),
                  pl.BlockSpec((None, D_MODEL, tn), lambda l, j: (l, 0, j)),
                  pl.BlockSpec((None, 1, tn), lambda l, j: (l, 0, j))],
        out_specs=pl.BlockSpec((None, 8, tn), lambda l, j: (l, 0, j)),
        compiler_params=_params("parallel", "parallel"),
        name="ada_table",
    )(cvec, ada_w, ada_b.reshape(DEPTH, 1, 6 * D_MODEL))
    return out.reshape(DEPTH * 8 * 6, 1, D_MODEL)


def _select_rows(i, n_lat_tiles, xl_ref, xc_ref):
    return jnp.where(i < n_lat_tiles, xl_ref[...], xc_ref[...])


def _split_row_specs(tm):
    n_lat = NL // tm
    return (pl.BlockSpec((tm, D_MODEL), lambda i: (jnp.minimum(i, n_lat - 1), 0)),
            pl.BlockSpec((tm, D_MODEL), lambda i: (jnp.maximum(i - n_lat, 0), 0)))


def _modulate_kernel(xl_ref, xc_ref, sh_ref, sc_ref, o_ref, *, n_lat_tiles):
    x = _select_rows(pl.program_id(0), n_lat_tiles, xl_ref, xc_ref)
    o_ref[...] = (x * (1.0 + sc_ref[...]) + sh_ref[...]).astype(BF)


def _modulate(x_lat, x_ctx, mod, layer):
    tm = 512
    rot = _mod_row_of_tile(tm)
    return pl.pallas_call(
        functools.partial(_modulate_kernel, n_lat_tiles=NL // tm),
        out_shape=jax.ShapeDtypeStruct((NT, D_MODEL), BF),
        grid=(NT // tm,),
        in_specs=[*_split_row_specs(tm), _mod_spec(layer, 0, rot), _mod_spec(layer, 1, rot)],
        out_specs=pl.BlockSpec((tm, D_MODEL), lambda i: (i, 0)),
        compiler_params=_params("parallel"),
        name="modulate",
    )(x_lat, x_ctx, mod, mod)


def _mm_wt_kernel(a_ref, wt_ref, o_ref, wb_ref, *, first_tile_scale):
    @pl.when(pl.program_id(1) == 0)
    def _():
        wb_ref[...] = wt_ref[0].T.astype(BF)

    acc = _dot(a_ref[...], wb_ref[...])
    if first_tile_scale is not None:
        acc = acc * jnp.where(pl.program_id(0) == 0, first_tile_scale, 1.0)
    o_ref[...] = acc.astype(o_ref.dtype)


def _project(a, wt_stack, layer, col0, n, tm, tn, name, out_dtype=F32, first_tile_scale=None):
    m, k = a.shape
    assert col0 % 8 == 0
    return pl.pallas_call(
        functools.partial(_mm_wt_kernel, first_tile_scale=first_tile_scale),
        out_shape=jax.ShapeDtypeStruct((m, n), out_dtype),
        grid=(n // tn, m // tm),
        in_specs=[pl.BlockSpec((tm, k), lambda j, i: (i, 0)),
                  pl.BlockSpec((pl.Element(1), pl.Element(tn), pl.Element(k)),
                               lambda j, i: (layer, pl.multiple_of(col0 + j * tn, 8), 0))],
        out_specs=pl.BlockSpec((tm, tn), lambda j, i: (i, j)),
        scratch_shapes=[pltpu.VMEM((k, tn), BF)],
        compiler_params=_params("arbitrary", "arbitrary"),
        name=name,
    )(a, wt_stack)


CONV_TM = 256


def _conv_kernel(b_ref, c_ref, x_ref, cp_ref, xp_ref, cn_ref, xn_ref, w_ref, o_ref):
    i = pl.program_id(0)
    tiles_per_seq = SEQ // CONV_TM
    is_ctx = i >= NL // CONV_TM
    is_start = jnp.logical_or(i % tiles_per_seq == 0, is_ctx)
    is_end = jnp.logical_or(i % tiles_per_seq == tiles_per_seq - 1, is_ctx)
    z = c_ref[...] * x_ref[...]
    zp = cp_ref[...] * xp_ref[...]
    zn = cn_ref[...] * xn_ref[...]
    prev_row = jnp.where(is_start, 0.0, zp[7:8, :])
    next_row = jnp.where(is_end, 0.0, zn[0:1, :])
    rows = lax.broadcasted_iota(jnp.int32, z.shape, 0)
    z_prev = jnp.where(rows == 0, prev_row, pltpu.roll(z, 1, 0))
    z_next = jnp.where(rows == CONV_TM - 1, next_row, pltpu.roll(z, CONV_TM - 1, 0))
    w = w_ref[...]
    y = w[0:1, :] * z_prev + w[1:2, :] * z + w[2:3, :] * z_next
    o_ref[...] = (b_ref[...] * y).astype(BF)


def _short_conv(u, conv_w, layer):
    tm = CONV_TM
    n8 = tm // 8
    last8 = NT // 8 - 1
    return pl.pallas_call(
        _conv_kernel,
        out_shape=jax.ShapeDtypeStruct((NT, CONV_DIM), BF),
        grid=(NT // tm,),
        in_specs=[pl.BlockSpec((tm, CONV_DIM), lambda i: (i, U_CONV_B // CONV_DIM)),
                  pl.BlockSpec((tm, CONV_DIM), lambda i: (i, U_CONV_C // CONV_DIM)),
                  pl.BlockSpec((tm, CONV_DIM), lambda i: (i, U_CONV_X // CONV_DIM)),
                  pl.BlockSpec((8, CONV_DIM), lambda i: (jnp.maximum(i * n8 - 1, 0), U_CONV_C // CONV_DIM)),
                  pl.BlockSpec((8, CONV_DIM), lambda i: (jnp.maximum(i * n8 - 1, 0), U_CONV_X // CONV_DIM)),
                  pl.BlockSpec((8, CONV_DIM), lambda i: (jnp.minimum(i * n8 + n8, last8), U_CONV_C // CONV_DIM)),
                  pl.BlockSpec((8, CONV_DIM), lambda i: (jnp.minimum(i * n8 + n8, last8), U_CONV_X // CONV_DIM)),
                  pl.BlockSpec((None, CONV_K, CONV_DIM), lambda i: (layer, 0, 0))],
        out_specs=pl.BlockSpec((tm, CONV_DIM), lambda i: (i, 0)),
        compiler_params=_params("parallel"),
        name="short_conv",
    )(u, u, u, u, u, u, u, conv_w)


GLA_STEPS = (CTX_LEN + SEQ) // GLA_CHUNK
GLA_CTX_STEPS = CTX_LEN // GLA_CHUNK


def _log_sigmoid(z):
    return -(jnp.maximum(-z, 0.0) + jnp.log1p(jnp.exp(-jnp.abs(z))))


def _rope(x, cs, sn):
    lane = lax.broadcasted_iota(jnp.int32, x.shape, 1)
    swapped = jnp.where(lane % 2 == 0, pltpu.roll(x, GLA_DK - 1, 1), pltpu.roll(x, 1, 1))
    return x * cs + swapped * sn


def _gla_decayed_operands(q_ref, k_ref, l_ref, r_ref, w_ref, b_ref, reverse):
    c = GLA_CHUNK
    z = _dot(l_ref[...].astype(BF), w_ref[...]) + b_ref[...]
    g = _log_sigmoid(z) * (1.0 / GLA_TAU)
    t_i = lax.broadcasted_iota(jnp.int32, (c, c), 0)
    s_i = lax.broadcasted_iota(jnp.int32, (c, c), 1)
    keep = (s_i >= t_i) if reverse else (s_i <= t_i)
    tri = jnp.where(keep, 1.0, 0.0).astype(BF)
    g1 = g.astype(BF)
    r1 = g - g1.astype(F32)
    g2 = r1.astype(BF)
    g3 = (r1 - g2.astype(F32)).astype(BF)
    b = _dot(tri, g1) + _dot(tri, g2) + _dot(tri, g3)
    total = b[0:1, :] if reverse else b[c - 1:c, :]
    cs = r_ref[:, :GLA_DK]
    sn = r_ref[:, GLA_DK:]
    heads = [slice(h * GLA_DK, (h + 1) * GLA_DK) for h in range(GLA_HEADS)]
    qs = jnp.concatenate([_rope(q_ref[:, hd] * (GLA_DK ** -0.5), cs, sn) for hd in heads], axis=1)
    kr = jnp.concatenate([_rope(k_ref[:, hd], cs, sn) for hd in heads], axis=1)
    q_dec = (qs * jnp.exp(b)).astype(BF)
    k_inv = (kr * jnp.exp(-b)).astype(BF)
    k_end = (kr * jnp.exp(total - b)).astype(BF)
    return keep, q_dec, k_inv, k_end, jnp.exp(total)


def _gla_kernel(qf_ref, kf_ref, vf_ref, lf_ref, rf_ref, qb_ref, kb_ref, vb_ref, lb_ref, rb_ref,
                wf_ref, bf_ref, wb_ref, bb_ref, of_ref, ob_ref, stf_ref, stb_ref):
    @pl.when(pl.program_id(1) == 0)
    def _():
        stf_ref[...] = jnp.zeros_like(stf_ref)
        stb_ref[...] = jnp.zeros_like(stb_ref)

    pre = [_gla_decayed_operands(qf_ref, kf_ref, lf_ref, rf_ref, wf_ref, bf_ref, False),
           _gla_decayed_operands(qb_ref, kb_ref, lb_ref, rb_ref, wb_ref, bb_ref, True)]
    chains = [(d, h) for d in range(2) for h in range(GLA_HEADS)]
    v_refs, o_refs, st_refs = (vf_ref, vb_ref), (of_ref, ob_ref), (stf_ref, stb_ref)

    def head(x, h):
        return x[:, h * GLA_DK:(h + 1) * GLA_DK]

    v = [v_refs[d][:, h * GLA_DV:(h + 1) * GLA_DV].astype(BF) for d, h in chains]
    st = [st_refs[d][h] for d, h in chains]
    att = [jnp.where(pre[d][0], _dot_nt(head(pre[d][1], h), head(pre[d][2], h)), 0.0).astype(BF)
           for d, h in chains]
    o = [_dot(att[i], v[i]) + _dot_nt(head(pre[d][1], h), st[i].astype(BF)) for i, (d, h) in enumerate(chains)]
    for d in range(2):
        o_refs[d][...] = jnp.concatenate(o[d * GLA_HEADS:(d + 1) * GLA_HEADS], axis=1)
    for i, (d, h) in enumerate(chains):
        st_refs[d][h] = st[i] * head(pre[d][4], h) + _dot_tn(v[i], head(pre[d][3], h))


def _gla(u, lr, rope_tab, wpad_f, bg_f, wpad_b, bg_b):
    c = GLA_CHUNK
    lat_chunks = SEQ // c
    ctx_base = NL // c

    def row_f(b, s):
        return jnp.where(s < GLA_CTX_STEPS, ctx_base + b * GLA_CTX_STEPS + s, b * lat_chunks + s - GLA_CTX_STEPS)

    def row_b(b, s):
        return jnp.where(s < GLA_CTX_STEPS, ctx_base + b * GLA_CTX_STEPS + GLA_CTX_STEPS - 1 - s,
                         b * lat_chunks + GLA_STEPS - 1 - s)

    def rope_f(s):
        return jnp.where(s < GLA_CTX_STEPS, lat_chunks, s - GLA_CTX_STEPS)

    def rope_b(s):
        return jnp.where(s < GLA_CTX_STEPS, lat_chunks, GLA_STEPS - 1 - s)

    def stream(row, rope):
        return [pl.BlockSpec((c, GLA_QK), lambda b, s: (row(b, s), U_GLA_Q // GLA_QK)),
                pl.BlockSpec((c, GLA_QK), lambda b, s: (row(b, s), U_GLA_K // GLA_QK)),
                pl.BlockSpec((c, GLA_V), lambda b, s: (row(b, s), U_GLA_V // GLA_V)),
                pl.BlockSpec((c, LR_COLS), lambda b, s: (row(b, s), 0)),
                pl.BlockSpec((c, 2 * GLA_DK), lambda b, s: (rope(s), 0))]

    head_w = pl.BlockSpec((LR_COLS, GLA_QK), lambda b, s: (0, 0))
    head_b = pl.BlockSpec((1, GLA_QK), lambda b, s: (0, 0))
    state = pltpu.VMEM((GLA_HEADS, GLA_DV, GLA_DK), F32)
    return pl.pallas_call(
        _gla_kernel,
        out_shape=(jax.ShapeDtypeStruct((NT, GLA_V), F32), jax.ShapeDtypeStruct((NT, GLA_V), F32)),
        grid=(BATCH, GLA_STEPS),
        in_specs=stream(row_f, rope_f) + stream(row_b, rope_b) + [head_w, head_b, head_w, head_b],
        out_specs=(pl.BlockSpec((c, GLA_V), lambda b, s: (row_f(b, s), 0)),
                   pl.BlockSpec((c, GLA_V), lambda b, s: (row_b(b, s), 0))),
        scratch_shapes=[state, state],
        compiler_params=_params("parallel", "arbitrary"),
        name="gla",
    )(u, u, u, lr, rope_tab, u, u, u, lr, rope_tab, wpad_f, bg_f, wpad_b, bg_b)


def _rope_table():
    t = jnp.arange(SEQ)
    rows = (t // GRID_W).astype(F32)
    cols = (t % GRID_W).astype(F32)
    n_freq = GLA_DK // 4
    inv = 1.0 / (ROPE_BASE ** (jnp.arange(n_freq, dtype=F32) / n_freq))
    ang = jnp.concatenate([rows[:, None] * inv, cols[:, None] * inv], -1)
    cos = jnp.repeat(jnp.cos(ang), 2, axis=-1)
    sin = jnp.repeat(jnp.sin(ang), 2, axis=-1)
    sign = jnp.tile(jnp.array([-1.0, 1.0], F32), GLA_DK // 2)
    cos = jnp.concatenate([cos, jnp.ones((GRID_W, GLA_DK), F32)], 0)
    sin = jnp.concatenate([sin * sign, jnp.zeros((GRID_W, GLA_DK), F32)], 0)
    return jnp.concatenate([cos, sin], axis=1)


def _decay_weights(wg, bg, lane_offset):
    w = jnp.pad(wg, ((lane_offset, LR_COLS - GLA_RANK - lane_offset), (0, 0))).astype(BF)
    return w, bg.reshape(1, GLA_QK)


NA_HPS = 2


def _na_softmax_out(parts):
    m = functools.reduce(jnp.maximum, [jnp.max(s, axis=-1, keepdims=True) for s, _ in parts])
    ps = [jnp.exp(s - m) for s, _ in parts]
    l = functools.reduce(jnp.add, [jnp.sum(p, axis=-1, keepdims=True) for p in ps])
    o = functools.reduce(jnp.add, [_dot(p.astype(BF), v) for p, (_, v) in zip(ps, parts)])
    return (o / l).astype(BF)


def _na_kernel(q_ref, k_ref, v_ref, kc_ref, vc_ref, bias_ref, o_ref):
    j = pl.program_id(2)
    head_cols = [slice(h * NA_DH, (h + 1) * NA_DH) for h in range(NA_HPS)]

    @pl.when(j < NA_NBLK)
    def _():
        base = jnp.clip(NA_QROWS * j - WIN_R // 2, 0, GRID_ROWS - NA_KROWS)
        start = pl.multiple_of(base * GRID_W, GRID_W)

        def pair_entry(qr, kp):
            r = NA_QROWS * j + qr
            rs = jnp.clip(r - WIN_R // 2, 0, GRID_ROWS - WIN_R)
            a0 = base + 2 * kp
            dr0 = a0 - r + WIN_R - 1
            in0 = jnp.logical_and(a0 >= rs, a0 < rs + WIN_R)
            in1 = jnp.logical_and(a0 + 1 >= rs, a0 + 1 < rs + WIN_R)
            entry = jnp.where(jnp.logical_and(in0, in1), NA_PAIR_BOTH + dr0,
                              jnp.where(in0, NA_PAIR_LEFT + dr0,
                                        jnp.where(in1, NA_PAIR_RIGHT + dr0 + 1, NA_PAIR_NONE)))
            return jnp.clip(entry, 0, NA_PAIR_ENTRIES - 1)

        entries = [[pair_entry(qr, kp) for kp in range(NA_KPAIRS)] for qr in range(NA_QROWS)]
        for h, hc in enumerate(head_cols):
            q = q_ref[:, hc]
            vl = v_ref[pl.ds(start, NA_KBLK), hc]
            bias = jnp.concatenate(
                [jnp.concatenate([bias_ref[h, entries[qr][kp]] for kp in range(NA_KPAIRS)], axis=1)
                 for qr in range(NA_QROWS)], axis=0)
            s_loc = _dot_nt(q, k_ref[pl.ds(start, NA_KBLK), hc]) + bias
            s_ctx = _dot_nt(q, kc_ref[:, hc])
            o_ref[:, hc] = _na_softmax_out([(s_loc, vl), (s_ctx, vc_ref[:, hc])])

    @pl.when(j == NA_NBLK)
    def _():
        for hc in head_cols:
            s_ctx = _dot_nt(q_ref[:, hc], kc_ref[:, hc])
            o_ref[:, hc] = _na_softmax_out([(s_ctx, vc_ref[:, hc])])


def _na_bias_pair_tables(rpb_all):
    qc = np.arange(GRID_W)
    cs = np.clip(qc - WIN_C // 2, 0, GRID_W - WIN_C)
    kc = np.arange(GRID_W)
    col_ok = (kc[None, :] >= cs[:, None]) & (kc[None, :] < cs[:, None] + WIN_C)
    dc = kc[None, :] - qc[:, None] + WIN_C - 1
    col_sel = ((dc[None] == np.arange(2 * WIN_C - 1)[:, None, None]) & col_ok[None]).astype(np.float32)
    t = jnp.einsum('lhrd,dxy->lhrxy', rpb_all, jnp.asarray(col_sel), precision=lax.Precision.HIGHEST)
    t = jnp.where(jnp.asarray(col_ok), t, NEG_INF)
    masked = jnp.full_like(t, NEG_INF)
    both = jnp.concatenate([t[:, :, :-1], t[:, :, 1:]], axis=-1)
    left = jnp.concatenate([t, masked], axis=-1)
    right = jnp.concatenate([masked, t], axis=-1)
    none = jnp.concatenate([masked[:, :, :1], masked[:, :, :1]], axis=-1)
    return jnp.concatenate([both, left, right, none], axis=2)


def _neighbourhood_attention(un, bias_pairs, layer, with_ctx):
    steps = NA_NBLK + (1 if with_ctx else 0)
    ctx_q = NL // NA_QBLK
    w = NA_HPS * NA_DH

    def q_row(b, j):
        return jnp.where(j < NA_NBLK, b * NA_NBLK + j, ctx_q + b)

    ctx_blk = NL // CTX_LEN
    return pl.pallas_call(
        _na_kernel,
        out_shape=jax.ShapeDtypeStruct((NT, NA_W), BF),
        grid=(BATCH, NA_HEADS // NA_HPS, steps),
        in_specs=[pl.BlockSpec((NA_QBLK, w), lambda b, h, j: (q_row(b, j), UN_Q // w + h)),
                  pl.BlockSpec((SEQ, w), lambda b, h, j: (b, UN_K // w + h)),
                  pl.BlockSpec((SEQ, w), lambda b, h, j: (b, UN_V // w + h)),
                  pl.BlockSpec((CTX_LEN, w), lambda b, h, j: (ctx_blk + b, UN_K // w + h)),
                  pl.BlockSpec((CTX_LEN, w), lambda b, h, j: (ctx_blk + b, UN_V // w + h)),
                  pl.BlockSpec((None, NA_HPS, NA_PAIR_ENTRIES, GRID_W, 2 * GRID_W),
                               lambda b, h, j: (layer, h, 0, 0, 0))],
        out_specs=pl.BlockSpec((NA_QBLK, w), lambda b, h, j: (q_row(b, j), h)),
        compiler_params=_params("parallel", "parallel", "arbitrary"),
        name="neighbourhood_attention",
    )(un, un, un, un, un, bias_pairs)


MERGE_TM = 256


def _merge_kernel(yc_ref, of_ref, ob_ref, r_ref, yn_ref, gt_ref, xl_ref, xc_ref,
                  ng_ref, gb_ref, wc_ref, wg_ref, wn_ref, wo_ref,
                  g1_ref, lg_ref, lb_ref, sh2_ref, sc2_ref, xo_ref, ho_ref):
    x = _select_rows(pl.program_id(0), NL // MERGE_TM, xl_ref, xc_ref)
    o = of_ref[...] + ob_ref[...]
    r = r_ref[...]
    ng = ng_ref[...]
    heads = []
    for h in range(GLA_HEADS):
        oh = o[:, h * GLA_DV:(h + 1) * GLA_DV]
        oh = oh * lax.rsqrt(jnp.mean(oh * oh, axis=-1, keepdims=True) + LN_EPS)
        heads.append(oh * ng * _silu(r[:, h * GLA_DV:(h + 1) * GLA_DV]))
    y_gla = jnp.concatenate(heads, axis=-1).astype(BF)
    g = jax.nn.sigmoid(gt_ref[...] + gb_ref[...])
    z = (g[:, :D_MODEL] * _dot(yc_ref[...], wc_ref[...])
         + g[:, D_MODEL:2 * D_MODEL] * _dot(y_gla, wg_ref[...])
         + g[:, 2 * D_MODEL:] * _dot(yn_ref[...], wn_ref[...]))
    y = _dot(z.astype(BF), wo_ref[...])
    xn = _layer_norm(DEEPNORM_ALPHA * x + g1_ref[...] * y, lg_ref[...], lb_ref[...])
    xo_ref[...] = xn
    ho_ref[...] = (xn * (1.0 + sc2_ref[...]) + sh2_ref[...]).astype(BF)


def _merge(y_conv, o_f, o_b, u, ug, y_na, x_lat, x_ctx, mod, layer, rows, norm_g, gate_b, w_conv, w_gla, w_na,
           w_out, ln_g, ln_b):
    tm = MERGE_TM
    rot = _mod_row_of_tile(tm)
    row = lambda width: pl.BlockSpec((tm, width), lambda i: (i, 0))
    const = lambda shape: pl.BlockSpec(shape, lambda i: (0,) * len(shape), pipeline_mode=pl.Buffered(1))
    return pl.pallas_call(
        _merge_kernel,
        out_shape=(jax.ShapeDtypeStruct((rows, D_MODEL), F32), jax.ShapeDtypeStruct((rows, D_MODEL), BF)),
        grid=(rows // tm,),
        in_specs=[row(CONV_DIM), row(GLA_V), row(GLA_V),
                  pl.BlockSpec((tm, GLA_V), lambda i: (i, U_GLA_R // GLA_V)),
                  row(NA_W), row(N_BRANCH * D_MODEL), *_split_row_specs(tm),
                  const((1, GLA_DV)), const((1, N_BRANCH * D_MODEL)),
                  const((CONV_DIM, D_MODEL)), const((GLA_V, D_MODEL)), const((NA_W, D_MODEL)),
                  const((D_MODEL, D_MODEL)),
                  _mod_spec(layer, 2, rot), const((1, D_MODEL)), const((1, D_MODEL)),
                  _mod_spec(layer, 3, rot), _mod_spec(layer, 4, rot)],
        out_specs=(row(D_MODEL), row(D_MODEL)),
        compiler_params=_params("parallel"),
        name="merge",
    )(y_conv, o_f, o_b, u, y_na, ug, x_lat, x_ctx, norm_g, gate_b, w_conv, w_gla, w_na, w_out,
      mod, ln_g, ln_b, mod, mod)


FFN_TM = 512
FFN_TF = 512


def _ffn_kernel(h_ref, wg_ref, wu_ref, wd_ref, x_ref, g2_ref, lg_ref, lb_ref, shn_ref, scn_ref,
                xo_ref, ho_ref, acc_ref):
    j = pl.program_id(1)

    @pl.when(j == 0)
    def _():
        acc_ref[...] = jnp.zeros_like(acc_ref)

    h = h_ref[...]
    a = _silu(_dot(h, wg_ref[...])) * _dot(h, wu_ref[...])
    acc_ref[...] += _dot(a.astype(BF), wd_ref[...])

    @pl.when(j == pl.num_programs(1) - 1)
    def _():
        xn = _layer_norm(DEEPNORM_ALPHA * x_ref[...] + g2_ref[...] * acc_ref[...], lg_ref[...], lb_ref[...])
        xo_ref[...] = xn
        ho_ref[...] = (xn * (1.0 + scn_ref[...]) + shn_ref[...]).astype(BF)


def _dense_ffn(h2, x_all, mod, layer, wg, wu, wd, ln_g, ln_b):
    tm, tf = FFN_TM, FFN_TF
    rot = _mod_row_of_tile(tm)
    row = lambda: pl.BlockSpec((tm, D_MODEL), lambda i, j: (i, 0))
    const = lambda: pl.BlockSpec((1, D_MODEL), lambda i, j: (0, 0))
    return pl.pallas_call(
        _ffn_kernel,
        out_shape=(jax.ShapeDtypeStruct((NT, D_MODEL), F32), jax.ShapeDtypeStruct((NT, D_MODEL), BF)),
        grid=(NT // tm, D_FF // tf),
        in_specs=[row(),
                  pl.BlockSpec((D_MODEL, tf), lambda i, j: (0, j)),
                  pl.BlockSpec((D_MODEL, tf), lambda i, j: (0, j)),
                  pl.BlockSpec((tf, D_MODEL), lambda i, j: (j, 0)),
                  row(), _mod_spec(layer, 5, rot), const(), const(),
                  _mod_spec(layer + 1, 0, rot), _mod_spec(layer + 1, 1, rot)],
        out_specs=(row(), row()),
        scratch_shapes=[pltpu.VMEM((tm, D_MODEL), F32)],
        compiler_params=_params("parallel", "arbitrary"),
        name="dense_ffn",
    )(h2, wg, wu, wd, x_all, mod, ln_g, ln_b, mod, mod)


def _router_kernel(h_ref, w_ref, idx_ref, wt_ref):
    logits = _dot(h_ref[...], w_ref[...])
    lane = lax.broadcasted_iota(jnp.int32, logits.shape, 1)
    logits = jnp.where(lane < N_EXPERTS, logits, -jnp.inf)
    m1 = jnp.max(logits, axis=-1, keepdims=True)
    i1 = jnp.min(jnp.where(logits == m1, lane, ROUTER_LANES), axis=-1, keepdims=True)
    rest = jnp.where(lane == i1, -jnp.inf, logits)
    m2 = jnp.max(rest, axis=-1, keepdims=True)
    i2 = jnp.min(jnp.where(rest == m2, lane, ROUTER_LANES), axis=-1, keepdims=True)
    e2 = jnp.exp(m2 - m1)
    w1 = 1.0 / (1.0 + e2)
    w2 = e2 / (1.0 + e2)
    idx_ref[...] = jnp.where(lane == 0, i1, jnp.where(lane == 1, i2, 0))
    wt_ref[...] = jnp.where(lane == 0, w1, jnp.where(lane == 1, w2, 0.0))


def _router(h2, router_w):
    tm = 512
    return pl.pallas_call(
        _router_kernel,
        out_shape=(jax.ShapeDtypeStruct((NL, ROUTER_LANES), jnp.int32),
                   jax.ShapeDtypeStruct((NL, ROUTER_LANES), F32)),
        grid=(NL // tm,),
        in_specs=[pl.BlockSpec((tm, D_MODEL), lambda i: (i, 0)),
                  pl.BlockSpec((D_MODEL, ROUTER_LANES), lambda i: (0, 0))],
        out_specs=(pl.BlockSpec((tm, ROUTER_LANES), lambda i: (i, 0)),
                   pl.BlockSpec((tm, ROUTER_LANES), lambda i: (i, 0))),
        compiler_params=_params("parallel"),
        name="moe_router",
    )(h2, router_w)


def _dispatch_plan(idx):
    i32 = jnp.int32
    experts = jnp.arange(N_EXPERTS, dtype=i32)
    oh_a = (idx[:, 0:1] == experts).astype(i32)
    oh_b = (idx[:, 1:2] == experts).astype(i32)
    cum = jnp.cumsum(oh_a + oh_b, axis=0)
    counts = cum[-1]
    region = ((counts + MOE_TM - 1) // MOE_TM) * MOE_TM
    g_end = jnp.cumsum(region)
    g_start = g_end - region
    slot_table = g_start[None, :] + cum - 1
    slot_a = jnp.sum(oh_a * slot_table, axis=1)
    slot_b = jnp.sum(oh_b * slot_table, axis=1)

    tile_start = jnp.arange(MOE_NBLK, dtype=i32) * MOE_TM
    tile_expert = jnp.minimum(jnp.sum((tile_start[:, None] >= g_end[None, :]).astype(i32), axis=1),
                              N_EXPERTS - 1)
    n_tiles = jnp.maximum(g_end[-1] // MOE_TM, 1).reshape(1)
    tile_expert = (tile_expert, n_tiles)

    pair_ids = jnp.arange(MOE_NPAIR, dtype=i32)
    tile_last = jnp.arange(MOE_NTT, dtype=i32) * MOE_GB + MOE_GB - 1
    hi = cum[tile_last]
    lo = jnp.concatenate([jnp.zeros((1, N_EXPERTS), i32), hi[:-1]], axis=0)

    blk_start = jnp.arange(MOE_NGB, dtype=i32) * MOE_GB
    blk_e = jnp.minimum(jnp.sum((blk_start[:, None] >= g_end[None, :]).astype(i32), axis=1), N_EXPERTS - 1)
    r0 = blk_start - g_start[blk_e]
    r1 = jnp.minimum(r0 + MOE_GB, counts[blk_e])
    has = r1 > r0
    hi_e = hi[:, blk_e]
    t_lo = jnp.sum((hi_e <= r0[None, :]).astype(i32), axis=0)
    t_hi = jnp.sum((hi_e < r1[None, :]).astype(i32), axis=0)
    t_lo = jnp.where(has, t_lo, 0)
    t_end = jnp.where(has, t_hi + 1, 0)
    dispatch = (t_lo, t_end, (n_tiles * (MOE_TM // MOE_GB)).astype(i32))

    b_lo = (g_start[None, :] + lo) // MOE_GB
    b_hi = (g_start[None, :] + hi - 1) // MOE_GB
    n_c = jnp.where(hi > lo, b_hi - b_lo + 1, 0).reshape(-1)
    b_lo = b_lo.reshape(-1)
    c_end = jnp.cumsum(n_c)
    c_total = c_end[-1]
    c_idx = jnp.minimum(jnp.sum((pair_ids[:, None] >= c_end[None, :]).astype(i32), axis=1),
                        MOE_NTT * N_EXPERTS - 1)
    c_blk = jnp.clip(b_lo[c_idx] + pair_ids - (c_end[c_idx] - n_c[c_idx]), 0, MOE_NGB - 1)
    c_tile = c_idx // N_EXPERTS
    c_valid = pair_ids < c_total
    c_tile = jnp.where(c_valid, c_tile, MOE_NTT - 1)
    c_blk = jnp.where(c_valid, c_blk, c_blk[jnp.maximum(c_total - 1, 0)])
    c_prev = jnp.concatenate([jnp.full((1,), -1, i32), c_tile[:-1]])
    c_next = jnp.concatenate([c_tile[1:], jnp.full((1,), -1, i32)])
    c_first = c_valid & (c_tile != c_prev)
    c_last = c_valid & ((c_tile != c_next) | (pair_ids == c_total - 1))
    combine = (c_tile, c_blk, c_first.astype(i32), c_last.astype(i32), c_valid.astype(i32))
    return slot_a, slot_b, tile_expert, dispatch, combine


def _dispatch_kernel(lo_ref, end_ref, n_ref, h_ref, sa_ref, sb_ref, wa_ref, wb_ref, hs_ref, ws_ref,
                     acc_ref, wacc_ref):
    blk = pl.program_id(0)

    @pl.when(blk < n_ref[0])
    def _():
        acc_ref[...] = jnp.zeros_like(acc_ref)
        wacc_ref[...] = jnp.zeros_like(wacc_ref)
        slots = blk * MOE_GB + lax.broadcasted_iota(jnp.int32, (MOE_GB, MOE_GB), 0)

        def scan_tile(t, carry):
            rows = pl.ds(pl.multiple_of(t * MOE_GB, MOE_GB), MOE_GB)
            hit_a = slots == sa_ref[t]
            hit_b = slots == sb_ref[t]
            onehot = jnp.where(hit_a, 1.0, jnp.where(hit_b, 1.0, 0.0)).astype(BF)
            acc_ref[...] += _dot(onehot, h_ref[rows, :])
            w = jnp.where(hit_a, wa_ref[t], jnp.where(hit_b, wb_ref[t], 0.0))
            wacc_ref[...] += jnp.broadcast_to(jnp.sum(w, axis=1, keepdims=True), wacc_ref.shape)
            return carry

        lax.fori_loop(lo_ref[blk], end_ref[blk], scan_tile, 0)
        hs_ref[...] = acc_ref[...].astype(BF)
        ws_ref[...] = wacc_ref[...]


def _dispatch(h2, slot_a, slot_b, w_a, w_b, plan):
    used = lambda i, lo, end, n: (jnp.minimum(i, n[0] - 1), 0)
    resident = lambda shape: pl.BlockSpec(shape, lambda i, *_: (0,) * len(shape), pipeline_mode=pl.Buffered(1))
    shape3 = (MOE_NTT, 1, MOE_GB)
    grid_spec = pltpu.PrefetchScalarGridSpec(
        num_scalar_prefetch=3,
        grid=(MOE_NGB,),
        in_specs=[resident((NL, D_MODEL)), resident(shape3), resident(shape3), resident(shape3),
                  resident(shape3)],
        out_specs=(pl.BlockSpec((MOE_GB, D_MODEL), used), pl.BlockSpec((MOE_GB, ROUTER_LANES), used)),
        scratch_shapes=[pltpu.VMEM((MOE_GB, D_MODEL), F32), pltpu.VMEM((MOE_GB, ROUTER_LANES), F32)])
    return pl.pallas_call(
        _dispatch_kernel,
        out_shape=(jax.ShapeDtypeStruct((MOE_NSLOT, D_MODEL), BF),
                   jax.ShapeDtypeStruct((MOE_NSLOT, ROUTER_LANES), F32)),
        grid_spec=grid_spec,
        compiler_params=_params("arbitrary"),
        name="moe_dispatch",
    )(*plan, h2, slot_a.reshape(shape3), slot_b.reshape(shape3), w_a.reshape(shape3), w_b.reshape(shape3))


def _tile_flags(e_ref, n_ref):
    i = pl.program_id(1)
    valid = i < n_ref[0]
    changed = jnp.logical_or(i == 0, e_ref[i] != e_ref[jnp.maximum(i - 1, 0)])
    return valid, jnp.logical_and(valid, changed)


def _used_tile(i, n_ref):
    return jnp.minimum(i, n_ref[0] - 1)


def _expert_up_kernel(e_ref, n_ref, hs_ref, wg_ref, wu_ref, a_ref, wgs_ref, wus_ref):
    valid, changed = _tile_flags(e_ref, n_ref)

    @pl.when(changed)
    def _():
        wgs_ref[...] = wg_ref[...].astype(BF)
        wus_ref[...] = wu_ref[...].astype(BF)

    @pl.when(valid)
    def _():
        h = hs_ref[...]
        a_ref[...] = (_silu(_dot(h, wgs_ref[...])) * _dot(h, wus_ref[...])).astype(BF)


def _expert_up(hs, tile_plan, wg, wu, moe_layer):
    tf = 512
    w_spec = pl.BlockSpec((None, None, D_MODEL, tf), lambda j, i, e, n: (moe_layer, e[_used_tile(i, n)], 0, j))
    grid_spec = pltpu.PrefetchScalarGridSpec(
        num_scalar_prefetch=2,
        grid=(D_FF // tf, MOE_NBLK),
        in_specs=[pl.BlockSpec((MOE_TM, D_MODEL), lambda j, i, e, n: (_used_tile(i, n), 0)), w_spec, w_spec],
        out_specs=pl.BlockSpec((MOE_TM, tf), lambda j, i, e, n: (_used_tile(i, n), j)),
        scratch_shapes=[pltpu.VMEM((D_MODEL, tf), BF), pltpu.VMEM((D_MODEL, tf), BF)])
    return pl.pallas_call(
        _expert_up_kernel,
        out_shape=jax.ShapeDtypeStruct((MOE_NSLOT, D_FF), BF),
        grid_spec=grid_spec,
        compiler_params=_params("arbitrary", "arbitrary"),
        name="moe_expert_up",
    )(*tile_plan, hs, wg, wu)


def _expert_down_kernel(e_ref, n_ref, a_ref, wd_ref, ws_ref, y_ref, wds_ref):
    valid, changed = _tile_flags(e_ref, n_ref)

    @pl.when(changed)
    def _():
        wds_ref[...] = wd_ref[...].astype(BF)

    @pl.when(valid)
    def _():
        y_ref[...] = (_dot(a_ref[...], wds_ref[...]) * ws_ref[:, 0:1]).astype(BF)


def _expert_down(a, tile_plan, wd, ws, moe_layer):
    tn = 512
    out_spec = pl.BlockSpec((MOE_TM, tn), lambda j, i, e, n: (_used_tile(i, n), j))
    grid_spec = pltpu.PrefetchScalarGridSpec(
        num_scalar_prefetch=2,
        grid=(D_MODEL // tn, MOE_NBLK),
        in_specs=[pl.BlockSpec((MOE_TM, D_FF), lambda j, i, e, n: (_used_tile(i, n), 0)),
                  pl.BlockSpec((None, None, D_FF, tn),
                               lambda j, i, e, n: (moe_layer, e[_used_tile(i, n)], 0, j)),
                  pl.BlockSpec((MOE_TM, ROUTER_LANES), lambda j, i, e, n: (_used_tile(i, n), 0))],
        out_specs=out_spec,
        scratch_shapes=[pltpu.VMEM((D_FF, tn), BF)])
    return pl.pallas_call(
        _expert_down_kernel,
        out_shape=jax.ShapeDtypeStruct((MOE_NSLOT, D_MODEL), BF),
        grid_spec=grid_spec,
        compiler_params=_params("arbitrary", "arbitrary"),
        name="moe_expert_down",
    )(*tile_plan, a, wd, ws)


def _combine_kernel(tile_ref, blk_ref, first_ref, last_ref, valid_ref,
                    y_ref, sa_ref, sb_ref, x_ref, g2_ref, lg_ref, lb_ref, o_ref, acc_ref):
    p = pl.program_id(0)

    @pl.when(first_ref[p] == 1)
    def _():
        acc_ref[...] = jnp.zeros_like(acc_ref)

    @pl.when(valid_ref[p] == 1)
    def _():
        slots = blk_ref[p] * MOE_GB + lax.broadcasted_iota(jnp.int32, (MOE_GB, MOE_GB), 1)
        onehot = jnp.where(slots == sa_ref[...], 1.0, jnp.where(slots == sb_ref[...], 1.0, 0.0)).astype(BF)
        acc_ref[...] += _dot(onehot, y_ref[...])

    @pl.when(last_ref[p] == 1)
    def _():
        o_ref[...] = _layer_norm(DEEPNORM_ALPHA * x_ref[...] + g2_ref[...] * acc_ref[...],
                                 lg_ref[...], lb_ref[...])


def _combine(y, slot_a, slot_b, x_lat, mod, layer, ln_g, ln_b, plan):
    per_batch = SEQ // MOE_GB
    slot_blk = lambda p, tile, blk, *_: (blk[p], 0)
    tok_blk = lambda p, tile, *_: (tile[p], 0)
    grid_spec = pltpu.PrefetchScalarGridSpec(
        num_scalar_prefetch=5,
        grid=(MOE_NPAIR,),
        in_specs=[pl.BlockSpec((MOE_GB, D_MODEL), slot_blk),
                  pl.BlockSpec((MOE_GB, 1), tok_blk), pl.BlockSpec((MOE_GB, 1), tok_blk),
                  pl.BlockSpec((MOE_GB, D_MODEL), tok_blk),
                  pl.BlockSpec((None, 1, D_MODEL),
                               lambda p, tile, *_: (layer * 48 + (tile[p] // per_batch) * 6 + 5, 0, 0)),
                  pl.BlockSpec((1, D_MODEL), lambda p, *_: (0, 0)),
                  pl.BlockSpec((1, D_MODEL), lambda p, *_: (0, 0))],
        out_specs=pl.BlockSpec((MOE_GB, D_MODEL), tok_blk),
        scratch_shapes=[pltpu.VMEM((MOE_GB, D_MODEL), F32)])
    return pl.pallas_call(
        _combine_kernel,
        out_shape=jax.ShapeDtypeStruct((NL, D_MODEL), F32),
        grid_spec=grid_spec,
        compiler_params=_params("arbitrary"),
        name="moe_combine",
    )(*plan, y, slot_a.reshape(NL, 1), slot_b.reshape(NL, 1), x_lat, mod, ln_g, ln_b)


def _moe_ffn(h2, x_lat, mod, layer, router_w, wg, wu, wd, moe_layer, ln_g, ln_b):
    router_pad = jnp.pad(router_w, ((0, 0), (0, ROUTER_LANES - N_EXPERTS))).astype(BF)
    idx, wts = _router(h2, router_pad)
    slot_a, slot_b, tile_plan, d_plan, c_plan = _dispatch_plan(idx[:, :TOP_K])
    hs, ws = _dispatch(h2, slot_a, slot_b, wts[:, 0], wts[:, 1], d_plan)
    a = _expert_up(hs, tile_plan, wg, wu, moe_layer)
    y = _expert_down(a, tile_plan, wd, ws, moe_layer)
    return _combine(y, slot_a, slot_b, x_lat, mod, layer, ln_g, ln_b, c_plan)


def kernel(x, c, ctx, c_ctx, ada_w, ada_b, w_in, conv_w, gla_wg_f, gla_bg_f, gla_wg_b, gla_bg_b, gla_norm_g, na_rpb, w_br_conv, w_br_gla, w_br_na, gate_b, w_out, ln1_g, ln1_b, ln2_g, ln2_b, ffn_w_gate, ffn_w_up, ffn_w_down, moe_router, moe_w_gate, moe_w_up, moe_w_down):
    assert DEPTH == 2 and x.shape == (BATCH, SEQ, D_MODEL) and ctx.shape == (BATCH, CTX_LEN, D_MODEL)
    cvec = jnp.concatenate([c, c_ctx[None, :], jnp.zeros((8 - BATCH - 1, D_MODEL), F32)], axis=0)
    mod = _ada_table(cvec, ada_w, ada_b)
    rope_tab = _rope_table()
    x_lat, x_ctx = x.reshape(NL, D_MODEL), ctx.reshape(NC, D_MODEL)
    h1 = _modulate(x_lat, x_ctx, mod, 0)
    w_in_t = jnp.swapaxes(w_in, 1, 2)
    na_bias = _na_bias_pair_tables(na_rpb)

    for layer in range(DEPTH):
        last = layer == DEPTH - 1
        u = _project(h1, w_in_t, layer, 0, U_COLS, 1088, 1024, "proj_in")
        lr = _project(h1, w_in_t, layer, W_IN_LR, LR_COLS, 1088, LR_COLS, "proj_decay")
        un = _project(h1, w_in_t, layer, W_IN_NA, UN_COLS, 1088, NA_W, "proj_na", out_dtype=BF,
                      first_tile_scale=NA_DH ** -0.5)
        ug = _project(h1, w_in_t, layer, W_IN_GATES, N_BRANCH * D_MODEL, 1088, 1024, "proj_gates")

        y_conv = _short_conv(u, conv_w, layer)
        wpad_f, bg_f = _decay_weights(gla_wg_f[layer], gla_bg_f[layer], 0)
        wpad_b, bg_b = _decay_weights(gla_wg_b[layer], gla_bg_b[layer], GLA_RANK)
        o_f, o_b = _gla(u, lr, rope_tab, wpad_f, bg_f, wpad_b, bg_b)
        y_na = _neighbourhood_attention(un, na_bias, layer, with_ctx=not last)

        rows = NL if last else NT
        x_mid, h2 = _merge(y_conv, o_f, o_b, u, ug, y_na, x_lat, x_ctx, mod, layer, rows,
                           gla_norm_g[layer].reshape(1, GLA_DV), gate_b[layer].reshape(1, N_BRANCH * D_MODEL),
                           w_br_conv[layer].astype(BF), w_br_gla[layer].astype(BF), w_br_na[layer].astype(BF),
                           w_out[layer].astype(BF), ln1_g[layer].reshape(1, D_MODEL),
                           ln1_b[layer].reshape(1, D_MODEL))
        ln_g = ln2_g[layer].reshape(1, D_MODEL)
        ln_b = ln2_b[layer].reshape(1, D_MODEL)
        if layer % 2 == 0:
            j = layer // 2
            x_lat, h1 = _dense_ffn(h2, x_mid, mod, layer, ffn_w_gate[j].astype(BF), ffn_w_up[j].astype(BF),
                                   ffn_w_down[j].astype(BF), ln_g, ln_b)
            x_ctx = x_lat
        else:
            x_lat = _moe_ffn(h2, x_mid, mod, layer, moe_router[layer // 2], moe_w_gate, moe_w_up, moe_w_down,
                             layer // 2, ln_g, ln_b)
    return x_lat.reshape(BATCH, SEQ, D_MODEL)
```

```python
import functools

import numpy as np
import jax
import jax.numpy as jnp
from jax import lax
from jax.experimental import pallas as pl
from jax.experimental.pallas import tpu as pltpu

D_MODEL = 2048
BATCH = 2
SEQ = 4096
DEPTH = 2
GRID_W = 64
CTX_LEN = 256
CONV_DIM = 1024
CONV_K = 3
GLA_HEADS = 4
GLA_DK = 128
GLA_DV = 256
GLA_RANK = 16
GLA_TAU = 16.0
GLA_CHUNK = 64
GLA_QK = GLA_HEADS * GLA_DK
GLA_V = GLA_HEADS * GLA_DV
NA_HEADS = 8
NA_DH = 128
NA_W = NA_HEADS * NA_DH
WIN_R = 8
WIN_C = 16
D_FF = 5632
N_EXPERTS = 8
TOP_K = 2
ROPE_BASE = 10000.0
LN_EPS = 1e-5
N_BRANCH = 3
DEEPNORM_ALPHA = (2 * DEPTH) ** 0.25

NL = BATCH * SEQ
NC = BATCH * CTX_LEN
NT = NL + NC
GRID_ROWS = SEQ // GRID_W

BF = jnp.bfloat16
F32 = jnp.float32

V7X_VMEM_BYTES = 64 * 1024 * 1024
VMEM_LIMIT = V7X_VMEM_BYTES - 4 * 1024 * 1024
NEG_INF = -1e30

U_CONV_B, U_CONV_C, U_CONV_X = 0, 1024, 2048
U_GLA_Q, U_GLA_K, U_GLA_V, U_GLA_R = 3072, 3584, 4096, 5120
U_COLS = 6144
UN_Q, UN_K, UN_V = 0, 1024, 2048
UN_COLS = 3072
W_IN_LR = U_COLS
W_IN_NA = W_IN_LR + 2 * GLA_RANK
W_IN_GATES = W_IN_NA + UN_COLS
LR_COLS = 128

NA_QROWS = 4
NA_KROWS = NA_QROWS + WIN_R
NA_QBLK = NA_QROWS * GRID_W
NA_KBLK = NA_KROWS * GRID_W
NA_KPAIRS = NA_KROWS // 2
NA_BIAS_DR = 2 * WIN_R - 1
NA_PAIR_BOTH = 0
NA_PAIR_LEFT = NA_PAIR_BOTH + NA_BIAS_DR - 1
NA_PAIR_RIGHT = NA_PAIR_LEFT + NA_BIAS_DR
NA_PAIR_NONE = NA_PAIR_RIGHT + NA_BIAS_DR
NA_PAIR_ENTRIES = NA_PAIR_NONE + 1
NA_NBLK = GRID_ROWS // NA_QROWS

MOE_GB = 256
MOE_SM = 9 * MOE_GB
MOE_CH = 4 * MOE_GB
MOE_TF = 256
MOE_NST = (NL * TOP_K) // MOE_SM + N_EXPERTS
MOE_NSLOT = MOE_NST * MOE_SM
MOE_NGB = MOE_NSLOT // MOE_GB
MOE_NTT = NL // MOE_GB
MOE_NPAIR = (NL * TOP_K) // MOE_GB + N_EXPERTS + N_EXPERTS * (MOE_NTT - 1)
ROUTER_LANES = 128


def _params(*sem):
    return pltpu.CompilerParams(dimension_semantics=sem, vmem_limit_bytes=VMEM_LIMIT)


def _dot(a, b):
    return jnp.dot(a, b, preferred_element_type=F32)


def _dot_nt(a, b):
    return lax.dot_general(a, b, (((1,), (1,)), ((), ())), preferred_element_type=F32)


def _dot_tn(a, b):
    return lax.dot_general(a, b, (((0,), (0,)), ((), ())), preferred_element_type=F32)


def _silu(x):
    return x * jax.nn.sigmoid(x)


def _layer_norm(x, g, b):
    mu = jnp.mean(x, axis=-1, keepdims=True)
    xc = x - mu
    var = jnp.mean(xc * xc, axis=-1, keepdims=True)
    return xc * lax.rsqrt(var + LN_EPS) * g + b


def _mod_row_of_tile(tile_rows):
    per_batch = SEQ // tile_rows
    return lambda i: jnp.minimum(i // per_batch, BATCH)


def _mod_spec(layer, k, row_of_tile):
    return pl.BlockSpec((None, 1, D_MODEL),
                        lambda i, *_: (layer * 48 + row_of_tile(i) * 6 + k, 0, 0))


def _ada_kernel(c_ref, w_ref, b_ref, o_ref):
    a = _silu(c_ref[...]).astype(BF)
    o_ref[...] = _dot(a, w_ref[...].astype(BF)) + b_ref[...]


def _ada_table(cvec, ada_w, ada_b):
    tn = 1024
    out = pl.pallas_call(
        _ada_kernel,
        out_shape=jax.ShapeDtypeStruct((DEPTH, 8, 6 * D_MODEL), F32),
        grid=(DEPTH, 6 * D_MODEL // tn),
        in_specs=[pl.BlockSpec((8, D_MODEL), lambda l, j: (0, 0)),
                  pl.BlockSpec((None, D_MODEL, tn), lambda l, j: (l, 0, j)),
                  pl.BlockSpec((None, 1, tn), lambda l, j: (l, 0, j))],
        out_specs=pl.BlockSpec((None, 8, tn), lambda l, j: (l, 0, j)),
        compiler_params=_params("parallel", "parallel"),
        name="ada_table",
    )(cvec, ada_w, ada_b.reshape(DEPTH, 1, 6 * D_MODEL))
    return out.reshape(DEPTH * 8 * 6, 1, D_MODEL)


def _select_rows(i, n_lat_tiles, xl_ref, xc_ref):
    return jnp.where(i < n_lat_tiles, xl_ref[...], xc_ref[...])


def _split_row_specs(tm):
    n_lat = NL // tm
    return (pl.BlockSpec((tm, D_MODEL), lambda i: (jnp.minimum(i, n_lat - 1), 0)),
            pl.BlockSpec((tm, D_MODEL), lambda i: (jnp.maximum(i - n_lat, 0), 0)))


def _modulate_kernel(xl_ref, xc_ref, sh_ref, sc_ref, o_ref, *, n_lat_tiles):
    x = _select_rows(pl.program_id(0), n_lat_tiles, xl_ref, xc_ref)
    o_ref[...] = (x * (1.0 + sc_ref[...]) + sh_ref[...]).astype(BF)


def _modulate(x_lat, x_ctx, mod, layer):
    tm = 512
    rot = _mod_row_of_tile(tm)
    return pl.pallas_call(
        functools.partial(_modulate_kernel, n_lat_tiles=NL // tm),
        out_shape=jax.ShapeDtypeStruct((NT, D_MODEL), BF),
        grid=(NT // tm,),
        in_specs=[*_split_row_specs(tm), _mod_spec(layer, 0, rot), _mod_spec(layer, 1, rot)],
        out_specs=pl.BlockSpec((tm, D_MODEL), lambda i: (i, 0)),
        compiler_params=_params("parallel"),
        name="modulate",
    )(x_lat, x_ctx, mod, mod)


def _mm_wt_kernel(a_ref, wt_ref, o_ref, wb_ref, *, first_tile_scale):
    @pl.when(pl.program_id(1) == 0)
    def _():
        wb_ref[...] = wt_ref[0].T.astype(BF)

    acc = _dot(a_ref[...], wb_ref[...])
    if first_tile_scale is not None:
        acc = acc * jnp.where(pl.program_id(0) == 0, first_tile_scale, 1.0)
    o_ref[...] = acc.astype(o_ref.dtype)


def _project(a, wt_stack, layer, col0, n, tm, tn, name, out_dtype=F32, first_tile_scale=None):
    m, k = a.shape
    assert col0 % 8 == 0
    return pl.pallas_call(
        functools.partial(_mm_wt_kernel, first_tile_scale=first_tile_scale),
        out_shape=jax.ShapeDtypeStruct((m, n), out_dtype),
        grid=(n // tn, m // tm),
        in_specs=[pl.BlockSpec((tm, k), lambda j, i: (i, 0)),
                  pl.BlockSpec((pl.Element(1), pl.Element(tn), pl.Element(k)),
                               lambda j, i: (layer, pl.multiple_of(col0 + j * tn, 8), 0))],
        out_specs=pl.BlockSpec((tm, tn), lambda j, i: (i, j)),
        scratch_shapes=[pltpu.VMEM((k, tn), BF)],
        compiler_params=_params("arbitrary", "arbitrary"),
        name=name,
    )(a, wt_stack)


CONV_TM = 256


def _conv_kernel(b_ref, c_ref, x_ref, cp_ref, xp_ref, cn_ref, xn_ref, w_ref, o_ref):
    i = pl.program_id(0)
    tiles_per_seq = SEQ // CONV_TM
    is_ctx = i >= NL // CONV_TM
    is_start = jnp.logical_or(i % tiles_per_seq == 0, is_ctx)
    is_end = jnp.logical_or(i % tiles_per_seq == tiles_per_seq - 1, is_ctx)
    z = c_ref[...] * x_ref[...]
    zp = cp_ref[...] * xp_ref[...]
    zn = cn_ref[...] * xn_ref[...]
    prev_row = jnp.where(is_start, 0.0, zp[7:8, :])
    next_row = jnp.where(is_end, 0.0, zn[0:1, :])
    rows = lax.broadcasted_iota(jnp.int32, z.shape, 0)
    z_prev = jnp.where(rows == 0, prev_row, pltpu.roll(z, 1, 0))
    z_next = jnp.where(rows == CONV_TM - 1, next_row, pltpu.roll(z, CONV_TM - 1, 0))
    w = w_ref[...]
    y = w[0:1, :] * z_prev + w[1:2, :] * z + w[2:3, :] * z_next
    o_ref[...] = (b_ref[...] * y).astype(BF)


def _short_conv(u, conv_w, layer):
    tm = CONV_TM
    n8 = tm // 8
    last8 = NT // 8 - 1
    return pl.pallas_call(
        _conv_kernel,
        out_shape=jax.ShapeDtypeStruct((NT, CONV_DIM), BF),
        grid=(NT // tm,),
        in_specs=[pl.BlockSpec((tm, CONV_DIM), lambda i: (i, U_CONV_B // CONV_DIM)),
                  pl.BlockSpec((tm, CONV_DIM), lambda i: (i, U_CONV_C // CONV_DIM)),
                  pl.BlockSpec((tm, CONV_DIM), lambda i: (i, U_CONV_X // CONV_DIM)),
                  pl.BlockSpec((8, CONV_DIM), lambda i: (jnp.maximum(i * n8 - 1, 0), U_CONV_C // CONV_DIM)),
                  pl.BlockSpec((8, CONV_DIM), lambda i: (jnp.maximum(i * n8 - 1, 0), U_CONV_X // CONV_DIM)),
                  pl.BlockSpec((8, CONV_DIM), lambda i: (jnp.minimum(i * n8 + n8, last8), U_CONV_C // CONV_DIM)),
                  pl.BlockSpec((8, CONV_DIM), lambda i: (jnp.minimum(i * n8 + n8, last8), U_CONV_X // CONV_DIM)),
                  pl.BlockSpec((None, CONV_K, CONV_DIM), lambda i: (layer, 0, 0))],
        out_specs=pl.BlockSpec((tm, CONV_DIM), lambda i: (i, 0)),
        compiler_params=_params("parallel"),
        name="short_conv",
    )(u, u, u, u, u, u, u, conv_w)


GLA_STEPS = (CTX_LEN + SEQ) // GLA_CHUNK
GLA_CTX_STEPS = CTX_LEN // GLA_CHUNK


def _log_sigmoid(z):
    return -(jnp.maximum(-z, 0.0) + jnp.log1p(jnp.exp(-jnp.abs(z))))


def _rope(x, cs, sn):
    lane = lax.broadcasted_iota(jnp.int32, x.shape, 1)
    swapped = jnp.where(lane % 2 == 0, pltpu.roll(x, GLA_DK - 1, 1), pltpu.roll(x, 1, 1))
    return x * cs + swapped * sn


def _gla_decayed_operands(q_ref, k_ref, l_ref, r_ref, w_ref, b_ref, reverse):
    c = GLA_CHUNK
    z = _dot(l_ref[...].astype(BF), w_ref[...]) + b_ref[...]
    g = _log_sigmoid(z) * (1.0 / GLA_TAU)
    t_i = lax.broadcasted_iota(jnp.int32, (c, c), 0)
    s_i = lax.broadcasted_iota(jnp.int32, (c, c), 1)
    keep = (s_i >= t_i) if reverse else (s_i <= t_i)
    tri = jnp.where(keep, 1.0, 0.0).astype(BF)
    g1 = g.astype(BF)
    r1 = g - g1.astype(F32)
    g2 = r1.astype(BF)
    g3 = (r1 - g2.astype(F32)).astype(BF)
    b = _dot(tri, g1) + _dot(tri, g2) + _dot(tri, g3)
    total = b[0:1, :] if reverse else b[c - 1:c, :]
    cs = r_ref[:, :GLA_DK]
    sn = r_ref[:, GLA_DK:]
    heads = [slice(h * GLA_DK, (h + 1) * GLA_DK) for h in range(GLA_HEADS)]
    qs = jnp.concatenate([_rope(q_ref[:, hd] * (GLA_DK ** -0.5), cs, sn) for hd in heads], axis=1)
    kr = jnp.concatenate([_rope(k_ref[:, hd], cs, sn) for hd in heads], axis=1)
    q_dec = (qs * jnp.exp(b)).astype(BF)
    k_inv = (kr * jnp.exp(-b)).astype(BF)
    k_end = (kr * jnp.exp(total - b)).astype(BF)
    return keep, q_dec, k_inv, k_end, jnp.exp(total)


def _gla_kernel(qf_ref, kf_ref, vf_ref, lf_ref, rf_ref, qb_ref, kb_ref, vb_ref, lb_ref, rb_ref,
                wf_ref, bf_ref, wb_ref, bb_ref, of_ref, ob_ref, stf_ref, stb_ref):
    @pl.when(pl.program_id(1) == 0)
    def _():
        stf_ref[...] = jnp.zeros_like(stf_ref)
        stb_ref[...] = jnp.zeros_like(stb_ref)

    pre = [_gla_decayed_operands(qf_ref, kf_ref, lf_ref, rf_ref, wf_ref, bf_ref, False),
           _gla_decayed_operands(qb_ref, kb_ref, lb_ref, rb_ref, wb_ref, bb_ref, True)]
    chains = [(d, h) for d in range(2) for h in range(GLA_HEADS)]
    v_refs, o_refs, st_refs = (vf_ref, vb_ref), (of_ref, ob_ref), (stf_ref, stb_ref)

    def head(x, h):
        return x[:, h * GLA_DK:(h + 1) * GLA_DK]

    v = [v_refs[d][:, h * GLA_DV:(h + 1) * GLA_DV].astype(BF) for d, h in chains]
    st = [st_refs[d][h] for d, h in chains]
    att = [jnp.where(pre[d][0], _dot_nt(head(pre[d][1], h), head(pre[d][2], h)), 0.0).astype(BF)
           for d, h in chains]
    o = [_dot(att[i], v[i]) + _dot_nt(head(pre[d][1], h), st[i].astype(BF)) for i, (d, h) in enumerate(chains)]
    for d in range(2):
        o_refs[d][...] = jnp.concatenate(o[d * GLA_HEADS:(d + 1) * GLA_HEADS], axis=1)
    for i, (d, h) in enumerate(chains):
        st_refs[d][h] = st[i] * head(pre[d][4], h) + _dot_tn(v[i], head(pre[d][3], h))


def _gla(u, lr, rope_tab, wpad_f, bg_f, wpad_b, bg_b):
    c = GLA_CHUNK
    lat_chunks = SEQ // c
    ctx_base = NL // c

    def row_f(b, s):
        return jnp.where(s < GLA_CTX_STEPS, ctx_base + b * GLA_CTX_STEPS + s, b * lat_chunks + s - GLA_CTX_STEPS)

    def row_b(b, s):
        return jnp.where(s < GLA_CTX_STEPS, ctx_base + b * GLA_CTX_STEPS + GLA_CTX_STEPS - 1 - s,
                         b * lat_chunks + GLA_STEPS - 1 - s)

    def rope_f(s):
        return jnp.where(s < GLA_CTX_STEPS, lat_chunks, s - GLA_CTX_STEPS)

    def rope_b(s):
        return jnp.where(s < GLA_CTX_STEPS, lat_chunks, GLA_STEPS - 1 - s)

    def stream(row, rope):
        return [pl.BlockSpec((c, GLA_QK), lambda b, s: (row(b, s), U_GLA_Q // GLA_QK)),
                pl.BlockSpec((c, GLA_QK), lambda b, s: (row(b, s), U_GLA_K // GLA_QK)),
                pl.BlockSpec((c, GLA_V), lambda b, s: (row(b, s), U_GLA_V // GLA_V)),
                pl.BlockSpec((c, LR_COLS), lambda b, s: (row(b, s), 0)),
                pl.BlockSpec((c, 2 * GLA_DK), lambda b, s: (rope(s), 0))]

    head_w = pl.BlockSpec((LR_COLS, GLA_QK), lambda b, s: (0, 0))
    head_b = pl.BlockSpec((1, GLA_QK), lambda b, s: (0, 0))
    state = pltpu.VMEM((GLA_HEADS, GLA_DV, GLA_DK), F32)
    return pl.pallas_call(
        _gla_kernel,
        out_shape=(jax.ShapeDtypeStruct((NT, GLA_V), F32), jax.ShapeDtypeStruct((NT, GLA_V), F32)),
        grid=(BATCH, GLA_STEPS),
        in_specs=stream(row_f, rope_f) + stream(row_b, rope_b) + [head_w, head_b, head_w, head_b],
        out_specs=(pl.BlockSpec((c, GLA_V), lambda b, s: (row_f(b, s), 0)),
                   pl.BlockSpec((c, GLA_V), lambda b, s: (row_b(b, s), 0))),
        scratch_shapes=[state, state],
        compiler_params=_params("parallel", "arbitrary"),
        name="gla",
    )(u, u, u, lr, rope_tab, u, u, u, lr, rope_tab, wpad_f, bg_f, wpad_b, bg_b)


def _rope_table():
    t = jnp.arange(SEQ)
    rows = (t // GRID_W).astype(F32)
    cols = (t % GRID_W).astype(F32)
    n_freq = GLA_DK // 4
    inv = 1.0 / (ROPE_BASE ** (jnp.arange(n_freq, dtype=F32) / n_freq))
    ang = jnp.concatenate([rows[:, None] * inv, cols[:, None] * inv], -1)
    cos = jnp.repeat(jnp.cos(ang), 2, axis=-1)
    sin = jnp.repeat(jnp.sin(ang), 2, axis=-1)
    sign = jnp.tile(jnp.array([-1.0, 1.0], F32), GLA_DK // 2)
    cos = jnp.concatenate([cos, jnp.ones((GRID_W, GLA_DK), F32)], 0)
    sin = jnp.concatenate([sin * sign, jnp.zeros((GRID_W, GLA_DK), F32)], 0)
    return jnp.concatenate([cos, sin], axis=1)


def _decay_weights(wg, bg, lane_offset):
    w = jnp.pad(wg, ((lane_offset, LR_COLS - GLA_RANK - lane_offset), (0, 0))).astype(BF)
    return w, bg.reshape(1, GLA_QK)


NA_HPS = 4


def _na_softmax_out(parts):
    m = functools.reduce(jnp.maximum, [jnp.max(s, axis=-1, keepdims=True) for s, _ in parts])
    ps = [jnp.exp(s - m) for s, _ in parts]
    l = functools.reduce(jnp.add, [jnp.sum(p, axis=-1, keepdims=True) for p in ps])
    o = functools.reduce(jnp.add, [_dot(p.astype(BF), v) for p, (_, v) in zip(ps, parts)])
    return (o / l).astype(BF)


def _na_kernel(q_ref, k_ref, v_ref, kc_ref, vc_ref, bias_ref, o_ref):
    j = pl.program_id(2)
    head_cols = [slice(h * NA_DH, (h + 1) * NA_DH) for h in range(NA_HPS)]

    @pl.when(j < NA_NBLK)
    def _():
        base = jnp.clip(NA_QROWS * j - WIN_R // 2, 0, GRID_ROWS - NA_KROWS)
        start = pl.multiple_of(base * GRID_W, GRID_W)

        def pair_entry(qr, kp):
            r = NA_QROWS * j + qr
            rs = jnp.clip(r - WIN_R // 2, 0, GRID_ROWS - WIN_R)
            a0 = base + 2 * kp
            dr0 = a0 - r + WIN_R - 1
            in0 = jnp.logical_and(a0 >= rs, a0 < rs + WIN_R)
            in1 = jnp.logical_and(a0 + 1 >= rs, a0 + 1 < rs + WIN_R)
            entry = jnp.where(jnp.logical_and(in0, in1), NA_PAIR_BOTH + dr0,
                              jnp.where(in0, NA_PAIR_LEFT + dr0,
                                        jnp.where(in1, NA_PAIR_RIGHT + dr0 + 1, NA_PAIR_NONE)))
            return jnp.clip(entry, 0, NA_PAIR_ENTRIES - 1)

        entries = [[pair_entry(qr, kp) for kp in range(NA_KPAIRS)] for qr in range(NA_QROWS)]
        for h, hc in enumerate(head_cols):
            q = q_ref[:, hc]
            vl = v_ref[pl.ds(start, NA_KBLK), hc]
            bias = jnp.concatenate(
                [jnp.concatenate([bias_ref[h, entries[qr][kp]] for kp in range(NA_KPAIRS)], axis=1)
                 for qr in range(NA_QROWS)], axis=0)
            s_loc = _dot_nt(q, k_ref[pl.ds(start, NA_KBLK), hc]) + bias
            s_ctx = _dot_nt(q, kc_ref[:, hc])
            o_ref[:, hc] = _na_softmax_out([(s_loc, vl), (s_ctx, vc_ref[:, hc])])

    @pl.when(j == NA_NBLK)
    def _():
        for hc in head_cols:
            s_ctx = _dot_nt(q_ref[:, hc], kc_ref[:, hc])
            o_ref[:, hc] = _na_softmax_out([(s_ctx, vc_ref[:, hc])])


def _na_bias_pair_tables(rpb_all):
    qc = np.arange(GRID_W)
    cs = np.clip(qc - WIN_C // 2, 0, GRID_W - WIN_C)
    kc = np.arange(GRID_W)
    col_ok = (kc[None, :] >= cs[:, None]) & (kc[None, :] < cs[:, None] + WIN_C)
    dc = kc[None, :] - qc[:, None] + WIN_C - 1
    col_sel = ((dc[None] == np.arange(2 * WIN_C - 1)[:, None, None]) & col_ok[None]).astype(np.float32)
    t = jnp.einsum('lhrd,dxy->lhrxy', rpb_all, jnp.asarray(col_sel), precision=lax.Precision.HIGHEST)
    t = jnp.where(jnp.asarray(col_ok), t, NEG_INF)
    masked = jnp.full_like(t, NEG_INF)
    both = jnp.concatenate([t[:, :, :-1], t[:, :, 1:]], axis=-1)
    left = jnp.concatenate([t, masked], axis=-1)
    right = jnp.concatenate([masked, t], axis=-1)
    none = jnp.concatenate([masked[:, :, :1], masked[:, :, :1]], axis=-1)
    return jnp.concatenate([both, left, right, none], axis=2)


def _neighbourhood_attention(un, bias_pairs, layer, with_ctx):
    steps = NA_NBLK + (1 if with_ctx else 0)
    ctx_q = NL // NA_QBLK
    w = NA_HPS * NA_DH

    def q_row(b, j):
        return jnp.where(j < NA_NBLK, b * NA_NBLK + j, ctx_q + b)

    ctx_blk = NL // CTX_LEN
    return pl.pallas_call(
        _na_kernel,
        out_shape=jax.ShapeDtypeStruct((NT, NA_W), BF),
        grid=(BATCH, NA_HEADS // NA_HPS, steps),
        in_specs=[pl.BlockSpec((NA_QBLK, w), lambda b, h, j: (q_row(b, j), UN_Q // w + h)),
                  pl.BlockSpec((SEQ, w), lambda b, h, j: (b, UN_K // w + h)),
                  pl.BlockSpec((SEQ, w), lambda b, h, j: (b, UN_V // w + h)),
                  pl.BlockSpec((CTX_LEN, w), lambda b, h, j: (ctx_blk + b, UN_K // w + h)),
                  pl.BlockSpec((CTX_LEN, w), lambda b, h, j: (ctx_blk + b, UN_V // w + h)),
                  pl.BlockSpec((None, NA_HPS, NA_PAIR_ENTRIES, GRID_W, 2 * GRID_W),
                               lambda b, h, j: (layer, h, 0, 0, 0))],
        out_specs=pl.BlockSpec((NA_QBLK, w), lambda b, h, j: (q_row(b, j), h)),
        compiler_params=_params("parallel", "parallel", "arbitrary"),
        name="neighbourhood_attention",
    )(un, un, un, un, un, bias_pairs)


MERGE_TM = 256


def _merge_kernel(yc_ref, of_ref, ob_ref, r_ref, yn_ref, gt_ref, xl_ref, xc_ref,
                  ng_ref, gb_ref, wc_ref, wg_ref, wn_ref, wo_ref,
                  g1_ref, lg_ref, lb_ref, sh2_ref, sc2_ref, xo_ref, ho_ref):
    x = _select_rows(pl.program_id(0), NL // MERGE_TM, xl_ref, xc_ref)
    o = of_ref[...] + ob_ref[...]
    r = r_ref[...]
    ng = ng_ref[...]
    heads = []
    for h in range(GLA_HEADS):
        oh = o[:, h * GLA_DV:(h + 1) * GLA_DV]
        oh = oh * lax.rsqrt(jnp.mean(oh * oh, axis=-1, keepdims=True) + LN_EPS)
        heads.append(oh * ng * _silu(r[:, h * GLA_DV:(h + 1) * GLA_DV]))
    y_gla = jnp.concatenate(heads, axis=-1).astype(BF)
    g = jax.nn.sigmoid(gt_ref[...] + gb_ref[...])
    z = (g[:, :D_MODEL] * _dot(yc_ref[...], wc_ref[...])
         + g[:, D_MODEL:2 * D_MODEL] * _dot(y_gla, wg_ref[...])
         + g[:, 2 * D_MODEL:] * _dot(yn_ref[...], wn_ref[...]))
    y = _dot(z.astype(BF), wo_ref[...])
    xn = _layer_norm(DEEPNORM_ALPHA * x + g1_ref[...] * y, lg_ref[...], lb_ref[...])
    xo_ref[...] = xn
    ho_ref[...] = (xn * (1.0 + sc2_ref[...]) + sh2_ref[...]).astype(BF)


def _merge(y_conv, o_f, o_b, u, ug, y_na, x_lat, x_ctx, mod, layer, rows, norm_g, gate_b, w_conv, w_gla, w_na,
           w_out, ln_g, ln_b):
    tm = MERGE_TM
    rot = _mod_row_of_tile(tm)
    row = lambda width: pl.BlockSpec((tm, width), lambda i: (i, 0))
    const = lambda shape: pl.BlockSpec(shape, lambda i: (0,) * len(shape), pipeline_mode=pl.Buffered(1))
    return pl.pallas_call(
        _merge_kernel,
        out_shape=(jax.ShapeDtypeStruct((rows, D_MODEL), F32), jax.ShapeDtypeStruct((rows, D_MODEL), BF)),
        grid=(rows // tm,),
        in_specs=[row(CONV_DIM), row(GLA_V), row(GLA_V),
                  pl.BlockSpec((tm, GLA_V), lambda i: (i, U_GLA_R // GLA_V)),
                  row(NA_W), row(N_BRANCH * D_MODEL), *_split_row_specs(tm),
                  const((1, GLA_DV)), const((1, N_BRANCH * D_MODEL)),
                  const((CONV_DIM, D_MODEL)), const((GLA_V, D_MODEL)), const((NA_W, D_MODEL)),
                  const((D_MODEL, D_MODEL)),
                  _mod_spec(layer, 2, rot), const((1, D_MODEL)), const((1, D_MODEL)),
                  _mod_spec(layer, 3, rot), _mod_spec(layer, 4, rot)],
        out_specs=(row(D_MODEL), row(D_MODEL)),
        compiler_params=_params("parallel"),
        name="merge",
    )(y_conv, o_f, o_b, u, y_na, ug, x_lat, x_ctx, norm_g, gate_b, w_conv, w_gla, w_na, w_out,
      mod, ln_g, ln_b, mod, mod)


FFN_TM = 512
FFN_TF = 512


def _ffn_kernel(h_ref, wg_ref, wu_ref, wd_ref, x_ref, g2_ref, lg_ref, lb_ref, shn_ref, scn_ref,
                xo_ref, ho_ref, acc_ref):
    j = pl.program_id(1)

    @pl.when(j == 0)
    def _():
        acc_ref[...] = jnp.zeros_like(acc_ref)

    h = h_ref[...]
    a = _silu(_dot(h, wg_ref[...])) * _dot(h, wu_ref[...])
    acc_ref[...] += _dot(a.astype(BF), wd_ref[...])

    @pl.when(j == pl.num_programs(1) - 1)
    def _():
        xn = _layer_norm(DEEPNORM_ALPHA * x_ref[...] + g2_ref[...] * acc_ref[...], lg_ref[...], lb_ref[...])
        xo_ref[...] = xn
        ho_ref[...] = (xn * (1.0 + scn_ref[...]) + shn_ref[...]).astype(BF)


def _dense_ffn(h2, x_all, mod, layer, wg, wu, wd, ln_g, ln_b):
    tm, tf = FFN_TM, FFN_TF
    rot = _mod_row_of_tile(tm)
    row = lambda: pl.BlockSpec((tm, D_MODEL), lambda i, j: (i, 0))
    const = lambda: pl.BlockSpec((1, D_MODEL), lambda i, j: (0, 0))
    return pl.pallas_call(
        _ffn_kernel,
        out_shape=(jax.ShapeDtypeStruct((NT, D_MODEL), F32), jax.ShapeDtypeStruct((NT, D_MODEL), BF)),
        grid=(NT // tm, D_FF // tf),
        in_specs=[row(),
                  pl.BlockSpec((D_MODEL, tf), lambda i, j: (0, j)),
                  pl.BlockSpec((D_MODEL, tf), lambda i, j: (0, j)),
                  pl.BlockSpec((tf, D_MODEL), lambda i, j: (j, 0)),
                  row(), _mod_spec(layer, 5, rot), const(), const(),
                  _mod_spec(layer + 1, 0, rot), _mod_spec(layer + 1, 1, rot)],
        out_specs=(row(), row()),
        scratch_shapes=[pltpu.VMEM((tm, D_MODEL), F32)],
        compiler_params=_params("parallel", "arbitrary"),
        name="dense_ffn",
    )(h2, wg, wu, wd, x_all, mod, ln_g, ln_b, mod, mod)


def _router_kernel(h_ref, w_ref, idx_ref, wt_ref):
    logits = _dot(h_ref[...], w_ref[...])
    lane = lax.broadcasted_iota(jnp.int32, logits.shape, 1)
    logits = jnp.where(lane < N_EXPERTS, logits, -jnp.inf)
    m1 = jnp.max(logits, axis=-1, keepdims=True)
    i1 = jnp.min(jnp.where(logits == m1, lane, ROUTER_LANES), axis=-1, keepdims=True)
    rest = jnp.where(lane == i1, -jnp.inf, logits)
    m2 = jnp.max(rest, axis=-1, keepdims=True)
    i2 = jnp.min(jnp.where(rest == m2, lane, ROUTER_LANES), axis=-1, keepdims=True)
    e2 = jnp.exp(m2 - m1)
    w1 = 1.0 / (1.0 + e2)
    w2 = e2 / (1.0 + e2)
    idx_ref[...] = jnp.where(lane == 0, i1, jnp.where(lane == 1, i2, 0))
    wt_ref[...] = jnp.where(lane == 0, w1, jnp.where(lane == 1, w2, 0.0))


def _router(h2, router_w):
    tm = 512
    return pl.pallas_call(
        _router_kernel,
        out_shape=(jax.ShapeDtypeStruct((NL, ROUTER_LANES), jnp.int32),
                   jax.ShapeDtypeStruct((NL, ROUTER_LANES), F32)),
        grid=(NL // tm,),
        in_specs=[pl.BlockSpec((tm, D_MODEL), lambda i: (i, 0)),
                  pl.BlockSpec((D_MODEL, ROUTER_LANES), lambda i: (0, 0))],
        out_specs=(pl.BlockSpec((tm, ROUTER_LANES), lambda i: (i, 0)),
                   pl.BlockSpec((tm, ROUTER_LANES), lambda i: (i, 0))),
        compiler_params=_params("parallel"),
        name="moe_router",
    )(h2, router_w)


def _dispatch_plan(idx):
    i32 = jnp.int32
    experts = jnp.arange(N_EXPERTS, dtype=i32)
    oh_a = (idx[:, 0:1] == experts).astype(i32)
    oh_b = (idx[:, 1:2] == experts).astype(i32)
    cum = jnp.cumsum(oh_a + oh_b, axis=0)
    counts = cum[-1]
    n_st = (counts + MOE_SM - 1) // MOE_SM
    st_end = jnp.cumsum(n_st)
    st_start = st_end - n_st
    g_start = st_start * MOE_SM
    g_end = st_end * MOE_SM
    slot_table = g_start[None, :] + cum - 1
    slot_a = jnp.sum(oh_a * slot_table, axis=1)
    slot_b = jnp.sum(oh_b * slot_table, axis=1)

    n_used = st_end[-1]
    st_ids = jnp.minimum(jnp.arange(MOE_NST, dtype=i32), n_used - 1)
    st_expert = jnp.minimum(jnp.sum((st_ids[:, None] >= st_end[None, :]).astype(i32), axis=1), N_EXPERTS - 1)
    st_left = counts[st_expert] - (st_ids - st_start[st_expert]) * MOE_SM
    st_rows = ((jnp.clip(st_left, 0, MOE_SM) + MOE_GB - 1) // MOE_GB) * MOE_GB
    tile_plan = (st_expert, st_rows, n_used.reshape(1))

    pair_ids = jnp.arange(MOE_NPAIR, dtype=i32)
    tile_last = jnp.arange(MOE_NTT, dtype=i32) * MOE_GB + MOE_GB - 1
    hi = cum[tile_last]
    lo = jnp.concatenate([jnp.zeros((1, N_EXPERTS), i32), hi[:-1]], axis=0)

    blk_ids = jnp.arange(MOE_NGB, dtype=i32)
    blk_start = blk_ids * MOE_GB
    blk_e = jnp.minimum(jnp.sum((blk_start[:, None] >= g_end[None, :]).astype(i32), axis=1), N_EXPERTS - 1)
    r0 = blk_start - g_start[blk_e]
    r1 = jnp.minimum(r0 + MOE_GB, counts[blk_e])
    has = (blk_start < g_end[-1]) & (r1 > r0)
    hi_e = hi[:, blk_e]
    t_lo = jnp.sum((hi_e <= r0[None, :]).astype(i32), axis=0)
    t_hi = jnp.sum((hi_e < r1[None, :]).astype(i32), axis=0)
    t_lo = jnp.where(has, t_lo, 0)
    t_end = jnp.where(has, t_hi + 1, 0)
    blk_map = lax.cummax(jnp.where(has, blk_ids, 0), axis=0)
    dispatch = (t_lo, t_end, blk_map, has.astype(i32))

    b_lo = (g_start[None, :] + lo) // MOE_GB
    b_hi = (g_start[None, :] + hi - 1) // MOE_GB
    n_c = jnp.where(hi > lo, b_hi - b_lo + 1, 0).reshape(-1)
    b_lo = b_lo.reshape(-1)
    c_end = jnp.cumsum(n_c)
    c_total = c_end[-1]
    c_idx = jnp.minimum(jnp.sum((pair_ids[:, None] >= c_end[None, :]).astype(i32), axis=1),
                        MOE_NTT * N_EXPERTS - 1)
    c_blk = jnp.clip(b_lo[c_idx] + pair_ids - (c_end[c_idx] - n_c[c_idx]), 0, MOE_NGB - 1)
    c_tile = c_idx // N_EXPERTS
    c_valid = pair_ids < c_total
    c_tile = jnp.where(c_valid, c_tile, MOE_NTT - 1)
    c_blk = jnp.where(c_valid, c_blk, c_blk[jnp.maximum(c_total - 1, 0)])
    c_prev = jnp.concatenate([jnp.full((1,), -1, i32), c_tile[:-1]])
    c_next = jnp.concatenate([c_tile[1:], jnp.full((1,), -1, i32)])
    c_first = c_valid & (c_tile != c_prev)
    c_last = c_valid & ((c_tile != c_next) | (pair_ids == c_total - 1))
    combine = (c_tile, c_blk, c_first.astype(i32), c_last.astype(i32), c_valid.astype(i32))
    return slot_a, slot_b, tile_plan, dispatch, combine


def _dispatch_kernel(lo_ref, end_ref, map_ref, has_ref, h_ref, sa_ref, sb_ref, wa_ref, wb_ref, hs_ref, ws_ref,
                     acc_ref, wacc_ref):
    blk = pl.program_id(0)

    @pl.when(has_ref[blk] == 1)
    def _():
        acc_ref[...] = jnp.zeros_like(acc_ref)
        wacc_ref[...] = jnp.zeros_like(wacc_ref)
        slots = blk * MOE_GB + lax.broadcasted_iota(jnp.int32, (MOE_GB, MOE_GB), 0)

        def scan_tile(t, carry):
            rows = pl.ds(pl.multiple_of(t * MOE_GB, MOE_GB), MOE_GB)
            hit_a = slots == sa_ref[t]
            hit_b = slots == sb_ref[t]
            onehot = jnp.where(hit_a, 1.0, jnp.where(hit_b, 1.0, 0.0)).astype(BF)
            acc_ref[...] += _dot(onehot, h_ref[rows, :])
            w = jnp.where(hit_a, wa_ref[t], jnp.where(hit_b, wb_ref[t], 0.0))
            wacc_ref[...] += jnp.broadcast_to(jnp.sum(w, axis=1, keepdims=True), wacc_ref.shape)
            return carry

        lax.fori_loop(lo_ref[blk], end_ref[blk], scan_tile, 0)
        hs_ref[...] = acc_ref[...].astype(BF)
        ws_ref[...] = wacc_ref[...]


def _dispatch(h2, slot_a, slot_b, w_a, w_b, plan):
    used = lambda i, lo, end, blk_map, has: (blk_map[i], 0)
    resident = lambda shape: pl.BlockSpec(shape, lambda i, *_: (0,) * len(shape), pipeline_mode=pl.Buffered(1))
    shape3 = (MOE_NTT, 1, MOE_GB)
    grid_spec = pltpu.PrefetchScalarGridSpec(
        num_scalar_prefetch=4,
        grid=(MOE_NGB,),
        in_specs=[resident((NL, D_MODEL)), resident(shape3), resident(shape3), resident(shape3),
                  resident(shape3)],
        out_specs=(pl.BlockSpec((MOE_GB, D_MODEL), used), pl.BlockSpec((MOE_GB, ROUTER_LANES), used)),
        scratch_shapes=[pltpu.VMEM((MOE_GB, D_MODEL), F32), pltpu.VMEM((MOE_GB, ROUTER_LANES), F32)])
    return pl.pallas_call(
        _dispatch_kernel,
        out_shape=(jax.ShapeDtypeStruct((MOE_NSLOT, D_MODEL), BF),
                   jax.ShapeDtypeStruct((MOE_NSLOT, ROUTER_LANES), F32)),
        grid_spec=grid_spec,
        compiler_params=_params("arbitrary"),
        name="moe_dispatch",
    )(*plan, h2, slot_a.reshape(shape3), slot_b.reshape(shape3), w_a.reshape(shape3), w_b.reshape(shape3))


def _expert_kernel(e_ref, rows_ref, n_ref, hs_ref, wg_ref, wu_ref, wd_ref, ws_ref, y_ref,
                   acc_ref, wgb_ref, wub_ref, wdb_ref):
    s = pl.program_id(0)
    j = pl.program_id(1)
    last = pl.num_programs(1) - 1

    @pl.when(s < n_ref[0])
    def _():
        wgb_ref[...] = wg_ref[...].astype(BF)
        wub_ref[...] = wu_ref[...].astype(BF)
        wdb_ref[...] = wd_ref[...].astype(BF)
        n_rows = rows_ref[s]

        def for_row_chunks(fn):
            n_full = n_rows // MOE_CH

            def full_chunk(c, carry):
                fn(pl.ds(pl.multiple_of(c * MOE_CH, MOE_CH), MOE_CH))
                return carry

            lax.fori_loop(0, n_full, full_chunk, 0)
            rest = n_rows - n_full * MOE_CH
            half = MOE_CH // 2

            @pl.when(rest >= half)
            def _():
                fn(pl.ds(pl.multiple_of(n_full * MOE_CH, MOE_GB), half))

            @pl.when(rest % half != 0)
            def _():
                fn(pl.ds(pl.multiple_of(n_rows - MOE_GB, MOE_GB), MOE_GB))

        @pl.when(j == 0)
        def _():
            acc_ref[...] = jnp.zeros_like(acc_ref)

        def accumulate(rows):
            h = hs_ref[rows, :]
            a = (_silu(_dot(h, wgb_ref[...])) * _dot(h, wub_ref[...])).astype(BF)
            acc_ref[rows, :] += _dot(a, wdb_ref[...])

        for_row_chunks(accumulate)

        @pl.when(j == last)
        def _():
            def emit(rows):
                y_ref[rows, :] = (acc_ref[rows, :] * ws_ref[rows, 0:1]).astype(BF)

            for_row_chunks(emit)


def _experts(hs, tile_plan, wg, wu, wd, ws, moe_layer):
    n_ff = D_FF // MOE_TF

    def used(s, n):
        return jnp.minimum(s, n[0] - 1)

    def ff_tile(s, j, n):
        return jnp.where(s < n[0], j, n_ff - 1)

    rows_spec = lambda width, **kw: pl.BlockSpec((MOE_SM, width), lambda s, j, e, r, n: (used(s, n), 0), **kw)
    up_spec = pl.BlockSpec((None, None, D_MODEL, MOE_TF),
                           lambda s, j, e, r, n: (moe_layer, e[s], 0, ff_tile(s, j, n)))
    grid_spec = pltpu.PrefetchScalarGridSpec(
        num_scalar_prefetch=3,
        grid=(MOE_NST, n_ff),
        in_specs=[rows_spec(D_MODEL, pipeline_mode=pl.Buffered(1)), up_spec, up_spec,
                  pl.BlockSpec((None, None, MOE_TF, D_MODEL),
                               lambda s, j, e, r, n: (moe_layer, e[s], ff_tile(s, j, n), 0)),
                  rows_spec(ROUTER_LANES)],
        out_specs=rows_spec(D_MODEL, pipeline_mode=pl.Buffered(1)),
        scratch_shapes=[pltpu.VMEM((MOE_SM, D_MODEL), F32), pltpu.VMEM((D_MODEL, MOE_TF), BF),
                        pltpu.VMEM((D_MODEL, MOE_TF), BF), pltpu.VMEM((MOE_TF, D_MODEL), BF)])
    return pl.pallas_call(
        _expert_kernel,
        out_shape=jax.ShapeDtypeStruct((MOE_NSLOT, D_MODEL), BF),
        grid_spec=grid_spec,
        compiler_params=_params("arbitrary", "arbitrary"),
        name="moe_experts",
    )(*tile_plan, hs, wg, wu, wd, ws)


def _combine_kernel(tile_ref, blk_ref, first_ref, last_ref, valid_ref,
                    y_ref, sa_ref, sb_ref, x_ref, g2_ref, lg_ref, lb_ref, o_ref, acc_ref):
    p = pl.program_id(0)

    @pl.when(first_ref[p] == 1)
    def _():
        acc_ref[...] = jnp.zeros_like(acc_ref)

    @pl.when(valid_ref[p] == 1)
    def _():
        slots = blk_ref[p] * MOE_GB + lax.broadcasted_iota(jnp.int32, (MOE_GB, MOE_GB), 1)
        onehot = jnp.where(slots == sa_ref[...], 1.0, jnp.where(slots == sb_ref[...], 1.0, 0.0)).astype(BF)
        acc_ref[...] += _dot(onehot, y_ref[...])

    @pl.when(last_ref[p] == 1)
    def _():
        o_ref[...] = _layer_norm(DEEPNORM_ALPHA * x_ref[...] + g2_ref[...] * acc_ref[...],
                                 lg_ref[...], lb_ref[...])


def _combine(y, slot_a, slot_b, x_lat, mod, layer, ln_g, ln_b, plan):
    per_batch = SEQ // MOE_GB
    slot_blk = lambda p, tile, blk, *_: (blk[p], 0)
    tok_blk = lambda p, tile, *_: (tile[p], 0)
    grid_spec = pltpu.PrefetchScalarGridSpec(
        num_scalar_prefetch=5,
        grid=(MOE_NPAIR,),
        in_specs=[pl.BlockSpec((MOE_GB, D_MODEL), slot_blk),
                  pl.BlockSpec((MOE_GB, 1), tok_blk), pl.BlockSpec((MOE_GB, 1), tok_blk),
                  pl.BlockSpec((MOE_GB, D_MODEL), tok_blk),
                  pl.BlockSpec((None, 1, D_MODEL),
                               lambda p, tile, *_: (layer * 48 + (tile[p] // per_batch) * 6 + 5, 0, 0)),
                  pl.BlockSpec((1, D_MODEL), lambda p, *_: (0, 0)),
                  pl.BlockSpec((1, D_MODEL), lambda p, *_: (0, 0))],
        out_specs=pl.BlockSpec((MOE_GB, D_MODEL), tok_blk),
        scratch_shapes=[pltpu.VMEM((MOE_GB, D_MODEL), F32)])
    return pl.pallas_call(
        _combine_kernel,
        out_shape=jax.ShapeDtypeStruct((NL, D_MODEL), F32),
        grid_spec=grid_spec,
        compiler_params=_params("arbitrary"),
        name="moe_combine",
    )(*plan, y, slot_a.reshape(NL, 1), slot_b.reshape(NL, 1), x_lat, mod, ln_g, ln_b)


def _moe_ffn(h2, x_lat, mod, layer, router_w, wg, wu, wd, moe_layer, ln_g, ln_b):
    router_pad = jnp.pad(router_w, ((0, 0), (0, ROUTER_LANES - N_EXPERTS))).astype(BF)
    idx, wts = _router(h2, router_pad)
    slot_a, slot_b, tile_plan, d_plan, c_plan = _dispatch_plan(idx[:, :TOP_K])
    hs, ws = _dispatch(h2, slot_a, slot_b, wts[:, 0], wts[:, 1], d_plan)
    y = _experts(hs, tile_plan, wg, wu, wd, ws, moe_layer)
    return _combine(y, slot_a, slot_b, x_lat, mod, layer, ln_g, ln_b, c_plan)


def kernel(x, c, ctx, c_ctx, ada_w, ada_b, w_in, conv_w, gla_wg_f, gla_bg_f, gla_wg_b, gla_bg_b, gla_norm_g, na_rpb, w_br_conv, w_br_gla, w_br_na, gate_b, w_out, ln1_g, ln1_b, ln2_g, ln2_b, ffn_w_gate, ffn_w_up, ffn_w_down, moe_router, moe_w_gate, moe_w_up, moe_w_down):
    assert DEPTH == 2 and x.shape == (BATCH, SEQ, D_MODEL) and ctx.shape == (BATCH, CTX_LEN, D_MODEL)
    cvec = jnp.concatenate([c, c_ctx[None, :], jnp.zeros((8 - BATCH - 1, D_MODEL), F32)], axis=0)
    mod = _ada_table(cvec, ada_w, ada_b)
    rope_tab = _rope_table()
    x_lat, x_ctx = x.reshape(NL, D_MODEL), ctx.reshape(NC, D_MODEL)
    h1 = _modulate(x_lat, x_ctx, mod, 0)
    w_in_t = jnp.swapaxes(w_in, 1, 2)
    na_bias = _na_bias_pair_tables(na_rpb)

    for layer in range(DEPTH):
        last = layer == DEPTH - 1
        u = _project(h1, w_in_t, layer, 0, U_COLS, 1088, 1024, "proj_in")
        lr = _project(h1, w_in_t, layer, W_IN_LR, LR_COLS, 1088, LR_COLS, "proj_decay")
        un = _project(h1, w_in_t, layer, W_IN_NA, UN_COLS, 1088, NA_W, "proj_na", out_dtype=BF,
                      first_tile_scale=NA_DH ** -0.5)
        ug = _project(h1, w_in_t, layer, W_IN_GATES, N_BRANCH * D_MODEL, 1088, 1024, "proj_gates")

        y_conv = _short_conv(u, conv_w, layer)
        wpad_f, bg_f = _decay_weights(gla_wg_f[layer], gla_bg_f[layer], 0)
        wpad_b, bg_b = _decay_weights(gla_wg_b[layer], gla_bg_b[layer], GLA_RANK)
        o_f, o_b = _gla(u, lr, rope_tab, wpad_f, bg_f, wpad_b, bg_b)
        y_na = _neighbourhood_attention(un, na_bias, layer, with_ctx=not last)

        rows = NL if last else NT
        x_mid, h2 = _merge(y_conv, o_f, o_b, u, ug, y_na, x_lat, x_ctx, mod, layer, rows,
                           gla_norm_g[layer].reshape(1, GLA_DV), gate_b[layer].reshape(1, N_BRANCH * D_MODEL),
                           w_br_conv[layer].astype(BF), w_br_gla[layer].astype(BF), w_br_na[layer].astype(BF),
                           w_out[layer].astype(BF), ln1_g[layer].reshape(1, D_MODEL),
                           ln1_b[layer].reshape(1, D_MODEL))
        ln_g = ln2_g[layer].reshape(1, D_MODEL)
        ln_b = ln2_b[layer].reshape(1, D_MODEL)
        if layer % 2 == 0:
            j = layer // 2
            x_lat, h1 = _dense_ffn(h2, x_mid, mod, layer, ffn_w_gate[j].astype(BF), ffn_w_up[j].astype(BF),
                                   ffn_w_down[j].astype(BF), ln_g, ln_b)
            x_ctx = x_lat
        else:
            x_lat = _moe_ffn(h2, x_mid, mod, layer, moe_router[layer // 2], moe_w_gate, moe_w_up, moe_w_down,
                             layer // 2, ln_g, ln_b)
    return x_lat.reshape(BATCH, SEQ, D_MODEL)
```

```python
import functools

import numpy as np
import jax
import jax.numpy as jnp
from jax import lax
from jax.experimental import pallas as pl
from jax.experimental.pallas import tpu as pltpu

D_MODEL = 2048
BATCH = 2
SEQ = 4096
DEPTH = 2
GRID_W = 64
CTX_LEN = 256
CONV_DIM = 1024
CONV_K = 3
GLA_HEADS = 4
GLA_DK = 128
GLA_DV = 256
GLA_RANK = 16
GLA_TAU = 16.0
GLA_CHUNK = 64
GLA_QK = GLA_HEADS * GLA_DK
GLA_V = GLA_HEADS * GLA_DV
NA_HEADS = 8
NA_DH = 128
NA_W = NA_HEADS * NA_DH
WIN_R = 8
WIN_C = 16
D_FF = 5632
N_EXPERTS = 8
TOP_K = 2
ROPE_BASE = 10000.0
LN_EPS = 1e-5
N_BRANCH = 3
DEEPNORM_ALPHA = (2 * DEPTH) ** 0.25

NL = BATCH * SEQ
NC = BATCH * CTX_LEN
NT = NL + NC
GRID_ROWS = SEQ // GRID_W

BF = jnp.bfloat16
F32 = jnp.float32

V7X_VMEM_BYTES = 64 * 1024 * 1024
VMEM_LIMIT = V7X_VMEM_BYTES - 4 * 1024 * 1024
NEG_INF = -1e30

U_CONV_B, U_CONV_C, U_CONV_X = 0, 1024, 2048
U_GLA_Q, U_GLA_K, U_GLA_V, U_GLA_R = 3072, 3584, 4096, 5120
U_COLS = 6144
UN_Q, UN_K, UN_V = 0, 1024, 2048
UN_COLS = 3072
W_IN_LR = U_COLS
W_IN_NA = W_IN_LR + 2 * GLA_RANK
W_IN_GATES = W_IN_NA + UN_COLS
LR_COLS = 128

NA_QROWS = 4
NA_KROWS = NA_QROWS + WIN_R
NA_QBLK = NA_QROWS * GRID_W
NA_KBLK = NA_KROWS * GRID_W
NA_KPAIRS = NA_KROWS // 2
NA_BIAS_DR = 2 * WIN_R - 1
NA_PAIR_BOTH = 0
NA_PAIR_LEFT = NA_PAIR_BOTH + NA_BIAS_DR - 1
NA_PAIR_RIGHT = NA_PAIR_LEFT + NA_BIAS_DR
NA_PAIR_NONE = NA_PAIR_RIGHT + NA_BIAS_DR
NA_PAIR_ENTRIES = NA_PAIR_NONE + 1
NA_NBLK = GRID_ROWS // NA_QROWS

MOE_GB = 256
MOE_SM = 9 * MOE_GB
MOE_CH = 4 * MOE_GB
MOE_TF = 256
MOE_NST = (NL * TOP_K) // MOE_SM + N_EXPERTS
MOE_NSLOT = MOE_NST * MOE_SM
MOE_NGB = MOE_NSLOT // MOE_GB
MOE_NTT = NL // MOE_GB
MOE_NPAIR = (NL * TOP_K) // MOE_GB + N_EXPERTS + N_EXPERTS * (MOE_NTT - 1)
ROUTER_LANES = 128


def _params(*sem):
    return pltpu.CompilerParams(dimension_semantics=sem, vmem_limit_bytes=VMEM_LIMIT)


def _dot(a, b):
    return jnp.dot(a, b, preferred_element_type=F32)


def _dot_nt(a, b):
    return lax.dot_general(a, b, (((1,), (1,)), ((), ())), preferred_element_type=F32)


def _dot_tn(a, b):
    return lax.dot_general(a, b, (((0,), (0,)), ((), ())), preferred_element_type=F32)


def _silu(x):
    return x * jax.nn.sigmoid(x)


def _layer_norm(x, g, b):
    mu = jnp.mean(x, axis=-1, keepdims=True)
    xc = x - mu
    var = jnp.mean(xc * xc, axis=-1, keepdims=True)
    return xc * lax.rsqrt(var + LN_EPS) * g + b


def _mod_row_of_tile(tile_rows):
    per_batch = SEQ // tile_rows
    return lambda i: jnp.minimum(i // per_batch, BATCH)


def _mod_spec(layer, k, row_of_tile):
    return pl.BlockSpec((None, 1, D_MODEL),
                        lambda i, *_: (layer * 48 + row_of_tile(i) * 6 + k, 0, 0))


def _ada_kernel(c_ref, w_ref, b_ref, o_ref):
    a = _silu(c_ref[...]).astype(BF)
    o_ref[...] = _dot(a, w_ref[...].astype(BF)) + b_ref[...]


def _ada_table(cvec, ada_w, ada_b):
    tn = 1024
    out = pl.pallas_call(
        _ada_kernel,
        out_shape=jax.ShapeDtypeStruct((DEPTH, 8, 6 * D_MODEL), F32),
        grid=(DEPTH, 6 * D_MODEL // tn),
        in_specs=[pl.BlockSpec((8, D_MODEL), lambda l, j: (0, 0)),
                  pl.BlockSpec((None, D_MODEL, tn), lambda l, j: (l, 0, j)),
                  pl.BlockSpec((None, 1, tn), lambda l, j: (l, 0, j))],
        out_specs=pl.BlockSpec((None, 8, tn), lambda l, j: (l, 0, j)),
        compiler_params=_params("parallel", "parallel"),
        name="ada_table",
    )(cvec, ada_w, ada_b.reshape(DEPTH, 1, 6 * D_MODEL))
    return out.reshape(DEPTH * 8 * 6, 1, D_MODEL)


def _select_rows(i, n_lat_tiles, xl_ref, xc_ref):
    return jnp.where(i < n_lat_tiles, xl_ref[...], xc_ref[...])


def _split_row_specs(tm):
    n_lat = NL // tm
    return (pl.BlockSpec((tm, D_MODEL), lambda i: (jnp.minimum(i, n_lat - 1), 0)),
            pl.BlockSpec((tm, D_MODEL), lambda i: (jnp.maximum(i - n_lat, 0), 0)))


def _modulate_kernel(xl_ref, xc_ref, sh_ref, sc_ref, o_ref, *, n_lat_tiles):
    x = _select_rows(pl.program_id(0), n_lat_tiles, xl_ref, xc_ref)
    o_ref[...] = (x * (1.0 + sc_ref[...]) + sh_ref[...]).astype(BF)


def _modulate(x_lat, x_ctx, mod, layer):
    tm = 512
    rot = _mod_row_of_tile(tm)
    return pl.pallas_call(
        functools.partial(_modulate_kernel, n_lat_tiles=NL // tm),
        out_shape=jax.ShapeDtypeStruct((NT, D_MODEL), BF),
        grid=(NT // tm,),
        in_specs=[*_split_row_specs(tm), _mod_spec(layer, 0, rot), _mod_spec(layer, 1, rot)],
        out_specs=pl.BlockSpec((tm, D_MODEL), lambda i: (i, 0)),
        compiler_params=_params("parallel"),
        name="modulate",
    )(x_lat, x_ctx, mod, mod)


def _mm_wt_kernel(a_ref, wt_ref, o_ref, wb_ref, *, first_tile_scale):
    @pl.when(pl.program_id(1) == 0)
    def _():
        wb_ref[...] = wt_ref[0].T.astype(BF)

    acc = _dot(a_ref[...], wb_ref[...])
    if first_tile_scale is not None:
        acc = acc * jnp.where(pl.program_id(0) == 0, first_tile_scale, 1.0)
    o_ref[...] = acc.astype(o_ref.dtype)


def _project(a, wt_stack, layer, col0, n, tm, tn, name, out_dtype=F32, first_tile_scale=None):
    m, k = a.shape
    assert col0 % 8 == 0
    return pl.pallas_call(
        functools.partial(_mm_wt_kernel, first_tile_scale=first_tile_scale),
        out_shape=jax.ShapeDtypeStruct((m, n), out_dtype),
        grid=(n // tn, m // tm),
        in_specs=[pl.BlockSpec((tm, k), lambda j, i: (i, 0)),
                  pl.BlockSpec((pl.Element(1), pl.Element(tn), pl.Element(k)),
                               lambda j, i: (layer, pl.multiple_of(col0 + j * tn, 8), 0))],
        out_specs=pl.BlockSpec((tm, tn), lambda j, i: (i, j)),
        scratch_shapes=[pltpu.VMEM((k, tn), BF)],
        compiler_params=_params("arbitrary", "arbitrary"),
        name=name,
    )(a, wt_stack)


CONV_TM = 256


def _conv_kernel(b_ref, c_ref, x_ref, cp_ref, xp_ref, cn_ref, xn_ref, w_ref, o_ref):
    i = pl.program_id(0)
    tiles_per_seq = SEQ // CONV_TM
    is_ctx = i >= NL // CONV_TM
    is_start = jnp.logical_or(i % tiles_per_seq == 0, is_ctx)
    is_end = jnp.logical_or(i % tiles_per_seq == tiles_per_seq - 1, is_ctx)
    z = c_ref[...] * x_ref[...]
    zp = cp_ref[...] * xp_ref[...]
    zn = cn_ref[...] * xn_ref[...]
    prev_row = jnp.where(is_start, 0.0, zp[7:8, :])
    next_row = jnp.where(is_end, 0.0, zn[0:1, :])
    rows = lax.broadcasted_iota(jnp.int32, z.shape, 0)
    z_prev = jnp.where(rows == 0, prev_row, pltpu.roll(z, 1, 0))
    z_next = jnp.where(rows == CONV_TM - 1, next_row, pltpu.roll(z, CONV_TM - 1, 0))
    w = w_ref[...]
    y = w[0:1, :] * z_prev + w[1:2, :] * z + w[2:3, :] * z_next
    o_ref[...] = (b_ref[...] * y).astype(BF)


def _short_conv(u, conv_w, layer):
    tm = CONV_TM
    n8 = tm // 8
    last8 = NT // 8 - 1
    return pl.pallas_call(
        _conv_kernel,
        out_shape=jax.ShapeDtypeStruct((NT, CONV_DIM), BF),
        grid=(NT // tm,),
        in_specs=[pl.BlockSpec((tm, CONV_DIM), lambda i: (i, U_CONV_B // CONV_DIM)),
                  pl.BlockSpec((tm, CONV_DIM), lambda i: (i, U_CONV_C // CONV_DIM)),
                  pl.BlockSpec((tm, CONV_DIM), lambda i: (i, U_CONV_X // CONV_DIM)),
                  pl.BlockSpec((8, CONV_DIM), lambda i: (jnp.maximum(i * n8 - 1, 0), U_CONV_C // CONV_DIM)),
                  pl.BlockSpec((8, CONV_DIM), lambda i: (jnp.maximum(i * n8 - 1, 0), U_CONV_X // CONV_DIM)),
                  pl.BlockSpec((8, CONV_DIM), lambda i: (jnp.minimum(i * n8 + n8, last8), U_CONV_C // CONV_DIM)),
                  pl.BlockSpec((8, CONV_DIM), lambda i: (jnp.minimum(i * n8 + n8, last8), U_CONV_X // CONV_DIM)),
                  pl.BlockSpec((None, CONV_K, CONV_DIM), lambda i: (layer, 0, 0))],
        out_specs=pl.BlockSpec((tm, CONV_DIM), lambda i: (i, 0)),
        compiler_params=_params("parallel"),
        name="short_conv",
    )(u, u, u, u, u, u, u, conv_w)


GLA_STEPS = (CTX_LEN + SEQ) // GLA_CHUNK
GLA_CTX_STEPS = CTX_LEN // GLA_CHUNK


def _log_sigmoid(z):
    return -(jnp.maximum(-z, 0.0) + jnp.log1p(jnp.exp(-jnp.abs(z))))


def _rope(x, cs, sn):
    lane = lax.broadcasted_iota(jnp.int32, x.shape, 1)
    swapped = jnp.where(lane % 2 == 0, pltpu.roll(x, GLA_DK - 1, 1), pltpu.roll(x, 1, 1))
    return x * cs + swapped * sn


def _gla_decayed_operands(q_ref, k_ref, l_ref, r_ref, w_ref, b_ref, reverse):
    c = GLA_CHUNK
    z = _dot(l_ref[...].astype(BF), w_ref[...]) + b_ref[...]
    g = _log_sigmoid(z) * (1.0 / GLA_TAU)
    t_i = lax.broadcasted_iota(jnp.int32, (c, c), 0)
    s_i = lax.broadcasted_iota(jnp.int32, (c, c), 1)
    keep = (s_i >= t_i) if reverse else (s_i <= t_i)
    tri = jnp.where(keep, 1.0, 0.0).astype(BF)
    g1 = g.astype(BF)
    r1 = g - g1.astype(F32)
    g2 = r1.astype(BF)
    g3 = (r1 - g2.astype(F32)).astype(BF)
    b = _dot(tri, g1) + _dot(tri, g2) + _dot(tri, g3)
    total = b[0:1, :] if reverse else b[c - 1:c, :]
    cs = r_ref[:, :GLA_DK]
    sn = r_ref[:, GLA_DK:]
    heads = [slice(h * GLA_DK, (h + 1) * GLA_DK) for h in range(GLA_HEADS)]
    qs = jnp.concatenate([_rope(q_ref[:, hd] * (GLA_DK ** -0.5), cs, sn) for hd in heads], axis=1)
    kr = jnp.concatenate([_rope(k_ref[:, hd], cs, sn) for hd in heads], axis=1)
    q_dec = (qs * jnp.exp(b)).astype(BF)
    k_inv = (kr * jnp.exp(-b)).astype(BF)
    k_end = (kr * jnp.exp(total - b)).astype(BF)
    return keep, q_dec, k_inv, k_end, jnp.exp(total)


def _gla_kernel(qf_ref, kf_ref, vf_ref, lf_ref, rf_ref, qb_ref, kb_ref, vb_ref, lb_ref, rb_ref,
                wf_ref, bf_ref, wb_ref, bb_ref, of_ref, ob_ref, stf_ref, stb_ref):
    @pl.when(pl.program_id(1) == 0)
    def _():
        stf_ref[...] = jnp.zeros_like(stf_ref)
        stb_ref[...] = jnp.zeros_like(stb_ref)

    pre = [_gla_decayed_operands(qf_ref, kf_ref, lf_ref, rf_ref, wf_ref, bf_ref, False),
           _gla_decayed_operands(qb_ref, kb_ref, lb_ref, rb_ref, wb_ref, bb_ref, True)]
    chains = [(d, h) for d in range(2) for h in range(GLA_HEADS)]
    v_refs, o_refs, st_refs = (vf_ref, vb_ref), (of_ref, ob_ref), (stf_ref, stb_ref)

    def head(x, h):
        return x[:, h * GLA_DK:(h + 1) * GLA_DK]

    v = [v_refs[d][:, h * GLA_DV:(h + 1) * GLA_DV].astype(BF) for d, h in chains]
    st = [st_refs[d][h] for d, h in chains]
    att = [jnp.where(pre[d][0], _dot_nt(head(pre[d][1], h), head(pre[d][2], h)), 0.0).astype(BF)
           for d, h in chains]
    o = [_dot(att[i], v[i]) + _dot_nt(head(pre[d][1], h), st[i].astype(BF)) for i, (d, h) in enumerate(chains)]
    for d in range(2):
        o_refs[d][...] = jnp.concatenate(o[d * GLA_HEADS:(d + 1) * GLA_HEADS], axis=1)
    for i, (d, h) in enumerate(chains):
        st_refs[d][h] = st[i] * head(pre[d][4], h) + _dot_tn(v[i], head(pre[d][3], h))


def _gla(u, lr, rope_tab, wpad_f, bg_f, wpad_b, bg_b):
    c = GLA_CHUNK
    lat_chunks = SEQ // c
    ctx_base = NL // c

    def row_f(b, s):
        return jnp.where(s < GLA_CTX_STEPS, ctx_base + b * GLA_CTX_STEPS + s, b * lat_chunks + s - GLA_CTX_STEPS)

    def row_b(b, s):
        return jnp.where(s < GLA_CTX_STEPS, ctx_base + b * GLA_CTX_STEPS + GLA_CTX_STEPS - 1 - s,
                         b * lat_chunks + GLA_STEPS - 1 - s)

    def rope_f(s):
        return jnp.where(s < GLA_CTX_STEPS, lat_chunks, s - GLA_CTX_STEPS)

    def rope_b(s):
        return jnp.where(s < GLA_CTX_STEPS, lat_chunks, GLA_STEPS - 1 - s)

    def stream(row, rope):
        return [pl.BlockSpec((c, GLA_QK), lambda b, s: (row(b, s), U_GLA_Q // GLA_QK)),
                pl.BlockSpec((c, GLA_QK), lambda b, s: (row(b, s), U_GLA_K // GLA_QK)),
                pl.BlockSpec((c, GLA_V), lambda b, s: (row(b, s), U_GLA_V // GLA_V)),
                pl.BlockSpec((c, LR_COLS), lambda b, s: (row(b, s), 0)),
                pl.BlockSpec((c, 2 * GLA_DK), lambda b, s: (rope(s), 0))]

    head_w = pl.BlockSpec((LR_COLS, GLA_QK), lambda b, s: (0, 0))
    head_b = pl.BlockSpec((1, GLA_QK), lambda b, s: (0, 0))
    state = pltpu.VMEM((GLA_HEADS, GLA_DV, GLA_DK), F32)
    return pl.pallas_call(
        _gla_kernel,
        out_shape=(jax.ShapeDtypeStruct((NT, GLA_V), F32), jax.ShapeDtypeStruct((NT, GLA_V), F32)),
        grid=(BATCH, GLA_STEPS),
        in_specs=stream(row_f, rope_f) + stream(row_b, rope_b) + [head_w, head_b, head_w, head_b],
        out_specs=(pl.BlockSpec((c, GLA_V), lambda b, s: (row_f(b, s), 0)),
                   pl.BlockSpec((c, GLA_V), lambda b, s: (row_b(b, s), 0))),
        scratch_shapes=[state, state],
        compiler_params=_params("parallel", "arbitrary"),
        name="gla",
    )(u, u, u, lr, rope_tab, u, u, u, lr, rope_tab, wpad_f, bg_f, wpad_b, bg_b)


def _rope_table():
    t = jnp.arange(SEQ)
    rows = (t // GRID_W).astype(F32)
    cols = (t % GRID_W).astype(F32)
    n_freq = GLA_DK // 4
    inv = 1.0 / (ROPE_BASE ** (jnp.arange(n_freq, dtype=F32) / n_freq))
    ang = jnp.concatenate([rows[:, None] * inv, cols[:, None] * inv], -1)
    cos = jnp.repeat(jnp.cos(ang), 2, axis=-1)
    sin = jnp.repeat(jnp.sin(ang), 2, axis=-1)
    sign = jnp.tile(jnp.array([-1.0, 1.0], F32), GLA_DK // 2)
    cos = jnp.concatenate([cos, jnp.ones((GRID_W, GLA_DK), F32)], 0)
    sin = jnp.concatenate([sin * sign, jnp.zeros((GRID_W, GLA_DK), F32)], 0)
    return jnp.concatenate([cos, sin], axis=1)


def _decay_weights(wg, bg, lane_offset):
    w = jnp.pad(wg, ((lane_offset, LR_COLS - GLA_RANK - lane_offset), (0, 0))).astype(BF)
    return w, bg.reshape(1, GLA_QK)


NA_HPS = 4


def _na_softmax_out(parts):
    m = functools.reduce(jnp.maximum, [jnp.max(s, axis=-1, keepdims=True) for s, _ in parts])
    ps = [jnp.exp(s - m) for s, _ in parts]
    l = functools.reduce(jnp.add, [jnp.sum(p, axis=-1, keepdims=True) for p in ps])
    o = functools.reduce(jnp.add, [_dot(p.astype(BF), v) for p, (_, v) in zip(ps, parts)])
    return (o / l).astype(BF)


def _na_kernel(q_ref, k_ref, v_ref, kc_ref, vc_ref, bias_ref, o_ref):
    j = pl.program_id(2)
    head_cols = [slice(h * NA_DH, (h + 1) * NA_DH) for h in range(NA_HPS)]

    @pl.when(j < NA_NBLK)
    def _():
        base = jnp.clip(NA_QROWS * j - WIN_R // 2, 0, GRID_ROWS - NA_KROWS)
        start = pl.multiple_of(base * GRID_W, GRID_W)

        def pair_entry(qr, kp):
            r = NA_QROWS * j + qr
            rs = jnp.clip(r - WIN_R // 2, 0, GRID_ROWS - WIN_R)
            a0 = base + 2 * kp
            dr0 = a0 - r + WIN_R - 1
            in0 = jnp.logical_and(a0 >= rs, a0 < rs + WIN_R)
            in1 = jnp.logical_and(a0 + 1 >= rs, a0 + 1 < rs + WIN_R)
            entry = jnp.where(jnp.logical_and(in0, in1), NA_PAIR_BOTH + dr0,
                              jnp.where(in0, NA_PAIR_LEFT + dr0,
                                        jnp.where(in1, NA_PAIR_RIGHT + dr0 + 1, NA_PAIR_NONE)))
            return jnp.clip(entry, 0, NA_PAIR_ENTRIES - 1)

        entries = [[pair_entry(qr, kp) for kp in range(NA_KPAIRS)] for qr in range(NA_QROWS)]
        for h, hc in enumerate(head_cols):
            q = q_ref[:, hc]
            vl = v_ref[pl.ds(start, NA_KBLK), hc]
            bias = jnp.concatenate(
                [jnp.concatenate([bias_ref[h, entries[qr][kp]] for kp in range(NA_KPAIRS)], axis=1)
                 for qr in range(NA_QROWS)], axis=0)
            s_loc = _dot_nt(q, k_ref[pl.ds(start, NA_KBLK), hc]) + bias
            s_ctx = _dot_nt(q, kc_ref[:, hc])
            o_ref[:, hc] = _na_softmax_out([(s_loc, vl), (s_ctx, vc_ref[:, hc])])

    @pl.when(j == NA_NBLK)
    def _():
        for hc in head_cols:
            s_ctx = _dot_nt(q_ref[:, hc], kc_ref[:, hc])
            o_ref[:, hc] = _na_softmax_out([(s_ctx, vc_ref[:, hc])])


def _na_bias_pair_tables(rpb_all):
    qc = np.arange(GRID_W)
    cs = np.clip(qc - WIN_C // 2, 0, GRID_W - WIN_C)
    kc = np.arange(GRID_W)
    col_ok = (kc[None, :] >= cs[:, None]) & (kc[None, :] < cs[:, None] + WIN_C)
    dc = kc[None, :] - qc[:, None] + WIN_C - 1
    col_sel = ((dc[None] == np.arange(2 * WIN_C - 1)[:, None, None]) & col_ok[None]).astype(np.float32)
    t = jnp.einsum('lhrd,dxy->lhrxy', rpb_all, jnp.asarray(col_sel), precision=lax.Precision.HIGHEST)
    t = jnp.where(jnp.asarray(col_ok), t, NEG_INF)
    masked = jnp.full_like(t, NEG_INF)
    both = jnp.concatenate([t[:, :, :-1], t[:, :, 1:]], axis=-1)
    left = jnp.concatenate([t, masked], axis=-1)
    right = jnp.concatenate([masked, t], axis=-1)
    none = jnp.concatenate([masked[:, :, :1], masked[:, :, :1]], axis=-1)
    return jnp.concatenate([both, left, right, none], axis=2)


def _neighbourhood_attention(un, bias_pairs, layer, with_ctx):
    steps = NA_NBLK + (1 if with_ctx else 0)
    ctx_q = NL // NA_QBLK
    w = NA_HPS * NA_DH

    def q_row(b, j):
        return jnp.where(j < NA_NBLK, b * NA_NBLK + j, ctx_q + b)

    ctx_blk = NL // CTX_LEN
    return pl.pallas_call(
        _na_kernel,
        out_shape=jax.ShapeDtypeStruct((NT, NA_W), BF),
        grid=(BATCH, NA_HEADS // NA_HPS, steps),
        in_specs=[pl.BlockSpec((NA_QBLK, w), lambda b, h, j: (q_row(b, j), UN_Q // w + h)),
                  pl.BlockSpec((SEQ, w), lambda b, h, j: (b, UN_K // w + h)),
                  pl.BlockSpec((SEQ, w), lambda b, h, j: (b, UN_V // w + h)),
                  pl.BlockSpec((CTX_LEN, w), lambda b, h, j: (ctx_blk + b, UN_K // w + h)),
                  pl.BlockSpec((CTX_LEN, w), lambda b, h, j: (ctx_blk + b, UN_V // w + h)),
                  pl.BlockSpec((None, NA_HPS, NA_PAIR_ENTRIES, GRID_W, 2 * GRID_W),
                               lambda b, h, j: (layer, h, 0, 0, 0))],
        out_specs=pl.BlockSpec((NA_QBLK, w), lambda b, h, j: (q_row(b, j), h)),
        compiler_params=_params("parallel", "parallel", "arbitrary"),
        name="neighbourhood_attention",
    )(un, un, un, un, un, bias_pairs)


MERGE_TM = 256


def _merge_kernel(yc_ref, of_ref, ob_ref, r_ref, yn_ref, gt_ref, xl_ref, xc_ref,
                  ng_ref, gb_ref, wc_ref, wg_ref, wn_ref, wo_ref,
                  g1_ref, lg_ref, lb_ref, sh2_ref, sc2_ref, xo_ref, ho_ref):
    x = _select_rows(pl.program_id(0), NL // MERGE_TM, xl_ref, xc_ref)
    o = of_ref[...] + ob_ref[...]
    r = r_ref[...]
    ng = ng_ref[...]
    heads = []
    for h in range(GLA_HEADS):
        oh = o[:, h * GLA_DV:(h + 1) * GLA_DV]
        oh = oh * lax.rsqrt(jnp.mean(oh * oh, axis=-1, keepdims=True) + LN_EPS)
        heads.append(oh * ng * _silu(r[:, h * GLA_DV:(h + 1) * GLA_DV]))
    y_gla = jnp.concatenate(heads, axis=-1).astype(BF)
    g = jax.nn.sigmoid(gt_ref[...] + gb_ref[...])
    z = (g[:, :D_MODEL] * _dot(yc_ref[...], wc_ref[...])
         + g[:, D_MODEL:2 * D_MODEL] * _dot(y_gla, wg_ref[...])
         + g[:, 2 * D_MODEL:] * _dot(yn_ref[...], wn_ref[...]))
    y = _dot(z.astype(BF), wo_ref[...])
    xn = _layer_norm(DEEPNORM_ALPHA * x + g1_ref[...] * y, lg_ref[...], lb_ref[...])
    xo_ref[...] = xn
    ho_ref[...] = (xn * (1.0 + sc2_ref[...]) + sh2_ref[...]).astype(BF)


def _merge(y_conv, o_f, o_b, u, ug, y_na, x_lat, x_ctx, mod, layer, rows, norm_g, gate_b, w_conv, w_gla, w_na,
           w_out, ln_g, ln_b):
    tm = MERGE_TM
    rot = _mod_row_of_tile(tm)
    row = lambda width: pl.BlockSpec((tm, width), lambda i: (i, 0))
    const = lambda shape: pl.BlockSpec(shape, lambda i: (0,) * len(shape), pipeline_mode=pl.Buffered(1))
    return pl.pallas_call(
        _merge_kernel,
        out_shape=(jax.ShapeDtypeStruct((rows, D_MODEL), F32), jax.ShapeDtypeStruct((rows, D_MODEL), BF)),
        grid=(rows // tm,),
        in_specs=[row(CONV_DIM), row(GLA_V), row(GLA_V),
                  pl.BlockSpec((tm, GLA_V), lambda i: (i, U_GLA_R // GLA_V)),
                  row(NA_W), row(N_BRANCH * D_MODEL), *_split_row_specs(tm),
                  const((1, GLA_DV)), const((1, N_BRANCH * D_MODEL)),
                  const((CONV_DIM, D_MODEL)), const((GLA_V, D_MODEL)), const((NA_W, D_MODEL)),
                  const((D_MODEL, D_MODEL)),
                  _mod_spec(layer, 2, rot), const((1, D_MODEL)), const((1, D_MODEL)),
                  _mod_spec(layer, 3, rot), _mod_spec(layer, 4, rot)],
        out_specs=(row(D_MODEL), row(D_MODEL)),
        compiler_params=_params("parallel"),
        name="merge",
    )(y_conv, o_f, o_b, u, y_na, ug, x_lat, x_ctx, norm_g, gate_b, w_conv, w_gla, w_na, w_out,
      mod, ln_g, ln_b, mod, mod)


FFN_TM_LATENT = 1024
FFN_TM_CONTEXT = NC
FFN_TF = 256
FFN_EPILOGUE_ROWS = 128


def _ffn_kernel(h_ref, wg_ref, wu_ref, wd_ref, x_ref, g2_ref, lg_ref, lb_ref, shn_ref, scn_ref, *refs):
    xo_ref, ho_ref, acc_ref = refs[-3:]
    j = pl.program_id(1)

    @pl.when(j == 0)
    def _():
        acc_ref[...] = jnp.zeros_like(acc_ref)

    h = h_ref[...]
    a = (_silu(_dot(h, wg_ref[...].astype(BF))) * _dot(h, wu_ref[...].astype(BF))).astype(BF)
    acc_ref[...] += _dot(a, wd_ref[...].astype(BF))

    @pl.when(j == pl.num_programs(1) - 1)
    def _():
        def norm_rows(c, carry):
            rows = pl.ds(pl.multiple_of(c * FFN_EPILOGUE_ROWS, FFN_EPILOGUE_ROWS), FFN_EPILOGUE_ROWS)
            xn = _layer_norm(DEEPNORM_ALPHA * x_ref[rows, :] + g2_ref[...] * acc_ref[rows, :],
                             lg_ref[...], lb_ref[...])
            xo_ref[rows, :] = xn
            ho_ref[rows, :] = (xn * (1.0 + scn_ref[...]) + shn_ref[...]).astype(BF)
            return carry

        lax.fori_loop(0, acc_ref.shape[0] // FFN_EPILOGUE_ROWS, norm_rows, 0)


def _dense_ffn_rows(h2, x_all, mod, layer, wg, wu, wd, ffn_layer, ln_g, ln_b, row0, n_rows, tm, prev):
    tile0 = row0 // tm
    rot = _mod_row_of_tile(tm)
    mod_spec = lambda lyr, k: pl.BlockSpec((None, 1, D_MODEL),
                                           lambda i, j: (lyr * 48 + rot(tile0 + i) * 6 + k, 0, 0))
    row = lambda **kw: pl.BlockSpec((tm, D_MODEL), lambda i, j: (tile0 + i, 0), **kw)
    const = lambda: pl.BlockSpec((1, D_MODEL), lambda i, j: (0, 0))
    once = dict(pipeline_mode=pl.Buffered(1))
    in_specs = [row(),
                pl.BlockSpec((None, D_MODEL, FFN_TF), lambda i, j: (ffn_layer, 0, j)),
                pl.BlockSpec((None, D_MODEL, FFN_TF), lambda i, j: (ffn_layer, 0, j)),
                pl.BlockSpec((None, FFN_TF, D_MODEL), lambda i, j: (ffn_layer, j, 0)),
                row(**once), mod_spec(layer, 5), const(), const(),
                mod_spec(layer + 1, 0), mod_spec(layer + 1, 1)]
    args = [h2, wg, wu, wd, x_all, mod, ln_g, ln_b, mod, mod]
    aliases = {}
    if prev is not None:
        aliases = {len(args): 0, len(args) + 1: 1}
        in_specs += [pl.BlockSpec(memory_space=pl.ANY), pl.BlockSpec(memory_space=pl.ANY)]
        args += list(prev)
    return pl.pallas_call(
        _ffn_kernel,
        out_shape=(jax.ShapeDtypeStruct((NT, D_MODEL), F32), jax.ShapeDtypeStruct((NT, D_MODEL), BF)),
        grid=(n_rows // tm, D_FF // FFN_TF),
        in_specs=in_specs,
        out_specs=(row(**once), row(**once)),
        scratch_shapes=[pltpu.VMEM((tm, D_MODEL), F32)],
        input_output_aliases=aliases,
        compiler_params=_params("parallel", "arbitrary"),
        name="dense_ffn",
    )(*args)


def _dense_ffn(h2, x_all, mod, layer, wg, wu, wd, ffn_layer, ln_g, ln_b):
    out = _dense_ffn_rows(h2, x_all, mod, layer, wg, wu, wd, ffn_layer, ln_g, ln_b, 0, NL, FFN_TM_LATENT, None)
    return _dense_ffn_rows(h2, x_all, mod, layer, wg, wu, wd, ffn_layer, ln_g, ln_b, NL, NC, FFN_TM_CONTEXT, out)


def _router_kernel(h_ref, w_ref, idx_ref, wt_ref):
    logits = _dot(h_ref[...], w_ref[...])
    lane = lax.broadcasted_iota(jnp.int32, logits.shape, 1)
    logits = jnp.where(lane < N_EXPERTS, logits, -jnp.inf)
    m1 = jnp.max(logits, axis=-1, keepdims=True)
    i1 = jnp.min(jnp.where(logits == m1, lane, ROUTER_LANES), axis=-1, keepdims=True)
    rest = jnp.where(lane == i1, -jnp.inf, logits)
    m2 = jnp.max(rest, axis=-1, keepdims=True)
    i2 = jnp.min(jnp.where(rest == m2, lane, ROUTER_LANES), axis=-1, keepdims=True)
    e2 = jnp.exp(m2 - m1)
    w1 = 1.0 / (1.0 + e2)
    w2 = e2 / (1.0 + e2)
    idx_ref[...] = jnp.where(lane == 0, i1, jnp.where(lane == 1, i2, 0))
    wt_ref[...] = jnp.where(lane == 0, w1, jnp.where(lane == 1, w2, 0.0))


def _router(h2, router_w):
    tm = 512
    return pl.pallas_call(
        _router_kernel,
        out_shape=(jax.ShapeDtypeStruct((NL, ROUTER_LANES), jnp.int32),
                   jax.ShapeDtypeStruct((NL, ROUTER_LANES), F32)),
        grid=(NL // tm,),
        in_specs=[pl.BlockSpec((tm, D_MODEL), lambda i: (i, 0)),
                  pl.BlockSpec((D_MODEL, ROUTER_LANES), lambda i: (0, 0))],
        out_specs=(pl.BlockSpec((tm, ROUTER_LANES), lambda i: (i, 0)),
                   pl.BlockSpec((tm, ROUTER_LANES), lambda i: (i, 0))),
        compiler_params=_params("parallel"),
        name="moe_router",
    )(h2, router_w)


def _dispatch_plan(idx):
    i32 = jnp.int32
    experts = jnp.arange(N_EXPERTS, dtype=i32)
    oh_a = (idx[:, 0:1] == experts).astype(i32)
    oh_b = (idx[:, 1:2] == experts).astype(i32)
    cum = jnp.cumsum(oh_a + oh_b, axis=0)
    counts = cum[-1]
    n_st = (counts + MOE_SM - 1) // MOE_SM
    st_end = jnp.cumsum(n_st)
    st_start = st_end - n_st
    g_start = st_start * MOE_SM
    g_end = st_end * MOE_SM
    slot_table = g_start[None, :] + cum - 1
    slot_a = jnp.sum(oh_a * slot_table, axis=1)
    slot_b = jnp.sum(oh_b * slot_table, axis=1)

    n_used = st_end[-1]
    st_ids = jnp.minimum(jnp.arange(MOE_NST, dtype=i32), n_used - 1)
    st_expert = jnp.minimum(jnp.sum((st_ids[:, None] >= st_end[None, :]).astype(i32), axis=1), N_EXPERTS - 1)
    st_left = counts[st_expert] - (st_ids - st_start[st_expert]) * MOE_SM
    st_rows = ((jnp.clip(st_left, 0, MOE_SM) + MOE_GB - 1) // MOE_GB) * MOE_GB
    tile_plan = (st_expert, st_rows, n_used.reshape(1))

    pair_ids = jnp.arange(MOE_NPAIR, dtype=i32)
    tile_last = jnp.arange(MOE_NTT, dtype=i32) * MOE_GB + MOE_GB - 1
    hi = cum[tile_last]
    lo = jnp.concatenate([jnp.zeros((1, N_EXPERTS), i32), hi[:-1]], axis=0)

    blk_ids = jnp.arange(MOE_NGB, dtype=i32)
    blk_start = blk_ids * MOE_GB
    blk_e = jnp.minimum(jnp.sum((blk_start[:, None] >= g_end[None, :]).astype(i32), axis=1), N_EXPERTS - 1)
    r0 = blk_start - g_start[blk_e]
    r1 = jnp.minimum(r0 + MOE_GB, counts[blk_e])
    has = (blk_start < g_end[-1]) & (r1 > r0)
    hi_e = hi[:, blk_e]
    t_lo = jnp.sum((hi_e <= r0[None, :]).astype(i32), axis=0)
    t_hi = jnp.sum((hi_e < r1[None, :]).astype(i32), axis=0)
    t_lo = jnp.where(has, t_lo, 0)
    t_end = jnp.where(has, t_hi + 1, 0)
    blk_map = lax.cummax(jnp.where(has, blk_ids, 0), axis=0)
    dispatch = (t_lo, t_end, blk_map, has.astype(i32))

    b_lo = (g_start[None, :] + lo) // MOE_GB
    b_hi = (g_start[None, :] + hi - 1) // MOE_GB
    n_c = jnp.where(hi > lo, b_hi - b_lo + 1, 0).reshape(-1)
    b_lo = b_lo.reshape(-1)
    c_end = jnp.cumsum(n_c)
    c_total = c_end[-1]
    c_idx = jnp.minimum(jnp.sum((pair_ids[:, None] >= c_end[None, :]).astype(i32), axis=1),
                        MOE_NTT * N_EXPERTS - 1)
    c_blk = jnp.clip(b_lo[c_idx] + pair_ids - (c_end[c_idx] - n_c[c_idx]), 0, MOE_NGB - 1)
    c_tile = c_idx // N_EXPERTS
    c_valid = pair_ids < c_total
    c_tile = jnp.where(c_valid, c_tile, MOE_NTT - 1)
    c_blk = jnp.where(c_valid, c_blk, c_blk[jnp.maximum(c_total - 1, 0)])
    c_prev = jnp.concatenate([jnp.full((1,), -1, i32), c_tile[:-1]])
    c_next = jnp.concatenate([c_tile[1:], jnp.full((1,), -1, i32)])
    c_first = c_valid & (c_tile != c_prev)
    c_last = c_valid & ((c_tile != c_next) | (pair_ids == c_total - 1))
    combine = (c_tile, c_blk, c_first.astype(i32), c_last.astype(i32), c_valid.astype(i32))
    return slot_a, slot_b, tile_plan, dispatch, combine


def _dispatch_kernel(lo_ref, end_ref, map_ref, has_ref, h_ref, sa_ref, sb_ref, wa_ref, wb_ref, hs_ref, ws_ref,
                     acc_ref, wacc_ref):
    blk = pl.program_id(0)

    @pl.when(has_ref[blk] == 1)
    def _():
        acc_ref[...] = jnp.zeros_like(acc_ref)
        wacc_ref[...] = jnp.zeros_like(wacc_ref)
        slots = blk * MOE_GB + lax.broadcasted_iota(jnp.int32, (MOE_GB, MOE_GB), 0)

        def scan_tile(t, carry):
            rows = pl.ds(pl.multiple_of(t * MOE_GB, MOE_GB), MOE_GB)
            hit_a = slots == sa_ref[t]
            hit_b = slots == sb_ref[t]
            onehot = jnp.where(hit_a, 1.0, jnp.where(hit_b, 1.0, 0.0)).astype(BF)
            acc_ref[...] += _dot(onehot, h_ref[rows, :])
            w = jnp.where(hit_a, wa_ref[t], jnp.where(hit_b, wb_ref[t], 0.0))
            wacc_ref[...] += jnp.broadcast_to(jnp.sum(w, axis=1, keepdims=True), wacc_ref.shape)
            return carry

        lax.fori_loop(lo_ref[blk], end_ref[blk], scan_tile, 0)
        hs_ref[...] = acc_ref[...].astype(BF)
        ws_ref[...] = wacc_ref[...]


def _dispatch(h2, slot_a, slot_b, w_a, w_b, plan):
    used = lambda i, lo, end, blk_map, has: (blk_map[i], 0)
    resident = lambda shape: pl.BlockSpec(shape, lambda i, *_: (0,) * len(shape), pipeline_mode=pl.Buffered(1))
    shape3 = (MOE_NTT, 1, MOE_GB)
    grid_spec = pltpu.PrefetchScalarGridSpec(
        num_scalar_prefetch=4,
        grid=(MOE_NGB,),
        in_specs=[resident((NL, D_MODEL)), resident(shape3), resident(shape3), resident(shape3),
                  resident(shape3)],
        out_specs=(pl.BlockSpec((MOE_GB, D_MODEL), used), pl.BlockSpec((MOE_GB, ROUTER_LANES), used)),
        scratch_shapes=[pltpu.VMEM((MOE_GB, D_MODEL), F32), pltpu.VMEM((MOE_GB, ROUTER_LANES), F32)])
    return pl.pallas_call(
        _dispatch_kernel,
        out_shape=(jax.ShapeDtypeStruct((MOE_NSLOT, D_MODEL), BF),
                   jax.ShapeDtypeStruct((MOE_NSLOT, ROUTER_LANES), F32)),
        grid_spec=grid_spec,
        compiler_params=_params("arbitrary"),
        name="moe_dispatch",
    )(*plan, h2, slot_a.reshape(shape3), slot_b.reshape(shape3), w_a.reshape(shape3), w_b.reshape(shape3))


def _expert_kernel(e_ref, rows_ref, n_ref, hs_ref, wg_ref, wu_ref, wd_ref, ws_ref, y_ref, acc_ref):
    s = pl.program_id(0)
    j = pl.program_id(1)
    last = pl.num_programs(1) - 1

    @pl.when(s < n_ref[0])
    def _():
        n_rows = rows_ref[s]

        def for_row_chunks(fn):
            n_full = n_rows // MOE_CH

            def full_chunk(c, carry):
                fn(pl.ds(pl.multiple_of(c * MOE_CH, MOE_CH), MOE_CH))
                return carry

            lax.fori_loop(0, n_full, full_chunk, 0)
            rest = n_rows - n_full * MOE_CH
            half = MOE_CH // 2

            @pl.when(rest >= half)
            def _():
                fn(pl.ds(pl.multiple_of(n_full * MOE_CH, MOE_GB), half))

            @pl.when(rest % half != 0)
            def _():
                fn(pl.ds(pl.multiple_of(n_rows - MOE_GB, MOE_GB), MOE_GB))

        @pl.when(j == 0)
        def _():
            acc_ref[...] = jnp.zeros_like(acc_ref)

        def accumulate(rows):
            h = hs_ref[rows, :]
            a = (_silu(_dot(h, wg_ref[...].astype(BF))) * _dot(h, wu_ref[...].astype(BF))).astype(BF)
            acc_ref[rows, :] += _dot(a, wd_ref[...].astype(BF))

        for_row_chunks(accumulate)

        @pl.when(j == last)
        def _():
            def emit(rows):
                y_ref[rows, :] = (acc_ref[rows, :] * ws_ref[rows, 0:1]).astype(BF)

            for_row_chunks(emit)


def _experts(hs, tile_plan, wg, wu, wd, ws, moe_layer):
    n_ff = D_FF // MOE_TF

    def used(s, n):
        return jnp.minimum(s, n[0] - 1)

    def ff_tile(s, j, n):
        return jnp.where(s < n[0], j, n_ff - 1)

    rows_spec = lambda width, **kw: pl.BlockSpec((MOE_SM, width), lambda s, j, e, r, n: (used(s, n), 0), **kw)
    up_spec = pl.BlockSpec((None, None, D_MODEL, MOE_TF),
                           lambda s, j, e, r, n: (moe_layer, e[s], 0, ff_tile(s, j, n)))
    grid_spec = pltpu.PrefetchScalarGridSpec(
        num_scalar_prefetch=3,
        grid=(MOE_NST, n_ff),
        in_specs=[rows_spec(D_MODEL, pipeline_mode=pl.Buffered(1)), up_spec, up_spec,
                  pl.BlockSpec((None, None, MOE_TF, D_MODEL),
                               lambda s, j, e, r, n: (moe_layer, e[s], ff_tile(s, j, n), 0)),
                  rows_spec(ROUTER_LANES)],
        out_specs=rows_spec(D_MODEL, pipeline_mode=pl.Buffered(1)),
        scratch_shapes=[pltpu.VMEM((MOE_SM, D_MODEL), F32)])
    return pl.pallas_call(
        _expert_kernel,
        out_shape=jax.ShapeDtypeStruct((MOE_NSLOT, D_MODEL), BF),
        grid_spec=grid_spec,
        compiler_params=_params("arbitrary", "arbitrary"),
        name="moe_experts",
    )(*tile_plan, hs, wg, wu, wd, ws)


def _combine_kernel(tile_ref, blk_ref, first_ref, last_ref, valid_ref,
                    y_ref, sa_ref, sb_ref, x_ref, g2_ref, lg_ref, lb_ref, o_ref, acc_ref):
    p = pl.program_id(0)

    @pl.when(first_ref[p] == 1)
    def _():
        acc_ref[...] = jnp.zeros_like(acc_ref)

    @pl.when(valid_ref[p] == 1)
    def _():
        slots = blk_ref[p] * MOE_GB + lax.broadcasted_iota(jnp.int32, (MOE_GB, MOE_GB), 1)
        onehot = jnp.where(slots == sa_ref[...], 1.0, jnp.where(slots == sb_ref[...], 1.0, 0.0)).astype(BF)
        acc_ref[...] += _dot(onehot, y_ref[...])

    @pl.when(last_ref[p] == 1)
    def _():
        o_ref[...] = _layer_norm(DEEPNORM_ALPHA * x_ref[...] + g2_ref[...] * acc_ref[...],
                                 lg_ref[...], lb_ref[...])


def _combine(y, slot_a, slot_b, x_lat, mod, layer, ln_g, ln_b, plan):
    per_batch = SEQ // MOE_GB
    slot_blk = lambda p, tile, blk, *_: (blk[p], 0)
    tok_blk = lambda p, tile, *_: (tile[p], 0)
    grid_spec = pltpu.PrefetchScalarGridSpec(
        num_scalar_prefetch=5,
        grid=(MOE_NPAIR,),
        in_specs=[pl.BlockSpec((MOE_GB, D_MODEL), slot_blk),
                  pl.BlockSpec((MOE_GB, 1), tok_blk), pl.BlockSpec((MOE_GB, 1), tok_blk),
                  pl.BlockSpec((MOE_GB, D_MODEL), tok_blk),
                  pl.BlockSpec((None, 1, D_MODEL),
                               lambda p, tile, *_: (layer * 48 + (tile[p] // per_batch) * 6 + 5, 0, 0)),
                  pl.BlockSpec((1, D_MODEL), lambda p, *_: (0, 0)),
                  pl.BlockSpec((1, D_MODEL), lambda p, *_: (0, 0))],
        out_specs=pl.BlockSpec((MOE_GB, D_MODEL), tok_blk),
        scratch_shapes=[pltpu.VMEM((MOE_GB, D_MODEL), F32)])
    return pl.pallas_call(
        _combine_kernel,
        out_shape=jax.ShapeDtypeStruct((NL, D_MODEL), F32),
        grid_spec=grid_spec,
        compiler_params=_params("arbitrary"),
        name="moe_combine",
    )(*plan, y, slot_a.reshape(NL, 1), slot_b.reshape(NL, 1), x_lat, mod, ln_g, ln_b)


def _moe_ffn(h2, x_lat, mod, layer, router_w, wg, wu, wd, moe_layer, ln_g, ln_b):
    router_pad = jnp.pad(router_w, ((0, 0), (0, ROUTER_LANES - N_EXPERTS))).astype(BF)
    idx, wts = _router(h2, router_pad)
    slot_a, slot_b, tile_plan, d_plan, c_plan = _dispatch_plan(idx[:, :TOP_K])
    hs, ws = _dispatch(h2, slot_a, slot_b, wts[:, 0], wts[:, 1], d_plan)
    y = _experts(hs, tile_plan, wg, wu, wd, ws, moe_layer)
    return _combine(y, slot_a, slot_b, x_lat, mod, layer, ln_g, ln_b, c_plan)


def kernel(x, c, ctx, c_ctx, ada_w, ada_b, w_in, conv_w, gla_wg_f, gla_bg_f, gla_wg_b, gla_bg_b, gla_norm_g, na_rpb, w_br_conv, w_br_gla, w_br_na, gate_b, w_out, ln1_g, ln1_b, ln2_g, ln2_b, ffn_w_gate, ffn_w_up, ffn_w_down, moe_router, moe_w_gate, moe_w_up, moe_w_down):
    assert DEPTH == 2 and x.shape == (BATCH, SEQ, D_MODEL) and ctx.shape == (BATCH, CTX_LEN, D_MODEL)
    cvec = jnp.concatenate([c, c_ctx[None, :], jnp.zeros((8 - BATCH - 1, D_MODEL), F32)], axis=0)
    mod = _ada_table(cvec, ada_w, ada_b)
    rope_tab = _rope_table()
    x_lat, x_ctx = x.reshape(NL, D_MODEL), ctx.reshape(NC, D_MODEL)
    h1 = _modulate(x_lat, x_ctx, mod, 0)
    w_in_t = jnp.swapaxes(w_in, 1, 2)
    na_bias = _na_bias_pair_tables(na_rpb)

    for layer in range(DEPTH):
        last = layer == DEPTH - 1
        u = _project(h1, w_in_t, layer, 0, U_COLS, 1088, 1024, "proj_in")
        lr = _project(h1, w_in_t, layer, W_IN_LR, LR_COLS, 1088, LR_COLS, "proj_decay")
        un = _project(h1, w_in_t, layer, W_IN_NA, UN_COLS, 1088, NA_W, "proj_na", out_dtype=BF,
                      first_tile_scale=NA_DH ** -0.5)
        ug = _project(h1, w_in_t, layer, W_IN_GATES, N_BRANCH * D_MODEL, 1088, 1024, "proj_gates")

        y_conv = _short_conv(u, conv_w, layer)
        wpad_f, bg_f = _decay_weights(gla_wg_f[layer], gla_bg_f[layer], 0)
        wpad_b, bg_b = _decay_weights(gla_wg_b[layer], gla_bg_b[layer], GLA_RANK)
        o_f, o_b = _gla(u, lr, rope_tab, wpad_f, bg_f, wpad_b, bg_b)
        y_na = _neighbourhood_attention(un, na_bias, layer, with_ctx=not last)

        rows = NL if last else NT
        x_mid, h2 = _merge(y_conv, o_f, o_b, u, ug, y_na, x_lat, x_ctx, mod, layer, rows,
                           gla_norm_g[layer].reshape(1, GLA_DV), gate_b[layer].reshape(1, N_BRANCH * D_MODEL),
                           w_br_conv[layer].astype(BF), w_br_gla[layer].astype(BF), w_br_na[layer].astype(BF),
                           w_out[layer].astype(BF), ln1_g[layer].reshape(1, D_MODEL),
                           ln1_b[layer].reshape(1, D_MODEL))
        ln_g = ln2_g[layer].reshape(1, D_MODEL)
        ln_b = ln2_b[layer].reshape(1, D_MODEL)
        if layer % 2 == 0:
            j = layer // 2
            x_lat, h1 = _dense_ffn(h2, x_mid, mod, layer, ffn_w_gate, ffn_w_up, ffn_w_down, j, ln_g, ln_b)
            x_ctx = x_lat
        else:
            x_lat = _moe_ffn(h2, x_mid, mod, layer, moe_router[layer // 2], moe_w_gate, moe_w_up, moe_w_down,
                             layer // 2, ln_g, ln_b)
    return x_lat.reshape(BATCH, SEQ, D_MODEL)
```

```python
import functools

import numpy as np
import jax
import jax.numpy as jnp
from jax import lax
from jax.experimental import pallas as pl
from jax.experimental.pallas import tpu as pltpu

D_MODEL = 2048
BATCH = 2
SEQ = 4096
DEPTH = 2
GRID_W = 64
CTX_LEN = 256
CONV_DIM = 1024
CONV_K = 3
GLA_HEADS = 4
GLA_DK = 128
GLA_DV = 256
GLA_RANK = 16
GLA_TAU = 16.0
GLA_CHUNK = 64
GLA_QK = GLA_HEADS * GLA_DK
GLA_V = GLA_HEADS * GLA_DV
NA_HEADS = 8
NA_DH = 128
NA_W = NA_HEADS * NA_DH
WIN_R = 8
WIN_C = 16
D_FF = 5632
N_EXPERTS = 8
TOP_K = 2
ROPE_BASE = 10000.0
LN_EPS = 1e-5
N_BRANCH = 3
DEEPNORM_ALPHA = (2 * DEPTH) ** 0.25

NL = BATCH * SEQ
NC = BATCH * CTX_LEN
NT = NL + NC
GRID_ROWS = SEQ // GRID_W

BF = jnp.bfloat16
F32 = jnp.float32

V7X_VMEM_BYTES = 64 * 1024 * 1024
VMEM_LIMIT = V7X_VMEM_BYTES - 4 * 1024 * 1024
NEG_INF = -1e30

U_CONV_B, U_CONV_C, U_CONV_X = 0, 1024, 2048
U_GLA_Q, U_GLA_K, U_GLA_V, U_GLA_R = 3072, 3584, 4096, 5120
U_COLS = 6144
UN_Q, UN_K, UN_V = 0, 1024, 2048
UN_COLS = 3072
W_IN_LR = U_COLS
W_IN_NA = W_IN_LR + 2 * GLA_RANK
W_IN_GATES = W_IN_NA + UN_COLS
LR_COLS = 128

NA_QROWS = 4
NA_KROWS = NA_QROWS + WIN_R
NA_QBLK = NA_QROWS * GRID_W
NA_KBLK = NA_KROWS * GRID_W
NA_KPAIRS = NA_KROWS // 2
NA_BIAS_DR = 2 * WIN_R - 1
NA_PAIR_BOTH = 0
NA_PAIR_LEFT = NA_PAIR_BOTH + NA_BIAS_DR - 1
NA_PAIR_RIGHT = NA_PAIR_LEFT + NA_BIAS_DR
NA_PAIR_NONE = NA_PAIR_RIGHT + NA_BIAS_DR
NA_PAIR_ENTRIES = NA_PAIR_NONE + 1
NA_NBLK = GRID_ROWS // NA_QROWS

MOE_GB = 256
MOE_SM = 9 * MOE_GB
MOE_CH = 4 * MOE_GB
MOE_TF = 256
MOE_NST = (NL * TOP_K) // MOE_SM + N_EXPERTS
MOE_NSLOT = MOE_NST * MOE_SM
MOE_NGB = MOE_NSLOT // MOE_GB
MOE_NTT = NL // MOE_GB
MOE_CT = 1024
MOE_NCT = NL // MOE_CT
MOE_NPAIR = (NL * TOP_K) // MOE_GB + N_EXPERTS + N_EXPERTS * (MOE_NCT - 1)
MOE_EPILOGUE_ROWS = 128
ROUTER_LANES = 128


def _params(*sem):
    return pltpu.CompilerParams(dimension_semantics=sem, vmem_limit_bytes=VMEM_LIMIT)


def _dot(a, b):
    return jnp.dot(a, b, preferred_element_type=F32)


def _dot_nt(a, b):
    return lax.dot_general(a, b, (((1,), (1,)), ((), ())), preferred_element_type=F32)


def _dot_tn(a, b):
    return lax.dot_general(a, b, (((0,), (0,)), ((), ())), preferred_element_type=F32)


def _silu(x):
    return x * jax.nn.sigmoid(x)


def _layer_norm(x, g, b):
    mu = jnp.mean(x, axis=-1, keepdims=True)
    xc = x - mu
    var = jnp.mean(xc * xc, axis=-1, keepdims=True)
    return xc * lax.rsqrt(var + LN_EPS) * g + b


def _mod_row_of_tile(tile_rows):
    per_batch = SEQ // tile_rows
    return lambda i: jnp.minimum(i // per_batch, BATCH)


def _mod_spec(layer, k, row_of_tile):
    return pl.BlockSpec((None, 1, D_MODEL),
                        lambda i, *_: (layer * 48 + row_of_tile(i) * 6 + k, 0, 0))


def _ada_kernel(c_ref, w_ref, b_ref, o_ref):
    a = _silu(c_ref[...]).astype(BF)
    o_ref[...] = _dot(a, w_ref[...].astype(BF)) + b_ref[...]


def _ada_table(cvec, ada_w, ada_b):
    tn = 1024
    out = pl.pallas_call(
        _ada_kernel,
        out_shape=jax.ShapeDtypeStruct((DEPTH, 8, 6 * D_MODEL), F32),
        grid=(DEPTH, 6 * D_MODEL // tn),
        in_specs=[pl.BlockSpec((8, D_MODEL), lambda l, j: (0, 0)),
                  pl.BlockSpec((None, D_MODEL, tn), lambda l, j: (l, 0, j)),
                  pl.BlockSpec((None, 1, tn), lambda l, j: (l, 0, j))],
        out_specs=pl.BlockSpec((None, 8, tn), lambda l, j: (l, 0, j)),
        compiler_params=_params("parallel", "parallel"),
        name="ada_table",
    )(cvec, ada_w, ada_b.reshape(DEPTH, 1, 6 * D_MODEL))
    return out.reshape(DEPTH * 8 * 6, 1, D_MODEL)


def _select_rows(i, n_lat_tiles, xl_ref, xc_ref):
    return jnp.where(i < n_lat_tiles, xl_ref[...], xc_ref[...])


def _split_row_specs(tm):
    n_lat = NL // tm
    return (pl.BlockSpec((tm, D_MODEL), lambda i: (jnp.minimum(i, n_lat - 1), 0)),
            pl.BlockSpec((tm, D_MODEL), lambda i: (jnp.maximum(i - n_lat, 0), 0)))


def _modulate_kernel(xl_ref, xc_ref, sh_ref, sc_ref, o_ref, *, n_lat_tiles):
    x = _select_rows(pl.program_id(0), n_lat_tiles, xl_ref, xc_ref)
    o_ref[...] = (x * (1.0 + sc_ref[...]) + sh_ref[...]).astype(BF)


def _modulate(x_lat, x_ctx, mod, layer):
    tm = 512
    rot = _mod_row_of_tile(tm)
    return pl.pallas_call(
        functools.partial(_modulate_kernel, n_lat_tiles=NL // tm),
        out_shape=jax.ShapeDtypeStruct((NT, D_MODEL), BF),
        grid=(NT // tm,),
        in_specs=[*_split_row_specs(tm), _mod_spec(layer, 0, rot), _mod_spec(layer, 1, rot)],
        out_specs=pl.BlockSpec((tm, D_MODEL), lambda i: (i, 0)),
        compiler_params=_params("parallel"),
        name="modulate",
    )(x_lat, x_ctx, mod, mod)


def _mm_wt_kernel(a_ref, wt_ref, o_ref, wb_ref, *, first_tile_scale):
    @pl.when(pl.program_id(1) == 0)
    def _():
        wb_ref[...] = wt_ref[0].T.astype(BF)

    acc = _dot(a_ref[...], wb_ref[...])
    if first_tile_scale is not None:
        acc = acc * jnp.where(pl.program_id(0) == 0, first_tile_scale, 1.0)
    o_ref[...] = acc.astype(o_ref.dtype)


def _project(a, wt_stack, layer, col0, n, tm, tn, name, out_dtype=F32, first_tile_scale=None):
    m, k = a.shape
    assert col0 % 8 == 0
    return pl.pallas_call(
        functools.partial(_mm_wt_kernel, first_tile_scale=first_tile_scale),
        out_shape=jax.ShapeDtypeStruct((m, n), out_dtype),
        grid=(n // tn, m // tm),
        in_specs=[pl.BlockSpec((tm, k), lambda j, i: (i, 0)),
                  pl.BlockSpec((pl.Element(1), pl.Element(tn), pl.Element(k)),
                               lambda j, i: (layer, pl.multiple_of(col0 + j * tn, 8), 0))],
        out_specs=pl.BlockSpec((tm, tn), lambda j, i: (i, j)),
        scratch_shapes=[pltpu.VMEM((k, tn), BF)],
        compiler_params=_params("arbitrary", "arbitrary"),
        name=name,
    )(a, wt_stack)


CONV_TM = 256


def _conv_kernel(b_ref, c_ref, x_ref, cp_ref, xp_ref, cn_ref, xn_ref, w_ref, o_ref):
    i = pl.program_id(0)
    tiles_per_seq = SEQ // CONV_TM
    is_ctx = i >= NL // CONV_TM
    is_start = jnp.logical_or(i % tiles_per_seq == 0, is_ctx)
    is_end = jnp.logical_or(i % tiles_per_seq == tiles_per_seq - 1, is_ctx)
    z = c_ref[...] * x_ref[...]
    zp = cp_ref[...] * xp_ref[...]
    zn = cn_ref[...] * xn_ref[...]
    prev_row = jnp.where(is_start, 0.0, zp[7:8, :])
    next_row = jnp.where(is_end, 0.0, zn[0:1, :])
    rows = lax.broadcasted_iota(jnp.int32, z.shape, 0)
    z_prev = jnp.where(rows == 0, prev_row, pltpu.roll(z, 1, 0))
    z_next = jnp.where(rows == CONV_TM - 1, next_row, pltpu.roll(z, CONV_TM - 1, 0))
    w = w_ref[...]
    y = w[0:1, :] * z_prev + w[1:2, :] * z + w[2:3, :] * z_next
    o_ref[...] = (b_ref[...] * y).astype(BF)


def _short_conv(u, conv_w, layer):
    tm = CONV_TM
    n8 = tm // 8
    last8 = NT // 8 - 1
    return pl.pallas_call(
        _conv_kernel,
        out_shape=jax.ShapeDtypeStruct((NT, CONV_DIM), BF),
        grid=(NT // tm,),
        in_specs=[pl.BlockSpec((tm, CONV_DIM), lambda i: (i, U_CONV_B // CONV_DIM)),
                  pl.BlockSpec((tm, CONV_DIM), lambda i: (i, U_CONV_C // CONV_DIM)),
                  pl.BlockSpec((tm, CONV_DIM), lambda i: (i, U_CONV_X // CONV_DIM)),
                  pl.BlockSpec((8, CONV_DIM), lambda i: (jnp.maximum(i * n8 - 1, 0), U_CONV_C // CONV_DIM)),
                  pl.BlockSpec((8, CONV_DIM), lambda i: (jnp.maximum(i * n8 - 1, 0), U_CONV_X // CONV_DIM)),
                  pl.BlockSpec((8, CONV_DIM), lambda i: (jnp.minimum(i * n8 + n8, last8), U_CONV_C // CONV_DIM)),
                  pl.BlockSpec((8, CONV_DIM), lambda i: (jnp.minimum(i * n8 + n8, last8), U_CONV_X // CONV_DIM)),
                  pl.BlockSpec((None, CONV_K, CONV_DIM), lambda i: (layer, 0, 0))],
        out_specs=pl.BlockSpec((tm, CONV_DIM), lambda i: (i, 0)),
        compiler_params=_params("parallel"),
        name="short_conv",
    )(u, u, u, u, u, u, u, conv_w)


GLA_STEPS = (CTX_LEN + SEQ) // GLA_CHUNK
GLA_CTX_STEPS = CTX_LEN // GLA_CHUNK


def _log_sigmoid(z):
    return -(jnp.maximum(-z, 0.0) + jnp.log1p(jnp.exp(-jnp.abs(z))))


def _rope(x, cs, sn):
    lane = lax.broadcasted_iota(jnp.int32, x.shape, 1)
    swapped = jnp.where(lane % 2 == 0, pltpu.roll(x, GLA_DK - 1, 1), pltpu.roll(x, 1, 1))
    return x * cs + swapped * sn


def _gla_decayed_operands(q_ref, k_ref, l_ref, r_ref, w_ref, b_ref, reverse):
    c = GLA_CHUNK
    z = _dot(l_ref[...].astype(BF), w_ref[...]) + b_ref[...]
    g = _log_sigmoid(z) * (1.0 / GLA_TAU)
    t_i = lax.broadcasted_iota(jnp.int32, (c, c), 0)
    s_i = lax.broadcasted_iota(jnp.int32, (c, c), 1)
    keep = (s_i >= t_i) if reverse else (s_i <= t_i)
    tri = jnp.where(keep, 1.0, 0.0).astype(BF)
    g1 = g.astype(BF)
    r1 = g - g1.astype(F32)
    g2 = r1.astype(BF)
    g3 = (r1 - g2.astype(F32)).astype(BF)
    b = _dot(tri, g1) + _dot(tri, g2) + _dot(tri, g3)
    total = b[0:1, :] if reverse else b[c - 1:c, :]
    cs = r_ref[:, :GLA_DK]
    sn = r_ref[:, GLA_DK:]
    heads = [slice(h * GLA_DK, (h + 1) * GLA_DK) for h in range(GLA_HEADS)]
    qs = jnp.concatenate([_rope(q_ref[:, hd] * (GLA_DK ** -0.5), cs, sn) for hd in heads], axis=1)
    kr = jnp.concatenate([_rope(k_ref[:, hd], cs, sn) for hd in heads], axis=1)
    q_dec = (qs * jnp.exp(b)).astype(BF)
    k_inv = (kr * jnp.exp(-b)).astype(BF)
    k_end = (kr * jnp.exp(total - b)).astype(BF)
    return keep, q_dec, k_inv, k_end, jnp.exp(total)


def _gla_kernel(qf_ref, kf_ref, vf_ref, lf_ref, rf_ref, qb_ref, kb_ref, vb_ref, lb_ref, rb_ref,
                wf_ref, bf_ref, wb_ref, bb_ref, of_ref, ob_ref, stf_ref, stb_ref):
    @pl.when(pl.program_id(1) == 0)
    def _():
        stf_ref[...] = jnp.zeros_like(stf_ref)
        stb_ref[...] = jnp.zeros_like(stb_ref)

    pre = [_gla_decayed_operands(qf_ref, kf_ref, lf_ref, rf_ref, wf_ref, bf_ref, False),
           _gla_decayed_operands(qb_ref, kb_ref, lb_ref, rb_ref, wb_ref, bb_ref, True)]
    chains = [(d, h) for d in range(2) for h in range(GLA_HEADS)]
    v_refs, o_refs, st_refs = (vf_ref, vb_ref), (of_ref, ob_ref), (stf_ref, stb_ref)

    def head(x, h):
        return x[:, h * GLA_DK:(h + 1) * GLA_DK]

    v = [v_refs[d][:, h * GLA_DV:(h + 1) * GLA_DV].astype(BF) for d, h in chains]
    st = [st_refs[d][h] for d, h in chains]
    att = [jnp.where(pre[d][0], _dot_nt(head(pre[d][1], h), head(pre[d][2], h)), 0.0).astype(BF)
           for d, h in chains]
    o = [_dot(att[i], v[i]) + _dot_nt(head(pre[d][1], h), st[i].astype(BF)) for i, (d, h) in enumerate(chains)]
    for d in range(2):
        o_refs[d][...] = jnp.concatenate(o[d * GLA_HEADS:(d + 1) * GLA_HEADS], axis=1)
    for i, (d, h) in enumerate(chains):
        st_refs[d][h] = st[i] * head(pre[d][4], h) + _dot_tn(v[i], head(pre[d][3], h))


def _gla(u, lr, rope_tab, wpad_f, bg_f, wpad_b, bg_b):
    c = GLA_CHUNK
    lat_chunks = SEQ // c
    ctx_base = NL // c

    def row_f(b, s):
        return jnp.where(s < GLA_CTX_STEPS, ctx_base + b * GLA_CTX_STEPS + s, b * lat_chunks + s - GLA_CTX_STEPS)

    def row_b(b, s):
        return jnp.where(s < GLA_CTX_STEPS, ctx_base + b * GLA_CTX_STEPS + GLA_CTX_STEPS - 1 - s,
                         b * lat_chunks + GLA_STEPS - 1 - s)

    def rope_f(s):
        return jnp.where(s < GLA_CTX_STEPS, lat_chunks, s - GLA_CTX_STEPS)

    def rope_b(s):
        return jnp.where(s < GLA_CTX_STEPS, lat_chunks, GLA_STEPS - 1 - s)

    def stream(row, rope):
        return [pl.BlockSpec((c, GLA_QK), lambda b, s: (row(b, s), U_GLA_Q // GLA_QK)),
                pl.BlockSpec((c, GLA_QK), lambda b, s: (row(b, s), U_GLA_K // GLA_QK)),
                pl.BlockSpec((c, GLA_V), lambda b, s: (row(b, s), U_GLA_V // GLA_V)),
                pl.BlockSpec((c, LR_COLS), lambda b, s: (row(b, s), 0)),
                pl.BlockSpec((c, 2 * GLA_DK), lambda b, s: (rope(s), 0))]

    head_w = pl.BlockSpec((LR_COLS, GLA_QK), lambda b, s: (0, 0))
    head_b = pl.BlockSpec((1, GLA_QK), lambda b, s: (0, 0))
    state = pltpu.VMEM((GLA_HEADS, GLA_DV, GLA_DK), F32)
    return pl.pallas_call(
        _gla_kernel,
        out_shape=(jax.ShapeDtypeStruct((NT, GLA_V), F32), jax.ShapeDtypeStruct((NT, GLA_V), F32)),
        grid=(BATCH, GLA_STEPS),
        in_specs=stream(row_f, rope_f) + stream(row_b, rope_b) + [head_w, head_b, head_w, head_b],
        out_specs=(pl.BlockSpec((c, GLA_V), lambda b, s: (row_f(b, s), 0)),
                   pl.BlockSpec((c, GLA_V), lambda b, s: (row_b(b, s), 0))),
        scratch_shapes=[state, state],
        compiler_params=_params("parallel", "arbitrary"),
        name="gla",
    )(u, u, u, lr, rope_tab, u, u, u, lr, rope_tab, wpad_f, bg_f, wpad_b, bg_b)


def _rope_table():
    t = jnp.arange(SEQ)
    rows = (t // GRID_W).astype(F32)
    cols = (t % GRID_W).astype(F32)
    n_freq = GLA_DK // 4
    inv = 1.0 / (ROPE_BASE ** (jnp.arange(n_freq, dtype=F32) / n_freq))
    ang = jnp.concatenate([rows[:, None] * inv, cols[:, None] * inv], -1)
    cos = jnp.repeat(jnp.cos(ang), 2, axis=-1)
    sin = jnp.repeat(jnp.sin(ang), 2, axis=-1)
    sign = jnp.tile(jnp.array([-1.0, 1.0], F32), GLA_DK // 2)
    cos = jnp.concatenate([cos, jnp.ones((GRID_W, GLA_DK), F32)], 0)
    sin = jnp.concatenate([sin * sign, jnp.zeros((GRID_W, GLA_DK), F32)], 0)
    return jnp.concatenate([cos, sin], axis=1)


def _decay_weights(wg, bg, lane_offset):
    w = jnp.pad(wg, ((lane_offset, LR_COLS - GLA_RANK - lane_offset), (0, 0))).astype(BF)
    return w, bg.reshape(1, GLA_QK)


NA_HPS = 4


def _na_softmax_out(parts):
    m = functools.reduce(jnp.maximum, [jnp.max(s, axis=-1, keepdims=True) for s, _ in parts])
    ps = [jnp.exp(s - m) for s, _ in parts]
    l = functools.reduce(jnp.add, [jnp.sum(p, axis=-1, keepdims=True) for p in ps])
    o = functools.reduce(jnp.add, [_dot(p.astype(BF), v) for p, (_, v) in zip(ps, parts)])
    return (o / l).astype(BF)


def _na_kernel(q_ref, k_ref, v_ref, kc_ref, vc_ref, bias_ref, o_ref):
    j = pl.program_id(2)
    head_cols = [slice(h * NA_DH, (h + 1) * NA_DH) for h in range(NA_HPS)]

    @pl.when(j < NA_NBLK)
    def _():
        base = jnp.clip(NA_QROWS * j - WIN_R // 2, 0, GRID_ROWS - NA_KROWS)
        start = pl.multiple_of(base * GRID_W, GRID_W)

        def pair_entry(qr, kp):
            r = NA_QROWS * j + qr
            rs = jnp.clip(r - WIN_R // 2, 0, GRID_ROWS - WIN_R)
            a0 = base + 2 * kp
            dr0 = a0 - r + WIN_R - 1
            in0 = jnp.logical_and(a0 >= rs, a0 < rs + WIN_R)
            in1 = jnp.logical_and(a0 + 1 >= rs, a0 + 1 < rs + WIN_R)
            entry = jnp.where(jnp.logical_and(in0, in1), NA_PAIR_BOTH + dr0,
                              jnp.where(in0, NA_PAIR_LEFT + dr0,
                                        jnp.where(in1, NA_PAIR_RIGHT + dr0 + 1, NA_PAIR_NONE)))
            return jnp.clip(entry, 0, NA_PAIR_ENTRIES - 1)

        entries = [[pair_entry(qr, kp) for kp in range(NA_KPAIRS)] for qr in range(NA_QROWS)]
        for h, hc in enumerate(head_cols):
            q = q_ref[:, hc]
            vl = v_ref[pl.ds(start, NA_KBLK), hc]
            bias = jnp.concatenate(
                [jnp.concatenate([bias_ref[h, entries[qr][kp]] for kp in range(NA_KPAIRS)], axis=1)
                 for qr in range(NA_QROWS)], axis=0)
            s_loc = _dot_nt(q, k_ref[pl.ds(start, NA_KBLK), hc]) + bias
            s_ctx = _dot_nt(q, kc_ref[:, hc])
            o_ref[:, hc] = _na_softmax_out([(s_loc, vl), (s_ctx, vc_ref[:, hc])])

    @pl.when(j == NA_NBLK)
    def _():
        for hc in head_cols:
            s_ctx = _dot_nt(q_ref[:, hc], kc_ref[:, hc])
            o_ref[:, hc] = _na_softmax_out([(s_ctx, vc_ref[:, hc])])


def _na_bias_pair_tables(rpb_all):
    qc = np.arange(GRID_W)
    cs = np.clip(qc - WIN_C // 2, 0, GRID_W - WIN_C)
    kc = np.arange(GRID_W)
    col_ok = (kc[None, :] >= cs[:, None]) & (kc[None, :] < cs[:, None] + WIN_C)
    dc = kc[None, :] - qc[:, None] + WIN_C - 1
    col_sel = ((dc[None] == np.arange(2 * WIN_C - 1)[:, None, None]) & col_ok[None]).astype(np.float32)
    t = jnp.einsum('lhrd,dxy->lhrxy', rpb_all, jnp.asarray(col_sel), precision=lax.Precision.HIGHEST)
    t = jnp.where(jnp.asarray(col_ok), t, NEG_INF)
    masked = jnp.full_like(t, NEG_INF)
    both = jnp.concatenate([t[:, :, :-1], t[:, :, 1:]], axis=-1)
    left = jnp.concatenate([t, masked], axis=-1)
    right = jnp.concatenate([masked, t], axis=-1)
    none = jnp.concatenate([masked[:, :, :1], masked[:, :, :1]], axis=-1)
    return jnp.concatenate([both, left, right, none], axis=2)


def _neighbourhood_attention(un, bias_pairs, layer, with_ctx):
    steps = NA_NBLK + (1 if with_ctx else 0)
    ctx_q = NL // NA_QBLK
    w = NA_HPS * NA_DH

    def q_row(b, j):
        return jnp.where(j < NA_NBLK, b * NA_NBLK + j, ctx_q + b)

    ctx_blk = NL // CTX_LEN
    return pl.pallas_call(
        _na_kernel,
        out_shape=jax.ShapeDtypeStruct((NT, NA_W), BF),
        grid=(BATCH, NA_HEADS // NA_HPS, steps),
        in_specs=[pl.BlockSpec((NA_QBLK, w), lambda b, h, j: (q_row(b, j), UN_Q // w + h)),
                  pl.BlockSpec((SEQ, w), lambda b, h, j: (b, UN_K // w + h)),
                  pl.BlockSpec((SEQ, w), lambda b, h, j: (b, UN_V // w + h)),
                  pl.BlockSpec((CTX_LEN, w), lambda b, h, j: (ctx_blk + b, UN_K // w + h)),
                  pl.BlockSpec((CTX_LEN, w), lambda b, h, j: (ctx_blk + b, UN_V // w + h)),
                  pl.BlockSpec((None, NA_HPS, NA_PAIR_ENTRIES, GRID_W, 2 * GRID_W),
                               lambda b, h, j: (layer, h, 0, 0, 0))],
        out_specs=pl.BlockSpec((NA_QBLK, w), lambda b, h, j: (q_row(b, j), h)),
        compiler_params=_params("parallel", "parallel", "arbitrary"),
        name="neighbourhood_attention",
    )(un, un, un, un, un, bias_pairs)


MERGE_TM = 256


def _merge_kernel(yc_ref, of_ref, ob_ref, r_ref, yn_ref, gt_ref, xl_ref, xc_ref,
                  ng_ref, gb_ref, wc_ref, wg_ref, wn_ref, wo_ref,
                  g1_ref, lg_ref, lb_ref, sh2_ref, sc2_ref, xo_ref, ho_ref):
    x = _select_rows(pl.program_id(0), NL // MERGE_TM, xl_ref, xc_ref)
    o = of_ref[...] + ob_ref[...]
    r = r_ref[...]
    ng = ng_ref[...]
    heads = []
    for h in range(GLA_HEADS):
        oh = o[:, h * GLA_DV:(h + 1) * GLA_DV]
        oh = oh * lax.rsqrt(jnp.mean(oh * oh, axis=-1, keepdims=True) + LN_EPS)
        heads.append(oh * ng * _silu(r[:, h * GLA_DV:(h + 1) * GLA_DV]))
    y_gla = jnp.concatenate(heads, axis=-1).astype(BF)
    g = jax.nn.sigmoid(gt_ref[...] + gb_ref[...])
    z = (g[:, :D_MODEL] * _dot(yc_ref[...], wc_ref[...])
         + g[:, D_MODEL:2 * D_MODEL] * _dot(y_gla, wg_ref[...])
         + g[:, 2 * D_MODEL:] * _dot(yn_ref[...], wn_ref[...]))
    y = _dot(z.astype(BF), wo_ref[...])
    xn = _layer_norm(DEEPNORM_ALPHA * x + g1_ref[...] * y, lg_ref[...], lb_ref[...])
    xo_ref[...] = xn
    ho_ref[...] = (xn * (1.0 + sc2_ref[...]) + sh2_ref[...]).astype(BF)


def _merge(y_conv, o_f, o_b, u, ug, y_na, x_lat, x_ctx, mod, layer, rows, norm_g, gate_b, w_conv, w_gla, w_na,
           w_out, ln_g, ln_b):
    tm = MERGE_TM
    rot = _mod_row_of_tile(tm)
    row = lambda width: pl.BlockSpec((tm, width), lambda i: (i, 0))
    const = lambda shape: pl.BlockSpec(shape, lambda i: (0,) * len(shape), pipeline_mode=pl.Buffered(1))
    return pl.pallas_call(
        _merge_kernel,
        out_shape=(jax.ShapeDtypeStruct((rows, D_MODEL), F32), jax.ShapeDtypeStruct((rows, D_MODEL), BF)),
        grid=(rows // tm,),
        in_specs=[row(CONV_DIM), row(GLA_V), row(GLA_V),
                  pl.BlockSpec((tm, GLA_V), lambda i: (i, U_GLA_R // GLA_V)),
                  row(NA_W), row(N_BRANCH * D_MODEL), *_split_row_specs(tm),
                  const((1, GLA_DV)), const((1, N_BRANCH * D_MODEL)),
                  const((CONV_DIM, D_MODEL)), const((GLA_V, D_MODEL)), const((NA_W, D_MODEL)),
                  const((D_MODEL, D_MODEL)),
                  _mod_spec(layer, 2, rot), const((1, D_MODEL)), const((1, D_MODEL)),
                  _mod_spec(layer, 3, rot), _mod_spec(layer, 4, rot)],
        out_specs=(row(D_MODEL), row(D_MODEL)),
        compiler_params=_params("parallel"),
        name="merge",
    )(y_conv, o_f, o_b, u, y_na, ug, x_lat, x_ctx, norm_g, gate_b, w_conv, w_gla, w_na, w_out,
      mod, ln_g, ln_b, mod, mod)


FFN_TM_LATENT = 1024
FFN_TM_CONTEXT = NC
FFN_TF = 256
FFN_EPILOGUE_ROWS = 128


def _ffn_kernel(h_ref, wg_ref, wu_ref, wd_ref, x_ref, g2_ref, lg_ref, lb_ref, shn_ref, scn_ref, *refs):
    xo_ref, ho_ref, acc_ref = refs[-3:]
    j = pl.program_id(1)

    @pl.when(j == 0)
    def _():
        acc_ref[...] = jnp.zeros_like(acc_ref)

    h = h_ref[...]
    a = (_silu(_dot(h, wg_ref[...].astype(BF))) * _dot(h, wu_ref[...].astype(BF))).astype(BF)
    acc_ref[...] += _dot(a, wd_ref[...].astype(BF))

    @pl.when(j == pl.num_programs(1) - 1)
    def _():
        def norm_rows(c, carry):
            rows = pl.ds(pl.multiple_of(c * FFN_EPILOGUE_ROWS, FFN_EPILOGUE_ROWS), FFN_EPILOGUE_ROWS)
            xn = _layer_norm(DEEPNORM_ALPHA * x_ref[rows, :] + g2_ref[...] * acc_ref[rows, :],
                             lg_ref[...], lb_ref[...])
            xo_ref[rows, :] = xn
            ho_ref[rows, :] = (xn * (1.0 + scn_ref[...]) + shn_ref[...]).astype(BF)
            return carry

        lax.fori_loop(0, acc_ref.shape[0] // FFN_EPILOGUE_ROWS, norm_rows, 0)


def _dense_ffn_rows(h2, x_all, mod, layer, wg, wu, wd, ffn_layer, ln_g, ln_b, row0, n_rows, tm, prev):
    tile0 = row0 // tm
    rot = _mod_row_of_tile(tm)
    mod_spec = lambda lyr, k: pl.BlockSpec((None, 1, D_MODEL),
                                           lambda i, j: (lyr * 48 + rot(tile0 + i) * 6 + k, 0, 0))
    row = lambda **kw: pl.BlockSpec((tm, D_MODEL), lambda i, j: (tile0 + i, 0), **kw)
    const = lambda: pl.BlockSpec((1, D_MODEL), lambda i, j: (0, 0))
    once = dict(pipeline_mode=pl.Buffered(1))
    in_specs = [row(),
                pl.BlockSpec((None, D_MODEL, FFN_TF), lambda i, j: (ffn_layer, 0, j)),
                pl.BlockSpec((None, D_MODEL, FFN_TF), lambda i, j: (ffn_layer, 0, j)),
                pl.BlockSpec((None, FFN_TF, D_MODEL), lambda i, j: (ffn_layer, j, 0)),
                row(**once), mod_spec(layer, 5), const(), const(),
                mod_spec(layer + 1, 0), mod_spec(layer + 1, 1)]
    args = [h2, wg, wu, wd, x_all, mod, ln_g, ln_b, mod, mod]
    aliases = {}
    if prev is not None:
        aliases = {len(args): 0, len(args) + 1: 1}
        in_specs += [pl.BlockSpec(memory_space=pl.ANY), pl.BlockSpec(memory_space=pl.ANY)]
        args += list(prev)
    return pl.pallas_call(
        _ffn_kernel,
        out_shape=(jax.ShapeDtypeStruct((NT, D_MODEL), F32), jax.ShapeDtypeStruct((NT, D_MODEL), BF)),
        grid=(n_rows // tm, D_FF // FFN_TF),
        in_specs=in_specs,
        out_specs=(row(**once), row(**once)),
        scratch_shapes=[pltpu.VMEM((tm, D_MODEL), F32)],
        input_output_aliases=aliases,
        compiler_params=_params("parallel", "arbitrary"),
        name="dense_ffn",
    )(*args)


def _dense_ffn(h2, x_all, mod, layer, wg, wu, wd, ffn_layer, ln_g, ln_b):
    out = _dense_ffn_rows(h2, x_all, mod, layer, wg, wu, wd, ffn_layer, ln_g, ln_b, 0, NL, FFN_TM_LATENT, None)
    return _dense_ffn_rows(h2, x_all, mod, layer, wg, wu, wd, ffn_layer, ln_g, ln_b, NL, NC, FFN_TM_CONTEXT, out)


def _router_kernel(h_ref, w_ref, idx_ref, wt_ref):
    logits = _dot(h_ref[...], w_ref[...])
    lane = lax.broadcasted_iota(jnp.int32, logits.shape, 1)
    logits = jnp.where(lane < N_EXPERTS, logits, -jnp.inf)
    m1 = jnp.max(logits, axis=-1, keepdims=True)
    i1 = jnp.min(jnp.where(logits == m1, lane, ROUTER_LANES), axis=-1, keepdims=True)
    rest = jnp.where(lane == i1, -jnp.inf, logits)
    m2 = jnp.max(rest, axis=-1, keepdims=True)
    i2 = jnp.min(jnp.where(rest == m2, lane, ROUTER_LANES), axis=-1, keepdims=True)
    e2 = jnp.exp(m2 - m1)
    w1 = 1.0 / (1.0 + e2)
    w2 = e2 / (1.0 + e2)
    idx_ref[...] = jnp.where(lane == 0, i1, jnp.where(lane == 1, i2, 0))
    wt_ref[...] = jnp.where(lane == 0, w1, jnp.where(lane == 1, w2, 0.0))


def _router(h2, router_w):
    tm = 512
    return pl.pallas_call(
        _router_kernel,
        out_shape=(jax.ShapeDtypeStruct((NL, ROUTER_LANES), jnp.int32),
                   jax.ShapeDtypeStruct((NL, ROUTER_LANES), F32)),
        grid=(NL // tm,),
        in_specs=[pl.BlockSpec((tm, D_MODEL), lambda i: (i, 0)),
                  pl.BlockSpec((D_MODEL, ROUTER_LANES), lambda i: (0, 0))],
        out_specs=(pl.BlockSpec((tm, ROUTER_LANES), lambda i: (i, 0)),
                   pl.BlockSpec((tm, ROUTER_LANES), lambda i: (i, 0))),
        compiler_params=_params("parallel"),
        name="moe_router",
    )(h2, router_w)


def _dispatch_plan(idx):
    i32 = jnp.int32
    experts = jnp.arange(N_EXPERTS, dtype=i32)
    oh_a = (idx[:, 0:1] == experts).astype(i32)
    oh_b = (idx[:, 1:2] == experts).astype(i32)
    cum = jnp.cumsum(oh_a + oh_b, axis=0)
    counts = cum[-1]
    n_st = (counts + MOE_SM - 1) // MOE_SM
    st_end = jnp.cumsum(n_st)
    st_start = st_end - n_st
    g_start = st_start * MOE_SM
    g_end = st_end * MOE_SM
    slot_table = g_start[None, :] + cum - 1
    slot_a = jnp.sum(oh_a * slot_table, axis=1)
    slot_b = jnp.sum(oh_b * slot_table, axis=1)

    n_used = st_end[-1]
    st_ids = jnp.minimum(jnp.arange(MOE_NST, dtype=i32), n_used - 1)
    st_expert = jnp.minimum(jnp.sum((st_ids[:, None] >= st_end[None, :]).astype(i32), axis=1), N_EXPERTS - 1)
    st_left = counts[st_expert] - (st_ids - st_start[st_expert]) * MOE_SM
    st_rows = ((jnp.clip(st_left, 0, MOE_SM) + MOE_GB - 1) // MOE_GB) * MOE_GB
    tile_plan = (st_expert, st_rows, n_used.reshape(1))

    pair_ids = jnp.arange(MOE_NPAIR, dtype=i32)
    tile_last = jnp.arange(MOE_NTT, dtype=i32) * MOE_GB + MOE_GB - 1
    hi = cum[tile_last]
    lo = jnp.concatenate([jnp.zeros((1, N_EXPERTS), i32), hi[:-1]], axis=0)

    blk_ids = jnp.arange(MOE_NGB, dtype=i32)
    blk_start = blk_ids * MOE_GB
    blk_e = jnp.minimum(jnp.sum((blk_start[:, None] >= g_end[None, :]).astype(i32), axis=1), N_EXPERTS - 1)
    r0 = blk_start - g_start[blk_e]
    r1 = jnp.minimum(r0 + MOE_GB, counts[blk_e])
    has = (blk_start < g_end[-1]) & (r1 > r0)
    hi_e = hi[:, blk_e]
    t_lo = jnp.sum((hi_e <= r0[None, :]).astype(i32), axis=0)
    t_hi = jnp.sum((hi_e < r1[None, :]).astype(i32), axis=0)
    t_lo = jnp.where(has, t_lo, 0)
    t_end = jnp.where(has, t_hi + 1, 0)
    blk_map = lax.cummax(jnp.where(has, blk_ids, 0), axis=0)
    dispatch = (t_lo, t_end, blk_map, has.astype(i32))

    c_hi = cum[jnp.arange(MOE_NCT, dtype=i32) * MOE_CT + MOE_CT - 1]
    c_lo = jnp.concatenate([jnp.zeros((1, N_EXPERTS), i32), c_hi[:-1]], axis=0)
    b_lo = (g_start[None, :] + c_lo) // MOE_GB
    b_hi = (g_start[None, :] + c_hi - 1) // MOE_GB
    n_c = jnp.where(c_hi > c_lo, b_hi - b_lo + 1, 0).reshape(-1)
    b_lo = b_lo.reshape(-1)
    c_end = jnp.cumsum(n_c)
    c_total = c_end[-1]
    c_idx = jnp.minimum(jnp.sum((pair_ids[:, None] >= c_end[None, :]).astype(i32), axis=1),
                        MOE_NCT * N_EXPERTS - 1)
    c_blk = jnp.clip(b_lo[c_idx] + pair_ids - (c_end[c_idx] - n_c[c_idx]), 0, MOE_NGB - 1)
    c_tile = c_idx // N_EXPERTS
    c_valid = pair_ids < c_total
    c_tile = jnp.where(c_valid, c_tile, MOE_NCT - 1)
    c_blk = jnp.where(c_valid, c_blk, c_blk[jnp.maximum(c_total - 1, 0)])
    c_prev = jnp.concatenate([jnp.full((1,), -1, i32), c_tile[:-1]])
    c_next = jnp.concatenate([c_tile[1:], jnp.full((1,), -1, i32)])
    c_first = c_valid & (c_tile != c_prev)
    c_last = c_valid & ((c_tile != c_next) | (pair_ids == c_total - 1))
    combine = (c_tile, c_blk, c_first.astype(i32), c_last.astype(i32), c_valid.astype(i32))
    return slot_a, slot_b, tile_plan, dispatch, combine


def _dispatch_kernel(lo_ref, end_ref, map_ref, has_ref, h_ref, sa_ref, sb_ref, wa_ref, wb_ref, hs_ref, ws_ref,
                     acc_ref, wacc_ref):
    blk = pl.program_id(0)

    @pl.when(has_ref[blk] == 1)
    def _():
        acc_ref[...] = jnp.zeros_like(acc_ref)
        wacc_ref[...] = jnp.zeros_like(wacc_ref)
        slots = blk * MOE_GB + lax.broadcasted_iota(jnp.int32, (MOE_GB, MOE_GB), 0)

        def scan_tile(t, carry):
            rows = pl.ds(pl.multiple_of(t * MOE_GB, MOE_GB), MOE_GB)
            hit_a = slots == sa_ref[t]
            hit_b = slots == sb_ref[t]
            onehot = jnp.where(hit_a, 1.0, jnp.where(hit_b, 1.0, 0.0)).astype(BF)
            acc_ref[...] += _dot(onehot, h_ref[rows, :])
            w = jnp.where(hit_a, wa_ref[t], jnp.where(hit_b, wb_ref[t], 0.0))
            wacc_ref[...] += jnp.broadcast_to(jnp.sum(w, axis=1, keepdims=True), wacc_ref.shape)
            return carry

        lax.fori_loop(lo_ref[blk], end_ref[blk], scan_tile, 0)
        hs_ref[...] = acc_ref[...].astype(BF)
        ws_ref[...] = wacc_ref[...]


def _dispatch(h2, slot_a, slot_b, w_a, w_b, plan):
    used = lambda i, lo, end, blk_map, has: (blk_map[i], 0)
    resident = lambda shape: pl.BlockSpec(shape, lambda i, *_: (0,) * len(shape), pipeline_mode=pl.Buffered(1))
    shape3 = (MOE_NTT, 1, MOE_GB)
    grid_spec = pltpu.PrefetchScalarGridSpec(
        num_scalar_prefetch=4,
        grid=(MOE_NGB,),
        in_specs=[resident((NL, D_MODEL)), resident(shape3), resident(shape3), resident(shape3),
                  resident(shape3)],
        out_specs=(pl.BlockSpec((MOE_GB, D_MODEL), used), pl.BlockSpec((MOE_GB, ROUTER_LANES), used)),
        scratch_shapes=[pltpu.VMEM((MOE_GB, D_MODEL), F32), pltpu.VMEM((MOE_GB, ROUTER_LANES), F32)])
    return pl.pallas_call(
        _dispatch_kernel,
        out_shape=(jax.ShapeDtypeStruct((MOE_NSLOT, D_MODEL), BF),
                   jax.ShapeDtypeStruct((MOE_NSLOT, ROUTER_LANES), F32)),
        grid_spec=grid_spec,
        compiler_params=_params("arbitrary"),
        name="moe_dispatch",
    )(*plan, h2, slot_a.reshape(shape3), slot_b.reshape(shape3), w_a.reshape(shape3), w_b.reshape(shape3))


def _expert_kernel(e_ref, rows_ref, n_ref, hs_ref, wg_ref, wu_ref, wd_ref, ws_ref, y_ref, acc_ref):
    s = pl.program_id(0)
    j = pl.program_id(1)
    last = pl.num_programs(1) - 1

    @pl.when(s < n_ref[0])
    def _():
        n_rows = rows_ref[s]

        def for_row_chunks(fn):
            n_full = n_rows // MOE_CH

            def full_chunk(c, carry):
                fn(pl.ds(pl.multiple_of(c * MOE_CH, MOE_CH), MOE_CH))
                return carry

            lax.fori_loop(0, n_full, full_chunk, 0)
            rest = n_rows - n_full * MOE_CH
            half = MOE_CH // 2

            @pl.when(rest >= half)
            def _():
                fn(pl.ds(pl.multiple_of(n_full * MOE_CH, MOE_GB), half))

            @pl.when(rest % half != 0)
            def _():
                fn(pl.ds(pl.multiple_of(n_rows - MOE_GB, MOE_GB), MOE_GB))

        @pl.when(j == 0)
        def _():
            acc_ref[...] = jnp.zeros_like(acc_ref)

        def accumulate(rows):
            h = hs_ref[rows, :]
            a = (_silu(_dot(h, wg_ref[...].astype(BF))) * _dot(h, wu_ref[...].astype(BF))).astype(BF)
            acc_ref[rows, :] += _dot(a, wd_ref[...].astype(BF))

        for_row_chunks(accumulate)

        @pl.when(j == last)
        def _():
            def emit(rows):
                y_ref[rows, :] = (acc_ref[rows, :] * ws_ref[rows, 0:1]).astype(BF)

            for_row_chunks(emit)


def _experts(hs, tile_plan, wg, wu, wd, ws, moe_layer):
    n_ff = D_FF // MOE_TF

    def used(s, n):
        return jnp.minimum(s, n[0] - 1)

    def ff_tile(s, j, n):
        return jnp.where(s < n[0], j, n_ff - 1)

    rows_spec = lambda width, **kw: pl.BlockSpec((MOE_SM, width), lambda s, j, e, r, n: (used(s, n), 0), **kw)
    up_spec = pl.BlockSpec((None, None, D_MODEL, MOE_TF),
                           lambda s, j, e, r, n: (moe_layer, e[s], 0, ff_tile(s, j, n)))
    grid_spec = pltpu.PrefetchScalarGridSpec(
        num_scalar_prefetch=3,
        grid=(MOE_NST, n_ff),
        in_specs=[rows_spec(D_MODEL, pipeline_mode=pl.Buffered(1)), up_spec, up_spec,
                  pl.BlockSpec((None, None, MOE_TF, D_MODEL),
                               lambda s, j, e, r, n: (moe_layer, e[s], ff_tile(s, j, n), 0)),
                  rows_spec(ROUTER_LANES)],
        out_specs=rows_spec(D_MODEL, pipeline_mode=pl.Buffered(1)),
        scratch_shapes=[pltpu.VMEM((MOE_SM, D_MODEL), F32)])
    return pl.pallas_call(
        _expert_kernel,
        out_shape=jax.ShapeDtypeStruct((MOE_NSLOT, D_MODEL), BF),
        grid_spec=grid_spec,
        compiler_params=_params("arbitrary", "arbitrary"),
        name="moe_experts",
    )(*tile_plan, hs, wg, wu, wd, ws)


def _combine_kernel(tile_ref, blk_ref, first_ref, last_ref, valid_ref,
                    y_ref, sa_ref, sb_ref, x_ref, g2_ref, lg_ref, lb_ref, o_ref, acc_ref):
    p = pl.program_id(0)

    @pl.when(first_ref[p] == 1)
    def _():
        acc_ref[...] = jnp.zeros_like(acc_ref)

    @pl.when(valid_ref[p] == 1)
    def _():
        slots = blk_ref[p] * MOE_GB + lax.broadcasted_iota(jnp.int32, (MOE_CT, MOE_GB), 1)
        onehot = jnp.where(slots == sa_ref[...], 1.0, jnp.where(slots == sb_ref[...], 1.0, 0.0)).astype(BF)
        acc_ref[...] += _dot(onehot, y_ref[...])

    @pl.when(last_ref[p] == 1)
    def _():
        def norm_rows(c, carry):
            rows = pl.ds(pl.multiple_of(c * MOE_EPILOGUE_ROWS, MOE_EPILOGUE_ROWS), MOE_EPILOGUE_ROWS)
            o_ref[rows, :] = _layer_norm(DEEPNORM_ALPHA * x_ref[rows, :] + g2_ref[...] * acc_ref[rows, :],
                                         lg_ref[...], lb_ref[...])
            return carry

        lax.fori_loop(0, MOE_CT // MOE_EPILOGUE_ROWS, norm_rows, 0)


def _combine(y, slot_a, slot_b, x_lat, mod, layer, ln_g, ln_b, plan):
    per_batch = SEQ // MOE_CT
    slot_blk = lambda p, tile, blk, *_: (blk[p], 0)
    tok_blk = lambda p, tile, *_: (tile[p], 0)
    grid_spec = pltpu.PrefetchScalarGridSpec(
        num_scalar_prefetch=5,
        grid=(MOE_NPAIR,),
        in_specs=[pl.BlockSpec((MOE_GB, D_MODEL), slot_blk),
                  pl.BlockSpec((MOE_CT, 1), tok_blk), pl.BlockSpec((MOE_CT, 1), tok_blk),
                  pl.BlockSpec((MOE_CT, D_MODEL), tok_blk),
                  pl.BlockSpec((None, 1, D_MODEL),
                               lambda p, tile, *_: (layer * 48 + (tile[p] // per_batch) * 6 + 5, 0, 0)),
                  pl.BlockSpec((1, D_MODEL), lambda p, *_: (0, 0)),
                  pl.BlockSpec((1, D_MODEL), lambda p, *_: (0, 0))],
        out_specs=pl.BlockSpec((MOE_CT, D_MODEL), tok_blk),
        scratch_shapes=[pltpu.VMEM((MOE_CT, D_MODEL), F32)])
    return pl.pallas_call(
        _combine_kernel,
        out_shape=jax.ShapeDtypeStruct((NL, D_MODEL), F32),
        grid_spec=grid_spec,
        compiler_params=_params("arbitrary"),
        name="moe_combine",
    )(*plan, y, slot_a.reshape(NL, 1), slot_b.reshape(NL, 1), x_lat, mod, ln_g, ln_b)


def _moe_ffn(h2, x_lat, mod, layer, router_w, wg, wu, wd, moe_layer, ln_g, ln_b):
    router_pad = jnp.pad(router_w, ((0, 0), (0, ROUTER_LANES - N_EXPERTS))).astype(BF)
    idx, wts = _router(h2, router_pad)
    slot_a, slot_b, tile_plan, d_plan, c_plan = _dispatch_plan(idx[:, :TOP_K])
    hs, ws = _dispatch(h2, slot_a, slot_b, wts[:, 0], wts[:, 1], d_plan)
    y = _experts(hs, tile_plan, wg, wu, wd, ws, moe_layer)
    return _combine(y, slot_a, slot_b, x_lat, mod, layer, ln_g, ln_b, c_plan)


def kernel(x, c, ctx, c_ctx, ada_w, ada_b, w_in, conv_w, gla_wg_f, gla_bg_f, gla_wg_b, gla_bg_b, gla_norm_g, na_rpb, w_br_conv, w_br_gla, w_br_na, gate_b, w_out, ln1_g, ln1_b, ln2_g, ln2_b, ffn_w_gate, ffn_w_up, ffn_w_down, moe_router, moe_w_gate, moe_w_up, moe_w_down):
    assert DEPTH == 2 and x.shape == (BATCH, SEQ, D_MODEL) and ctx.shape == (BATCH, CTX_LEN, D_MODEL)
    cvec = jnp.concatenate([c, c_ctx[None, :], jnp.zeros((8 - BATCH - 1, D_MODEL), F32)], axis=0)
    mod = _ada_table(cvec, ada_w, ada_b)
    rope_tab = _rope_table()
    x_lat, x_ctx = x.reshape(NL, D_MODEL), ctx.reshape(NC, D_MODEL)
    h1 = _modulate(x_lat, x_ctx, mod, 0)
    w_in_t = jnp.swapaxes(w_in, 1, 2)
    na_bias = _na_bias_pair_tables(na_rpb)

    for layer in range(DEPTH):
        last = layer == DEPTH - 1
        u = _project(h1, w_in_t, layer, 0, U_COLS, 1088, 1024, "proj_in")
        lr = _project(h1, w_in_t, layer, W_IN_LR, LR_COLS, 1088, LR_COLS, "proj_decay")
        un = _project(h1, w_in_t, layer, W_IN_NA, UN_COLS, 1088, NA_W, "proj_na", out_dtype=BF,
                      first_tile_scale=NA_DH ** -0.5)
        ug = _project(h1, w_in_t, layer, W_IN_GATES, N_BRANCH * D_MODEL, 1088, 1024, "proj_gates")

        y_conv = _short_conv(u, conv_w, layer)
        wpad_f, bg_f = _decay_weights(gla_wg_f[layer], gla_bg_f[layer], 0)
        wpad_b, bg_b = _decay_weights(gla_wg_b[layer], gla_bg_b[layer], GLA_RANK)
        o_f, o_b = _gla(u, lr, rope_tab, wpad_f, bg_f, wpad_b, bg_b)
        y_na = _neighbourhood_attention(un, na_bias, layer, with_ctx=not last)

        rows = NL if last else NT
        x_mid, h2 = _merge(y_conv, o_f, o_b, u, ug, y_na, x_lat, x_ctx, mod, layer, rows,
                           gla_norm_g[layer].reshape(1, GLA_DV), gate_b[layer].reshape(1, N_BRANCH * D_MODEL),
                           w_br_conv[layer].astype(BF), w_br_gla[layer].astype(BF), w_br_na[layer].astype(BF),
                           w_out[layer].astype(BF), ln1_g[layer].reshape(1, D_MODEL),
                           ln1_b[layer].reshape(1, D_MODEL))
        ln_g = ln2_g[layer].reshape(1, D_MODEL)
        ln_b = ln2_b[layer].reshape(1, D_MODEL)
        if layer % 2 == 0:
            j = layer // 2
            x_lat, h1 = _dense_ffn(h2, x_mid, mod, layer, ffn_w_gate, ffn_w_up, ffn_w_down, j, ln_g, ln_b)
            x_ctx = x_lat
        else:
            x_lat = _moe_ffn(h2, x_mid, mod, layer, moe_router[layer // 2], moe_w_gate, moe_w_up, moe_w_down,
                             layer // 2, ln_g, ln_b)
    return x_lat.reshape(BATCH, SEQ, D_MODEL)
```

```python
import functools

import numpy as np
import jax
import jax.numpy as jnp
from jax import lax
from jax.experimental import pallas as pl
from jax.experimental.pallas import tpu as pltpu

D_MODEL = 2048
BATCH = 2
SEQ = 4096
DEPTH = 2
GRID_W = 64
CTX_LEN = 256
CONV_DIM = 1024
CONV_K = 3
GLA_HEADS = 4
GLA_DK = 128
GLA_DV = 256
GLA_RANK = 16
GLA_TAU = 16.0
GLA_CHUNK = 64
GLA_QK = GLA_HEADS * GLA_DK
GLA_V = GLA_HEADS * GLA_DV
NA_HEADS = 8
NA_DH = 128
NA_W = NA_HEADS * NA_DH
WIN_R = 8
WIN_C = 16
D_FF = 5632
N_EXPERTS = 8
TOP_K = 2
ROPE_BASE = 10000.0
LN_EPS = 1e-5
N_BRANCH = 3
DEEPNORM_ALPHA = (2 * DEPTH) ** 0.25

NL = BATCH * SEQ
NC = BATCH * CTX_LEN
NT = NL + NC
GRID_ROWS = SEQ // GRID_W

BF = jnp.bfloat16
F32 = jnp.float32

V7X_VMEM_BYTES = 64 * 1024 * 1024
VMEM_LIMIT = V7X_VMEM_BYTES - 4 * 1024 * 1024
NEG_INF = -1e30

U_CONV_B, U_CONV_C, U_CONV_X = 0, 1024, 2048
U_GLA_Q, U_GLA_K, U_GLA_V, U_GLA_R = 3072, 3584, 4096, 5120
U_COLS = 6144
UN_Q, UN_K, UN_V = 0, 1024, 2048
UN_COLS = 3072
W_IN_LR = U_COLS
W_IN_NA = W_IN_LR + 2 * GLA_RANK
W_IN_GATES = W_IN_NA + UN_COLS
LR_COLS = 128

NA_QROWS = 4
NA_KROWS = NA_QROWS + WIN_R
NA_QBLK = NA_QROWS * GRID_W
NA_KBLK = NA_KROWS * GRID_W
NA_KPAIRS = NA_KROWS // 2
NA_BIAS_DR = 2 * WIN_R - 1
NA_PAIR_BOTH = 0
NA_PAIR_LEFT = NA_PAIR_BOTH + NA_BIAS_DR - 1
NA_PAIR_RIGHT = NA_PAIR_LEFT + NA_BIAS_DR
NA_PAIR_NONE = NA_PAIR_RIGHT + NA_BIAS_DR
NA_PAIR_ENTRIES = NA_PAIR_NONE + 1
NA_NBLK = GRID_ROWS // NA_QROWS

MOE_GB = 256
MOE_SM = 9 * MOE_GB
MOE_CH = 4 * MOE_GB
MOE_TF = 256
MOE_NST = (NL * TOP_K) // MOE_SM + N_EXPERTS
MOE_NSLOT = MOE_NST * MOE_SM
MOE_NGB = MOE_NSLOT // MOE_GB
MOE_NTT = NL // MOE_GB
MOE_CT = 1024
MOE_NCT = NL // MOE_CT
MOE_NPAIR = (NL * TOP_K) // MOE_GB + N_EXPERTS + N_EXPERTS * (MOE_NCT - 1)
MOE_EPILOGUE_ROWS = 128
ROUTER_LANES = 128


def _params(*sem):
    return pltpu.CompilerParams(dimension_semantics=sem, vmem_limit_bytes=VMEM_LIMIT)


def _dot(a, b):
    return jnp.dot(a, b, preferred_element_type=F32)


def _dot_nt(a, b):
    return lax.dot_general(a, b, (((1,), (1,)), ((), ())), preferred_element_type=F32)


def _dot_tn(a, b):
    return lax.dot_general(a, b, (((0,), (0,)), ((), ())), preferred_element_type=F32)


def _silu(x):
    return x * jax.nn.sigmoid(x)


def _layer_norm(x, g, b):
    mu = jnp.mean(x, axis=-1, keepdims=True)
    xc = x - mu
    var = jnp.mean(xc * xc, axis=-1, keepdims=True)
    return xc * lax.rsqrt(var + LN_EPS) * g + b


def _mod_row_of_tile(tile_rows):
    per_batch = SEQ // tile_rows
    return lambda i: jnp.minimum(i // per_batch, BATCH)


def _mod_spec(layer, k, row_of_tile):
    return pl.BlockSpec((None, 1, D_MODEL),
                        lambda i, *_: (layer * 48 + row_of_tile(i) * 6 + k, 0, 0))


def _ada_kernel(c_ref, w_ref, b_ref, o_ref):
    a = _silu(c_ref[...]).astype(BF)
    o_ref[...] = _dot(a, w_ref[...].astype(BF)) + b_ref[...]


def _ada_table(cvec, ada_w, ada_b):
    tn = 1024
    out = pl.pallas_call(
        _ada_kernel,
        out_shape=jax.ShapeDtypeStruct((DEPTH, 8, 6 * D_MODEL), F32),
        grid=(DEPTH, 6 * D_MODEL // tn),
        in_specs=[pl.BlockSpec((8, D_MODEL), lambda l, j: (0, 0)),
                  pl.BlockSpec((None, D_MODEL, tn), lambda l, j: (l, 0, j)),
                  pl.BlockSpec((None, 1, tn), lambda l, j: (l, 0, j))],
        out_specs=pl.BlockSpec((None, 8, tn), lambda l, j: (l, 0, j)),
        compiler_params=_params("parallel", "parallel"),
        name="ada_table",
    )(cvec, ada_w, ada_b.reshape(DEPTH, 1, 6 * D_MODEL))
    return out.reshape(DEPTH * 8 * 6, 1, D_MODEL)


def _select_rows(i, n_lat_tiles, xl_ref, xc_ref):
    return jnp.where(i < n_lat_tiles, xl_ref[...], xc_ref[...])


def _split_row_specs(tm):
    n_lat = NL // tm
    return (pl.BlockSpec((tm, D_MODEL), lambda i: (jnp.minimum(i, n_lat - 1), 0)),
            pl.BlockSpec((tm, D_MODEL), lambda i: (jnp.maximum(i - n_lat, 0), 0)))


def _modulate_kernel(xl_ref, xc_ref, sh_ref, sc_ref, o_ref, *, n_lat_tiles):
    x = _select_rows(pl.program_id(0), n_lat_tiles, xl_ref, xc_ref)
    o_ref[...] = (x * (1.0 + sc_ref[...]) + sh_ref[...]).astype(BF)


def _modulate(x_lat, x_ctx, mod, layer):
    tm = 512
    rot = _mod_row_of_tile(tm)
    return pl.pallas_call(
        functools.partial(_modulate_kernel, n_lat_tiles=NL // tm),
        out_shape=jax.ShapeDtypeStruct((NT, D_MODEL), BF),
        grid=(NT // tm,),
        in_specs=[*_split_row_specs(tm), _mod_spec(layer, 0, rot), _mod_spec(layer, 1, rot)],
        out_specs=pl.BlockSpec((tm, D_MODEL), lambda i: (i, 0)),
        compiler_params=_params("parallel"),
        name="modulate",
    )(x_lat, x_ctx, mod, mod)


PROJ_TM = NT // 8


def _mm_wt_kernel(a_ref, wt_ref, o_ref, wb_ref, *, first_tile_scale):
    @pl.when(pl.program_id(1) == 0)
    def _():
        wb_ref[...] = wt_ref[0].T.astype(BF)

    acc = _dot(a_ref[...], wb_ref[...])
    if first_tile_scale is not None:
        acc = acc * jnp.where(pl.program_id(0) == 0, first_tile_scale, 1.0)
    o_ref[...] = acc.astype(o_ref.dtype)


def _project(a, wt_stack, layer, col0, n, tm, tn, name, out_dtype=F32, first_tile_scale=None):
    m, k = a.shape
    assert col0 % 8 == 0
    return pl.pallas_call(
        functools.partial(_mm_wt_kernel, first_tile_scale=first_tile_scale),
        out_shape=jax.ShapeDtypeStruct((m, n), out_dtype),
        grid=(n // tn, m // tm),
        in_specs=[pl.BlockSpec((tm, k), lambda j, i: (i, 0)),
                  pl.BlockSpec((pl.Element(1), pl.Element(tn), pl.Element(k)),
                               lambda j, i: (layer, pl.multiple_of(col0 + j * tn, 8), 0))],
        out_specs=pl.BlockSpec((tm, tn), lambda j, i: (i, j)),
        scratch_shapes=[pltpu.VMEM((k, tn), BF)],
        compiler_params=_params("arbitrary", "arbitrary"),
        name=name,
    )(a, wt_stack)


CONV_TM = 256


def _conv_kernel(b_ref, c_ref, x_ref, cp_ref, xp_ref, cn_ref, xn_ref, w_ref, o_ref):
    i = pl.program_id(0)
    tiles_per_seq = SEQ // CONV_TM
    is_ctx = i >= NL // CONV_TM
    is_start = jnp.logical_or(i % tiles_per_seq == 0, is_ctx)
    is_end = jnp.logical_or(i % tiles_per_seq == tiles_per_seq - 1, is_ctx)
    z = c_ref[...] * x_ref[...]
    zp = cp_ref[...] * xp_ref[...]
    zn = cn_ref[...] * xn_ref[...]
    prev_row = jnp.where(is_start, 0.0, zp[7:8, :])
    next_row = jnp.where(is_end, 0.0, zn[0:1, :])
    rows = lax.broadcasted_iota(jnp.int32, z.shape, 0)
    z_prev = jnp.where(rows == 0, prev_row, pltpu.roll(z, 1, 0))
    z_next = jnp.where(rows == CONV_TM - 1, next_row, pltpu.roll(z, CONV_TM - 1, 0))
    w = w_ref[...]
    y = w[0:1, :] * z_prev + w[1:2, :] * z + w[2:3, :] * z_next
    o_ref[...] = (b_ref[...] * y).astype(BF)


def _short_conv(u, conv_w, layer):
    tm = CONV_TM
    n8 = tm // 8
    last8 = NT // 8 - 1
    return pl.pallas_call(
        _conv_kernel,
        out_shape=jax.ShapeDtypeStruct((NT, CONV_DIM), BF),
        grid=(NT // tm,),
        in_specs=[pl.BlockSpec((tm, CONV_DIM), lambda i: (i, U_CONV_B // CONV_DIM)),
                  pl.BlockSpec((tm, CONV_DIM), lambda i: (i, U_CONV_C // CONV_DIM)),
                  pl.BlockSpec((tm, CONV_DIM), lambda i: (i, U_CONV_X // CONV_DIM)),
                  pl.BlockSpec((8, CONV_DIM), lambda i: (jnp.maximum(i * n8 - 1, 0), U_CONV_C // CONV_DIM)),
                  pl.BlockSpec((8, CONV_DIM), lambda i: (jnp.maximum(i * n8 - 1, 0), U_CONV_X // CONV_DIM)),
                  pl.BlockSpec((8, CONV_DIM), lambda i: (jnp.minimum(i * n8 + n8, last8), U_CONV_C // CONV_DIM)),
                  pl.BlockSpec((8, CONV_DIM), lambda i: (jnp.minimum(i * n8 + n8, last8), U_CONV_X // CONV_DIM)),
                  pl.BlockSpec((None, CONV_K, CONV_DIM), lambda i: (layer, 0, 0))],
        out_specs=pl.BlockSpec((tm, CONV_DIM), lambda i: (i, 0)),
        compiler_params=_params("parallel"),
        name="short_conv",
    )(u, u, u, u, u, u, u, conv_w)


GLA_STEPS = (CTX_LEN + SEQ) // GLA_CHUNK
GLA_CTX_STEPS = CTX_LEN // GLA_CHUNK


def _log_sigmoid(z):
    return -(jnp.maximum(-z, 0.0) + jnp.log(1.0 + jnp.exp(-jnp.abs(z))))


def _rope(x, cs, sn):
    lane = lax.broadcasted_iota(jnp.int32, x.shape, 1)
    swapped = jnp.where(lane % 2 == 0, pltpu.roll(x, GLA_DK - 1, 1), pltpu.roll(x, 1, 1))
    return x * cs + swapped * sn


def _gla_decayed_operands(q_ref, k_ref, l_ref, r_ref, w_ref, b_ref, reverse):
    c = GLA_CHUNK
    z = _dot(l_ref[...].astype(BF), w_ref[...]) + b_ref[...]
    g = _log_sigmoid(z) * (1.0 / GLA_TAU)
    t_i = lax.broadcasted_iota(jnp.int32, (c, c), 0)
    s_i = lax.broadcasted_iota(jnp.int32, (c, c), 1)
    keep = (s_i >= t_i) if reverse else (s_i <= t_i)
    tri = jnp.where(keep, 1.0, 0.0).astype(BF)
    g1 = g.astype(BF)
    r1 = g - g1.astype(F32)
    g2 = r1.astype(BF)
    g3 = (r1 - g2.astype(F32)).astype(BF)
    b = _dot(tri, g1) + _dot(tri, g2) + _dot(tri, g3)
    total = b[0:1, :] if reverse else b[c - 1:c, :]
    cs = r_ref[:, :GLA_DK]
    sn = r_ref[:, GLA_DK:]
    heads = [slice(h * GLA_DK, (h + 1) * GLA_DK) for h in range(GLA_HEADS)]
    qs = jnp.concatenate([_rope(q_ref[:, hd] * (GLA_DK ** -0.5), cs, sn) for hd in heads], axis=1)
    kr = jnp.concatenate([_rope(k_ref[:, hd], cs, sn) for hd in heads], axis=1)
    q_dec = (qs * jnp.exp(b)).astype(BF)
    k_inv = (kr * jnp.exp(-b)).astype(BF)
    k_end = (kr * jnp.exp(total - b)).astype(BF)
    return keep, q_dec, k_inv, k_end, jnp.exp(total)


GLA_STREAM_REFS = 4 * BATCH + 1


def _gla_kernel(*refs):
    fwd, bwd = refs[:GLA_STREAM_REFS], refs[GLA_STREAM_REFS:2 * GLA_STREAM_REFS]
    wf_ref, bf_ref, wb_ref, bb_ref, of_ref, ob_ref, stf_ref, stb_ref = refs[2 * GLA_STREAM_REFS:]

    @pl.when(pl.program_id(0) == 0)
    def _():
        stf_ref[...] = jnp.zeros_like(stf_ref)
        stb_ref[...] = jnp.zeros_like(stb_ref)

    def stream_refs(stream, b):
        return stream[4 * b:4 * b + 4]

    pre, v_refs = {}, {}
    for d, (stream, w_ref, b_ref, reverse) in enumerate(((fwd, wf_ref, bf_ref, False), (bwd, wb_ref, bb_ref, True))):
        for b in range(BATCH):
            q_ref, k_ref, v_ref, l_ref = stream_refs(stream, b)
            pre[d, b] = _gla_decayed_operands(q_ref, k_ref, l_ref, stream[-1], w_ref, b_ref, reverse)
            v_refs[d, b] = v_ref
    chains = [(d, b, h) for d in range(2) for b in range(BATCH) for h in range(GLA_HEADS)]
    o_refs, st_refs = (of_ref, ob_ref), (stf_ref, stb_ref)

    def head(x, h):
        return x[:, h * GLA_DK:(h + 1) * GLA_DK]

    v = [v_refs[d, b][:, h * GLA_DV:(h + 1) * GLA_DV].astype(BF) for d, b, h in chains]
    st = [st_refs[d][b, h] for d, b, h in chains]
    att = [jnp.where(pre[d, b][0], _dot_nt(head(pre[d, b][1], h), head(pre[d, b][2], h)), 0.0).astype(BF)
           for d, b, h in chains]
    o = [_dot(att[i], v[i]) + _dot_nt(head(pre[d, b][1], h), st[i].astype(BF))
         for i, (d, b, h) in enumerate(chains)]
    for i in range(0, len(chains), GLA_HEADS):
        d, b, _ = chains[i]
        o_refs[d][b] = jnp.concatenate(o[i:i + GLA_HEADS], axis=1)
    for i, (d, b, h) in enumerate(chains):
        st_refs[d][b, h] = st[i] * head(pre[d, b][4], h) + _dot_tn(v[i], head(pre[d, b][3], h))


def _gla(u, lr, rope_tab, wpad_f, bg_f, wpad_b, bg_b):
    c = GLA_CHUNK
    lat_chunks = SEQ // c
    ctx_base = NL // c

    def row_f(b, s):
        return jnp.where(s < GLA_CTX_STEPS, ctx_base + b * GLA_CTX_STEPS + s, b * lat_chunks + s - GLA_CTX_STEPS)

    def row_b(b, s):
        return jnp.where(s < GLA_CTX_STEPS, ctx_base + b * GLA_CTX_STEPS + GLA_CTX_STEPS - 1 - s,
                         b * lat_chunks + GLA_STEPS - 1 - s)

    def pos_f(s):
        return jnp.where(s < GLA_CTX_STEPS, lat_chunks + s, s - GLA_CTX_STEPS)

    def pos_b(s):
        return jnp.where(s < GLA_CTX_STEPS, lat_chunks + GLA_CTX_STEPS - 1 - s, GLA_STEPS - 1 - s)

    def rope_chunk(pos):
        return jnp.minimum(pos, lat_chunks)

    def stream(row, pos):
        specs, args = [], []
        for b in range(BATCH):
            specs += [pl.BlockSpec((c, GLA_QK), lambda s, b=b: (row(b, s), U_GLA_Q // GLA_QK)),
                      pl.BlockSpec((c, GLA_QK), lambda s, b=b: (row(b, s), U_GLA_K // GLA_QK)),
                      pl.BlockSpec((c, GLA_V), lambda s, b=b: (row(b, s), U_GLA_V // GLA_V)),
                      pl.BlockSpec((c, LR_COLS), lambda s, b=b: (row(b, s), 0))]
            args += [u, u, u, lr]
        specs.append(pl.BlockSpec((c, 2 * GLA_DK), lambda s: (rope_chunk(pos(s)), 0)))
        args.append(rope_tab)
        return specs, args

    specs_f, args_f = stream(row_f, pos_f)
    specs_b, args_b = stream(row_b, pos_b)
    head_w = pl.BlockSpec((LR_COLS, GLA_QK), lambda s: (0, 0))
    head_b = pl.BlockSpec((1, GLA_QK), lambda s: (0, 0))
    state = pltpu.VMEM((BATCH, GLA_HEADS, GLA_DV, GLA_DK), F32)
    out = jax.ShapeDtypeStruct((BATCH, SEQ + CTX_LEN, GLA_V), F32)
    return pl.pallas_call(
        _gla_kernel,
        out_shape=(out, out),
        grid=(GLA_STEPS,),
        in_specs=specs_f + specs_b + [head_w, head_b, head_w, head_b],
        out_specs=(pl.BlockSpec((BATCH, c, GLA_V), lambda s: (0, pos_f(s), 0)),
                   pl.BlockSpec((BATCH, c, GLA_V), lambda s: (0, pos_b(s), 0))),
        scratch_shapes=[state, state],
        compiler_params=_params("arbitrary"),
        name="gla",
    )(*args_f, *args_b, wpad_f, bg_f, wpad_b, bg_b)


def _rope_table():
    t = jnp.arange(SEQ)
    rows = (t // GRID_W).astype(F32)
    cols = (t % GRID_W).astype(F32)
    n_freq = GLA_DK // 4
    inv = 1.0 / (ROPE_BASE ** (jnp.arange(n_freq, dtype=F32) / n_freq))
    ang = jnp.concatenate([rows[:, None] * inv, cols[:, None] * inv], -1)
    cos = jnp.repeat(jnp.cos(ang), 2, axis=-1)
    sin = jnp.repeat(jnp.sin(ang), 2, axis=-1)
    sign = jnp.tile(jnp.array([-1.0, 1.0], F32), GLA_DK // 2)
    cos = jnp.concatenate([cos, jnp.ones((GRID_W, GLA_DK), F32)], 0)
    sin = jnp.concatenate([sin * sign, jnp.zeros((GRID_W, GLA_DK), F32)], 0)
    return jnp.concatenate([cos, sin], axis=1)


def _decay_weights(wg, bg, lane_offset):
    w = jnp.pad(wg, ((lane_offset, LR_COLS - GLA_RANK - lane_offset), (0, 0))).astype(BF)
    return w, bg.reshape(1, GLA_QK)


NA_HPS = 4


def _na_softmax_out(parts):
    m = functools.reduce(jnp.maximum, [jnp.max(s, axis=-1, keepdims=True) for s, _ in parts])
    ps = [jnp.exp(s - m) for s, _ in parts]
    l = functools.reduce(jnp.add, [jnp.sum(p, axis=-1, keepdims=True) for p in ps])
    o = functools.reduce(jnp.add, [_dot(p.astype(BF), v) for p, (_, v) in zip(ps, parts)])
    return (o / l).astype(BF)


def _na_kernel(q_ref, k_ref, v_ref, kc_ref, vc_ref, bias_ref, o_ref):
    j = pl.program_id(2)
    head_cols = [slice(h * NA_DH, (h + 1) * NA_DH) for h in range(NA_HPS)]

    @pl.when(j < NA_NBLK)
    def _():
        base = jnp.clip(NA_QROWS * j - WIN_R // 2, 0, GRID_ROWS - NA_KROWS)
        start = pl.multiple_of(base * GRID_W, GRID_W)

        def pair_entry(qr, kp):
            r = NA_QROWS * j + qr
            rs = jnp.clip(r - WIN_R // 2, 0, GRID_ROWS - WIN_R)
            a0 = base + 2 * kp
            dr0 = a0 - r + WIN_R - 1
            in0 = jnp.logical_and(a0 >= rs, a0 < rs + WIN_R)
            in1 = jnp.logical_and(a0 + 1 >= rs, a0 + 1 < rs + WIN_R)
            entry = jnp.where(jnp.logical_and(in0, in1), NA_PAIR_BOTH + dr0,
                              jnp.where(in0, NA_PAIR_LEFT + dr0,
                                        jnp.where(in1, NA_PAIR_RIGHT + dr0 + 1, NA_PAIR_NONE)))
            return jnp.clip(entry, 0, NA_PAIR_ENTRIES - 1)

        entries = [[pair_entry(qr, kp) for kp in range(NA_KPAIRS)] for qr in range(NA_QROWS)]
        for h, hc in enumerate(head_cols):
            q = q_ref[:, hc]
            vl = v_ref[pl.ds(start, NA_KBLK), hc]
            bias = jnp.concatenate(
                [jnp.concatenate([bias_ref[h, entries[qr][kp]] for kp in range(NA_KPAIRS)], axis=1)
                 for qr in range(NA_QROWS)], axis=0)
            s_loc = _dot_nt(q, k_ref[pl.ds(start, NA_KBLK), hc]) + bias
            s_ctx = _dot_nt(q, kc_ref[:, hc])
            o_ref[:, hc] = _na_softmax_out([(s_loc, vl), (s_ctx, vc_ref[:, hc])])

    @pl.when(j == NA_NBLK)
    def _():
        for hc in head_cols:
            s_ctx = _dot_nt(q_ref[:, hc], kc_ref[:, hc])
            o_ref[:, hc] = _na_softmax_out([(s_ctx, vc_ref[:, hc])])


def _na_bias_pair_tables(rpb_all):
    qc = np.arange(GRID_W)
    cs = np.clip(qc - WIN_C // 2, 0, GRID_W - WIN_C)
    kc = np.arange(GRID_W)
    col_ok = (kc[None, :] >= cs[:, None]) & (kc[None, :] < cs[:, None] + WIN_C)
    dc = kc[None, :] - qc[:, None] + WIN_C - 1
    col_sel = ((dc[None] == np.arange(2 * WIN_C - 1)[:, None, None]) & col_ok[None]).astype(np.float32)
    t = jnp.einsum('lhrd,dxy->lhrxy', rpb_all, jnp.asarray(col_sel), precision=lax.Precision.HIGHEST)
    t = jnp.where(jnp.asarray(col_ok), t, NEG_INF)
    masked = jnp.full_like(t, NEG_INF)
    both = jnp.concatenate([t[:, :, :-1], t[:, :, 1:]], axis=-1)
    left = jnp.concatenate([t, masked], axis=-1)
    right = jnp.concatenate([masked, t], axis=-1)
    none = jnp.concatenate([masked[:, :, :1], masked[:, :, :1]], axis=-1)
    return jnp.concatenate([both, left, right, none], axis=2)


def _neighbourhood_attention(un, bias_pairs, layer, with_ctx):
    steps = NA_NBLK + (1 if with_ctx else 0)
    ctx_q = NL // NA_QBLK
    w = NA_HPS * NA_DH

    def q_row(b, j):
        return jnp.where(j < NA_NBLK, b * NA_NBLK + j, ctx_q + b)

    ctx_blk = NL // CTX_LEN
    return pl.pallas_call(
        _na_kernel,
        out_shape=jax.ShapeDtypeStruct((NT, NA_W), BF),
        grid=(BATCH, NA_HEADS // NA_HPS, steps),
        in_specs=[pl.BlockSpec((NA_QBLK, w), lambda b, h, j: (q_row(b, j), UN_Q // w + h)),
                  pl.BlockSpec((SEQ, w), lambda b, h, j: (b, UN_K // w + h)),
                  pl.BlockSpec((SEQ, w), lambda b, h, j: (b, UN_V // w + h)),
                  pl.BlockSpec((CTX_LEN, w), lambda b, h, j: (ctx_blk + b, UN_K // w + h)),
                  pl.BlockSpec((CTX_LEN, w), lambda b, h, j: (ctx_blk + b, UN_V // w + h)),
                  pl.BlockSpec((None, NA_HPS, NA_PAIR_ENTRIES, GRID_W, 2 * GRID_W),
                               lambda b, h, j: (layer, h, 0, 0, 0))],
        out_specs=pl.BlockSpec((NA_QBLK, w), lambda b, h, j: (q_row(b, j), h)),
        compiler_params=_params("parallel", "parallel", "arbitrary"),
        name="neighbourhood_attention",
    )(un, un, un, un, un, bias_pairs)


MERGE_TM = 256


def _merge_kernel(yc_ref, of_ref, ob_ref, r_ref, yn_ref, gt_ref, xl_ref, xc_ref,
                  ng_ref, gb_ref, wc_ref, wg_ref, wn_ref, wo_ref,
                  g1_ref, lg_ref, lb_ref, sh2_ref, sc2_ref, xo_ref, ho_ref):
    x = _select_rows(pl.program_id(0), NL // MERGE_TM, xl_ref, xc_ref)
    o = of_ref[...] + ob_ref[...]
    r = r_ref[...]
    ng = ng_ref[...]
    heads = []
    for h in range(GLA_HEADS):
        oh = o[:, h * GLA_DV:(h + 1) * GLA_DV]
        oh = oh * lax.rsqrt(jnp.mean(oh * oh, axis=-1, keepdims=True) + LN_EPS)
        heads.append(oh * ng * _silu(r[:, h * GLA_DV:(h + 1) * GLA_DV]))
    y_gla = jnp.concatenate(heads, axis=-1).astype(BF)
    g = jax.nn.sigmoid(gt_ref[...] + gb_ref[...])
    z = (g[:, :D_MODEL] * _dot(yc_ref[...], wc_ref[...])
         + g[:, D_MODEL:2 * D_MODEL] * _dot(y_gla, wg_ref[...])
         + g[:, 2 * D_MODEL:] * _dot(yn_ref[...], wn_ref[...]))
    y = _dot(z.astype(BF), wo_ref[...])
    xn = _layer_norm(DEEPNORM_ALPHA * x + g1_ref[...] * y, lg_ref[...], lb_ref[...])
    xo_ref[...] = xn
    ho_ref[...] = (xn * (1.0 + sc2_ref[...]) + sh2_ref[...]).astype(BF)


def _merge(y_conv, o_f, o_b, u, ug, y_na, x_lat, x_ctx, mod, layer, rows, norm_g, gate_b, w_conv, w_gla, w_na,
           w_out, ln_g, ln_b):
    tm = MERGE_TM
    rot = _mod_row_of_tile(tm)
    row = lambda width: pl.BlockSpec((tm, width), lambda i: (i, 0))
    const = lambda shape: pl.BlockSpec(shape, lambda i: (0,) * len(shape), pipeline_mode=pl.Buffered(1))
    lat_tiles, seq_tiles = NL // tm, SEQ // tm
    gla_rows = pl.BlockSpec((None, tm, GLA_V),
                            lambda i: (jnp.where(i < lat_tiles, i // seq_tiles, i - lat_tiles),
                                       jnp.where(i < lat_tiles, i % seq_tiles, seq_tiles), 0))
    assert CTX_LEN == tm
    return pl.pallas_call(
        _merge_kernel,
        out_shape=(jax.ShapeDtypeStruct((rows, D_MODEL), F32), jax.ShapeDtypeStruct((rows, D_MODEL), BF)),
        grid=(rows // tm,),
        in_specs=[row(CONV_DIM), gla_rows, gla_rows,
                  pl.BlockSpec((tm, GLA_V), lambda i: (i, U_GLA_R // GLA_V)),
                  row(NA_W), row(N_BRANCH * D_MODEL), *_split_row_specs(tm),
                  const((1, GLA_DV)), const((1, N_BRANCH * D_MODEL)),
                  const((CONV_DIM, D_MODEL)), const((GLA_V, D_MODEL)), const((NA_W, D_MODEL)),
                  const((D_MODEL, D_MODEL)),
                  _mod_spec(layer, 2, rot), const((1, D_MODEL)), const((1, D_MODEL)),
                  _mod_spec(layer, 3, rot), _mod_spec(layer, 4, rot)],
        out_specs=(row(D_MODEL), row(D_MODEL)),
        compiler_params=_params("parallel"),
        name="merge",
    )(y_conv, o_f, o_b, u, y_na, ug, x_lat, x_ctx, norm_g, gate_b, w_conv, w_gla, w_na, w_out,
      mod, ln_g, ln_b, mod, mod)


FFN_TM_LATENT = 1024
FFN_TM_CONTEXT = NC
FFN_TF = 256
FFN_EPILOGUE_ROWS = 128


def _ffn_kernel(h_ref, wg_ref, wu_ref, wd_ref, x_ref, g2_ref, lg_ref, lb_ref, shn_ref, scn_ref, *refs):
    xo_ref, ho_ref, acc_ref = refs[-3:]
    j = pl.program_id(1)

    @pl.when(j == 0)
    def _():
        acc_ref[...] = jnp.zeros_like(acc_ref)

    h = h_ref[...]
    a = (_silu(_dot(h, wg_ref[...].astype(BF))) * _dot(h, wu_ref[...].astype(BF))).astype(BF)
    acc_ref[...] += _dot(a, wd_ref[...].astype(BF))

    @pl.when(j == pl.num_programs(1) - 1)
    def _():
        def norm_rows(c, carry):
            rows = pl.ds(pl.multiple_of(c * FFN_EPILOGUE_ROWS, FFN_EPILOGUE_ROWS), FFN_EPILOGUE_ROWS)
            xn = _layer_norm(DEEPNORM_ALPHA * x_ref[rows, :] + g2_ref[...] * acc_ref[rows, :],
                             lg_ref[...], lb_ref[...])
            xo_ref[rows, :] = xn
            ho_ref[rows, :] = (xn * (1.0 + scn_ref[...]) + shn_ref[...]).astype(BF)
            return carry

        lax.fori_loop(0, acc_ref.shape[0] // FFN_EPILOGUE_ROWS, norm_rows, 0)


def _dense_ffn_rows(h2, x_all, mod, layer, wg, wu, wd, ffn_layer, ln_g, ln_b, row0, n_rows, tm, prev):
    tile0 = row0 // tm
    rot = _mod_row_of_tile(tm)
    mod_spec = lambda lyr, k: pl.BlockSpec((None, 1, D_MODEL),
                                           lambda i, j: (lyr * 48 + rot(tile0 + i) * 6 + k, 0, 0))
    row = lambda **kw: pl.BlockSpec((tm, D_MODEL), lambda i, j: (tile0 + i, 0), **kw)
    const = lambda: pl.BlockSpec((1, D_MODEL), lambda i, j: (0, 0))
    once = dict(pipeline_mode=pl.Buffered(1))
    in_specs = [row(),
                pl.BlockSpec((None, D_MODEL, FFN_TF), lambda i, j: (ffn_layer, 0, j)),
                pl.BlockSpec((None, D_MODEL, FFN_TF), lambda i, j: (ffn_layer, 0, j)),
                pl.BlockSpec((None, FFN_TF, D_MODEL), lambda i, j: (ffn_layer, j, 0)),
                row(**once), mod_spec(layer, 5), const(), const(),
                mod_spec(layer + 1, 0), mod_spec(layer + 1, 1)]
    args = [h2, wg, wu, wd, x_all, mod, ln_g, ln_b, mod, mod]
    aliases = {}
    if prev is not None:
        aliases = {len(args): 0, len(args) + 1: 1}
        in_specs += [pl.BlockSpec(memory_space=pl.ANY), pl.BlockSpec(memory_space=pl.ANY)]
        args += list(prev)
    return pl.pallas_call(
        _ffn_kernel,
        out_shape=(jax.ShapeDtypeStruct((NT, D_MODEL), F32), jax.ShapeDtypeStruct((NT, D_MODEL), BF)),
        grid=(n_rows // tm, D_FF // FFN_TF),
        in_specs=in_specs,
        out_specs=(row(**once), row(**once)),
        scratch_shapes=[pltpu.VMEM((tm, D_MODEL), F32)],
        input_output_aliases=aliases,
        compiler_params=_params("parallel", "arbitrary"),
        name="dense_ffn",
    )(*args)


def _dense_ffn(h2, x_all, mod, layer, wg, wu, wd, ffn_layer, ln_g, ln_b):
    out = _dense_ffn_rows(h2, x_all, mod, layer, wg, wu, wd, ffn_layer, ln_g, ln_b, 0, NL, FFN_TM_LATENT, None)
    return _dense_ffn_rows(h2, x_all, mod, layer, wg, wu, wd, ffn_layer, ln_g, ln_b, NL, NC, FFN_TM_CONTEXT, out)


def _router_kernel(h_ref, w_ref, idx_ref, wt_ref):
    logits = _dot(h_ref[...], w_ref[...])
    lane = lax.broadcasted_iota(jnp.int32, logits.shape, 1)
    logits = jnp.where(lane < N_EXPERTS, logits, -jnp.inf)
    m1 = jnp.max(logits, axis=-1, keepdims=True)
    i1 = jnp.min(jnp.where(logits == m1, lane, ROUTER_LANES), axis=-1, keepdims=True)
    rest = jnp.where(lane == i1, -jnp.inf, logits)
    m2 = jnp.max(rest, axis=-1, keepdims=True)
    i2 = jnp.min(jnp.where(rest == m2, lane, ROUTER_LANES), axis=-1, keepdims=True)
    e2 = jnp.exp(m2 - m1)
    w1 = 1.0 / (1.0 + e2)
    w2 = e2 / (1.0 + e2)
    idx_ref[...] = jnp.where(lane == 0, i1, jnp.where(lane == 1, i2, 0))
    wt_ref[...] = jnp.where(lane == 0, w1, jnp.where(lane == 1, w2, 0.0))


def _router(h2, router_w):
    tm = 512
    return pl.pallas_call(
        _router_kernel,
        out_shape=(jax.ShapeDtypeStruct((NL, ROUTER_LANES), jnp.int32),
                   jax.ShapeDtypeStruct((NL, ROUTER_LANES), F32)),
        grid=(NL // tm,),
        in_specs=[pl.BlockSpec((tm, D_MODEL), lambda i: (i, 0)),
                  pl.BlockSpec((D_MODEL, ROUTER_LANES), lambda i: (0, 0))],
        out_specs=(pl.BlockSpec((tm, ROUTER_LANES), lambda i: (i, 0)),
                   pl.BlockSpec((tm, ROUTER_LANES), lambda i: (i, 0))),
        compiler_params=_params("parallel"),
        name="moe_router",
    )(h2, router_w)


def _dispatch_plan(idx):
    i32 = jnp.int32
    experts = jnp.arange(N_EXPERTS, dtype=i32)
    oh_a = (idx[:, 0:1] == experts).astype(i32)
    oh_b = (idx[:, 1:2] == experts).astype(i32)
    cum = jnp.cumsum(oh_a + oh_b, axis=0)
    counts = cum[-1]
    n_st = (counts + MOE_SM - 1) // MOE_SM
    st_end = jnp.cumsum(n_st)
    st_start = st_end - n_st
    g_start = st_start * MOE_SM
    g_end = st_end * MOE_SM
    slot_table = g_start[None, :] + cum - 1
    slot_a = jnp.sum(oh_a * slot_table, axis=1)
    slot_b = jnp.sum(oh_b * slot_table, axis=1)

    n_used = st_end[-1]
    st_ids = jnp.minimum(jnp.arange(MOE_NST, dtype=i32), n_used - 1)
    st_expert = jnp.minimum(jnp.sum((st_ids[:, None] >= st_end[None, :]).astype(i32), axis=1), N_EXPERTS - 1)
    st_left = counts[st_expert] - (st_ids - st_start[st_expert]) * MOE_SM
    st_rows = ((jnp.clip(st_left, 0, MOE_SM) + MOE_GB - 1) // MOE_GB) * MOE_GB
    tile_plan = (st_expert, st_rows, n_used.reshape(1))

    pair_ids = jnp.arange(MOE_NPAIR, dtype=i32)
    tile_last = jnp.arange(MOE_NTT, dtype=i32) * MOE_GB + MOE_GB - 1
    hi = cum[tile_last]
    lo = jnp.concatenate([jnp.zeros((1, N_EXPERTS), i32), hi[:-1]], axis=0)

    blk_ids = jnp.arange(MOE_NGB, dtype=i32)
    blk_start = blk_ids * MOE_GB
    blk_e = jnp.minimum(jnp.sum((blk_start[:, None] >= g_end[None, :]).astype(i32), axis=1), N_EXPERTS - 1)
    r0 = blk_start - g_start[blk_e]
    r1 = jnp.minimum(r0 + MOE_GB, counts[blk_e])
    has = (blk_start < g_end[-1]) & (r1 > r0)
    hi_e = hi[:, blk_e]
    t_lo = jnp.sum((hi_e <= r0[None, :]).astype(i32), axis=0)
    t_hi = jnp.sum((hi_e < r1[None, :]).astype(i32), axis=0)
    t_lo = jnp.where(has, t_lo, 0)
    t_end = jnp.where(has, t_hi + 1, 0)
    blk_map = lax.cummax(jnp.where(has, blk_ids, 0), axis=0)
    dispatch = (t_lo, t_end, blk_map, has.astype(i32))

    c_hi = cum[jnp.arange(MOE_NCT, dtype=i32) * MOE_CT + MOE_CT - 1]
    c_lo = jnp.concatenate([jnp.zeros((1, N_EXPERTS), i32), c_hi[:-1]], axis=0)
    b_lo = (g_start[None, :] + c_lo) // MOE_GB
    b_hi = (g_start[None, :] + c_hi - 1) // MOE_GB
    n_c = jnp.where(c_hi > c_lo, b_hi - b_lo + 1, 0).reshape(-1)
    b_lo = b_lo.reshape(-1)
    c_end = jnp.cumsum(n_c)
    c_total = c_end[-1]
    c_idx = jnp.minimum(jnp.sum((pair_ids[:, None] >= c_end[None, :]).astype(i32), axis=1),
                        MOE_NCT * N_EXPERTS - 1)
    c_blk = jnp.clip(b_lo[c_idx] + pair_ids - (c_end[c_idx] - n_c[c_idx]), 0, MOE_NGB - 1)
    c_tile = c_idx // N_EXPERTS
    c_valid = pair_ids < c_total
    c_tile = jnp.where(c_valid, c_tile, MOE_NCT - 1)
    c_blk = jnp.where(c_valid, c_blk, c_blk[jnp.maximum(c_total - 1, 0)])
    c_prev = jnp.concatenate([jnp.full((1,), -1, i32), c_tile[:-1]])
    c_next = jnp.concatenate([c_tile[1:], jnp.full((1,), -1, i32)])
    c_first = c_valid & (c_tile != c_prev)
    c_last = c_valid & ((c_tile != c_next) | (pair_ids == c_total - 1))
    combine = (c_tile, c_blk, c_first.astype(i32), c_last.astype(i32), c_valid.astype(i32))
    return slot_a, slot_b, tile_plan, dispatch, combine


def _dispatch_kernel(lo_ref, end_ref, map_ref, has_ref, h_ref, sa_ref, sb_ref, wa_ref, wb_ref, hs_ref, ws_ref,
                     acc_ref, wacc_ref):
    blk = pl.program_id(0)

    @pl.when(has_ref[blk] == 1)
    def _():
        acc_ref[...] = jnp.zeros_like(acc_ref)
        wacc_ref[...] = jnp.zeros_like(wacc_ref)
        slots = blk * MOE_GB + lax.broadcasted_iota(jnp.int32, (MOE_GB, MOE_GB), 0)

        def scan_tile(t, carry):
            rows = pl.ds(pl.multiple_of(t * MOE_GB, MOE_GB), MOE_GB)
            hit_a = slots == sa_ref[t]
            hit_b = slots == sb_ref[t]
            onehot = jnp.where(hit_a, 1.0, jnp.where(hit_b, 1.0, 0.0)).astype(BF)
            acc_ref[...] += _dot(onehot, h_ref[rows, :])
            w = jnp.where(hit_a, wa_ref[t], jnp.where(hit_b, wb_ref[t], 0.0))
            wacc_ref[...] += jnp.broadcast_to(jnp.sum(w, axis=1, keepdims=True), wacc_ref.shape)
            return carry

        lax.fori_loop(lo_ref[blk], end_ref[blk], scan_tile, 0)
        hs_ref[...] = acc_ref[...].astype(BF)
        ws_ref[...] = wacc_ref[...]


def _dispatch(h2, slot_a, slot_b, w_a, w_b, plan):
    used = lambda i, lo, end, blk_map, has: (blk_map[i], 0)
    resident = lambda shape: pl.BlockSpec(shape, lambda i, *_: (0,) * len(shape), pipeline_mode=pl.Buffered(1))
    shape3 = (MOE_NTT, 1, MOE_GB)
    grid_spec = pltpu.PrefetchScalarGridSpec(
        num_scalar_prefetch=4,
        grid=(MOE_NGB,),
        in_specs=[resident((NL, D_MODEL)), resident(shape3), resident(shape3), resident(shape3),
                  resident(shape3)],
        out_specs=(pl.BlockSpec((MOE_GB, D_MODEL), used), pl.BlockSpec((MOE_GB, ROUTER_LANES), used)),
        scratch_shapes=[pltpu.VMEM((MOE_GB, D_MODEL), F32), pltpu.VMEM((MOE_GB, ROUTER_LANES), F32)])
    return pl.pallas_call(
        _dispatch_kernel,
        out_shape=(jax.ShapeDtypeStruct((MOE_NSLOT, D_MODEL), BF),
                   jax.ShapeDtypeStruct((MOE_NSLOT, ROUTER_LANES), F32)),
        grid_spec=grid_spec,
        compiler_params=_params("arbitrary"),
        name="moe_dispatch",
    )(*plan, h2, slot_a.reshape(shape3), slot_b.reshape(shape3), w_a.reshape(shape3), w_b.reshape(shape3))


def _expert_kernel(e_ref, rows_ref, n_ref, hs_ref, wg_ref, wu_ref, wd_ref, ws_ref, y_ref, acc_ref):
    s = pl.program_id(0)
    j = pl.program_id(1)
    last = pl.num_programs(1) - 1

    @pl.when(s < n_ref[0])
    def _():
        n_rows = rows_ref[s]

        def for_row_chunks(fn):
            n_full = n_rows // MOE_CH

            def full_chunk(c, carry):
                fn(pl.ds(pl.multiple_of(c * MOE_CH, MOE_CH), MOE_CH))
                return carry

            lax.fori_loop(0, n_full, full_chunk, 0)
            rest = n_rows - n_full * MOE_CH
            half = MOE_CH // 2

            @pl.when(rest >= half)
            def _():
                fn(pl.ds(pl.multiple_of(n_full * MOE_CH, MOE_GB), half))

            @pl.when(rest % half != 0)
            def _():
                fn(pl.ds(pl.multiple_of(n_rows - MOE_GB, MOE_GB), MOE_GB))

        @pl.when(j == 0)
        def _():
            acc_ref[...] = jnp.zeros_like(acc_ref)

        def accumulate(rows):
            h = hs_ref[rows, :]
            a = (_silu(_dot(h, wg_ref[...].astype(BF))) * _dot(h, wu_ref[...].astype(BF))).astype(BF)
            acc_ref[rows, :] += _dot(a, wd_ref[...].astype(BF))

        for_row_chunks(accumulate)

        @pl.when(j == last)
        def _():
            def emit(rows):
                y_ref[rows, :] = (acc_ref[rows, :] * ws_ref[rows, 0:1]).astype(BF)

            for_row_chunks(emit)


def _experts(hs, tile_plan, wg, wu, wd, ws, moe_layer):
    n_ff = D_FF // MOE_TF

    def used(s, n):
        return jnp.minimum(s, n[0] - 1)

    def ff_tile(s, j, n):
        return jnp.where(s < n[0], j, n_ff - 1)

    rows_spec = lambda width, **kw: pl.BlockSpec((MOE_SM, width), lambda s, j, e, r, n: (used(s, n), 0), **kw)
    up_spec = pl.BlockSpec((None, None, D_MODEL, MOE_TF),
                           lambda s, j, e, r, n: (moe_layer, e[s], 0, ff_tile(s, j, n)))
    grid_spec = pltpu.PrefetchScalarGridSpec(
        num_scalar_prefetch=3,
        grid=(MOE_NST, n_ff),
        in_specs=[rows_spec(D_MODEL, pipeline_mode=pl.Buffered(1)), up_spec, up_spec,
                  pl.BlockSpec((None, None, MOE_TF, D_MODEL),
                               lambda s, j, e, r, n: (moe_layer, e[s], ff_tile(s, j, n), 0)),
                  rows_spec(ROUTER_LANES)],
        out_specs=rows_spec(D_MODEL, pipeline_mode=pl.Buffered(1)),
        scratch_shapes=[pltpu.VMEM((MOE_SM, D_MODEL), F32)])
    return pl.pallas_call(
        _expert_kernel,
        out_shape=jax.ShapeDtypeStruct((MOE_NSLOT, D_MODEL), BF),
        grid_spec=grid_spec,
        compiler_params=_params("arbitrary", "arbitrary"),
        name="moe_experts",
    )(*tile_plan, hs, wg, wu, wd, ws)


def _combine_kernel(tile_ref, blk_ref, first_ref, last_ref, valid_ref,
                    y_ref, sa_ref, sb_ref, x_ref, g2_ref, lg_ref, lb_ref, o_ref, acc_ref):
    p = pl.program_id(0)

    @pl.when(first_ref[p] == 1)
    def _():
        acc_ref[...] = jnp.zeros_like(acc_ref)

    @pl.when(valid_ref[p] == 1)
    def _():
        slots = blk_ref[p] * MOE_GB + lax.broadcasted_iota(jnp.int32, (MOE_CT, MOE_GB), 1)
        onehot = jnp.where(slots == sa_ref[...], 1.0, jnp.where(slots == sb_ref[...], 1.0, 0.0)).astype(BF)
        acc_ref[...] += _dot(onehot, y_ref[...])

    @pl.when(last_ref[p] == 1)
    def _():
        def norm_rows(c, carry):
            rows = pl.ds(pl.multiple_of(c * MOE_EPILOGUE_ROWS, MOE_EPILOGUE_ROWS), MOE_EPILOGUE_ROWS)
            o_ref[rows, :] = _layer_norm(DEEPNORM_ALPHA * x_ref[rows, :] + g2_ref[...] * acc_ref[rows, :],
                                         lg_ref[...], lb_ref[...])
            return carry

        lax.fori_loop(0, MOE_CT // MOE_EPILOGUE_ROWS, norm_rows, 0)


def _combine(y, slot_a, slot_b, x_lat, mod, layer, ln_g, ln_b, plan):
    per_batch = SEQ // MOE_CT
    slot_blk = lambda p, tile, blk, *_: (blk[p], 0)
    tok_blk = lambda p, tile, *_: (tile[p], 0)
    grid_spec = pltpu.PrefetchScalarGridSpec(
        num_scalar_prefetch=5,
        grid=(MOE_NPAIR,),
        in_specs=[pl.BlockSpec((MOE_GB, D_MODEL), slot_blk),
                  pl.BlockSpec((MOE_CT, 1), tok_blk), pl.BlockSpec((MOE_CT, 1), tok_blk),
                  pl.BlockSpec((MOE_CT, D_MODEL), tok_blk),
                  pl.BlockSpec((None, 1, D_MODEL),
                               lambda p, tile, *_: (layer * 48 + (tile[p] // per_batch) * 6 + 5, 0, 0)),
                  pl.BlockSpec((1, D_MODEL), lambda p, *_: (0, 0)),
                  pl.BlockSpec((1, D_MODEL), lambda p, *_: (0, 0))],
        out_specs=pl.BlockSpec((MOE_CT, D_MODEL), tok_blk),
        scratch_shapes=[pltpu.VMEM((MOE_CT, D_MODEL), F32)])
    return pl.pallas_call(
        _combine_kernel,
        out_shape=jax.ShapeDtypeStruct((NL, D_MODEL), F32),
        grid_spec=grid_spec,
        compiler_params=_params("arbitrary"),
        name="moe_combine",
    )(*plan, y, slot_a.reshape(NL, 1), slot_b.reshape(NL, 1), x_lat, mod, ln_g, ln_b)


def _moe_ffn(h2, x_lat, mod, layer, router_w, wg, wu, wd, moe_layer, ln_g, ln_b):
    router_pad = jnp.pad(router_w, ((0, 0), (0, ROUTER_LANES - N_EXPERTS))).astype(BF)
    idx, wts = _router(h2, router_pad)
    slot_a, slot_b, tile_plan, d_plan, c_plan = _dispatch_plan(idx[:, :TOP_K])
    hs, ws = _dispatch(h2, slot_a, slot_b, wts[:, 0], wts[:, 1], d_plan)
    y = _experts(hs, tile_plan, wg, wu, wd, ws, moe_layer)
    return _combine(y, slot_a, slot_b, x_lat, mod, layer, ln_g, ln_b, c_plan)


def kernel(x, c, ctx, c_ctx, ada_w, ada_b, w_in, conv_w, gla_wg_f, gla_bg_f, gla_wg_b, gla_bg_b, gla_norm_g, na_rpb, w_br_conv, w_br_gla, w_br_na, gate_b, w_out, ln1_g, ln1_b, ln2_g, ln2_b, ffn_w_gate, ffn_w_up, ffn_w_down, moe_router, moe_w_gate, moe_w_up, moe_w_down):
    assert DEPTH == 2 and x.shape == (BATCH, SEQ, D_MODEL) and ctx.shape == (BATCH, CTX_LEN, D_MODEL)
    cvec = jnp.concatenate([c, c_ctx[None, :], jnp.zeros((8 - BATCH - 1, D_MODEL), F32)], axis=0)
    mod = _ada_table(cvec, ada_w, ada_b)
    rope_tab = _rope_table()
    x_lat, x_ctx = x.reshape(NL, D_MODEL), ctx.reshape(NC, D_MODEL)
    h1 = _modulate(x_lat, x_ctx, mod, 0)
    w_in_t = jnp.swapaxes(w_in, 1, 2)
    na_bias = _na_bias_pair_tables(na_rpb)

    for layer in range(DEPTH):
        last = layer == DEPTH - 1
        u = _project(h1, w_in_t, layer, 0, U_COLS, PROJ_TM, 1024, "proj_in")
        lr = _project(h1, w_in_t, layer, W_IN_LR, LR_COLS, PROJ_TM, LR_COLS, "proj_decay")
        un = _project(h1, w_in_t, layer, W_IN_NA, UN_COLS, PROJ_TM, NA_W, "proj_na", out_dtype=BF,
                      first_tile_scale=NA_DH ** -0.5)
        ug = _project(h1, w_in_t, layer, W_IN_GATES, N_BRANCH * D_MODEL, PROJ_TM, 1024, "proj_gates")

        y_conv = _short_conv(u, conv_w, layer)
        wpad_f, bg_f = _decay_weights(gla_wg_f[layer], gla_bg_f[layer], 0)
        wpad_b, bg_b = _decay_weights(gla_wg_b[layer], gla_bg_b[layer], GLA_RANK)
        o_f, o_b = _gla(u, lr, rope_tab, wpad_f, bg_f, wpad_b, bg_b)
        y_na = _neighbourhood_attention(un, na_bias, layer, with_ctx=not last)

        rows = NL if last else NT
        x_mid, h2 = _merge(y_conv, o_f, o_b, u, ug, y_na, x_lat, x_ctx, mod, layer, rows,
                           gla_norm_g[layer].reshape(1, GLA_DV), gate_b[layer].reshape(1, N_BRANCH * D_MODEL),
                           w_br_conv[layer].astype(BF), w_br_gla[layer].astype(BF), w_br_na[layer].astype(BF),
                           w_out[layer].astype(BF), ln1_g[layer].reshape(1, D_MODEL),
                           ln1_b[layer].reshape(1, D_MODEL))
        ln_g = ln2_g[layer].reshape(1, D_MODEL)
        ln_b = ln2_b[layer].reshape(1, D_MODEL)
        if layer % 2 == 0:
            j = layer // 2
            x_lat, h1 = _dense_ffn(h2, x_mid, mod, layer, ffn_w_gate, ffn_w_up, ffn_w_down, j, ln_g, ln_b)
            x_ctx = x_lat
        else:
            x_lat = _moe_ffn(h2, x_mid, mod, layer, moe_router[layer // 2], moe_w_gate, moe_w_up, moe_w_down,
                             layer // 2, ln_g, ln_b)
    return x_lat.reshape(BATCH, SEQ, D_MODEL)
```

```python
import functools

import numpy as np
import jax
import jax.numpy as jnp
from jax import lax
from jax.experimental import pallas as pl
from jax.experimental.pallas import tpu as pltpu

D_MODEL = 2048
BATCH = 2
SEQ = 4096
DEPTH = 2
GRID_W = 64
CTX_LEN = 256
CONV_DIM = 1024
CONV_K = 3
GLA_HEADS = 4
GLA_DK = 128
GLA_DV = 256
GLA_RANK = 16
GLA_TAU = 16.0
GLA_CHUNK = 64
GLA_QK = GLA_HEADS * GLA_DK
GLA_V = GLA_HEADS * GLA_DV
NA_HEADS = 8
NA_DH = 128
NA_W = NA_HEADS * NA_DH
WIN_R = 8
WIN_C = 16
D_FF = 5632
N_EXPERTS = 8
TOP_K = 2
ROPE_BASE = 10000.0
LN_EPS = 1e-5
N_BRANCH = 3
DEEPNORM_ALPHA = (2 * DEPTH) ** 0.25

NL = BATCH * SEQ
NC = BATCH * CTX_LEN
NT = NL + NC
GRID_ROWS = SEQ // GRID_W

BF = jnp.bfloat16
F32 = jnp.float32

V7X_VMEM_BYTES = 64 * 1024 * 1024
VMEM_LIMIT = V7X_VMEM_BYTES - 4 * 1024 * 1024
NEG_INF = -1e30

U_CONV_B, U_CONV_C, U_CONV_X = 0, 1024, 2048
U_GLA_Q, U_GLA_K, U_GLA_V, U_GLA_R = 3072, 3584, 4096, 5120
U_COLS = 6144
UN_Q, UN_K, UN_V = 0, 1024, 2048
UN_COLS = 3072
W_IN_LR = U_COLS
W_IN_NA = W_IN_LR + 2 * GLA_RANK
W_IN_GATES = W_IN_NA + UN_COLS
LR_COLS = 128

NA_QROWS = 4
NA_KROWS = NA_QROWS + WIN_R
NA_QBLK = NA_QROWS * GRID_W
NA_KBLK = NA_KROWS * GRID_W
NA_KPAIRS = NA_KROWS // 2
NA_BIAS_DR = 2 * WIN_R - 1
NA_PAIR_BOTH = 0
NA_PAIR_LEFT = NA_PAIR_BOTH + NA_BIAS_DR - 1
NA_PAIR_RIGHT = NA_PAIR_LEFT + NA_BIAS_DR
NA_PAIR_NONE = NA_PAIR_RIGHT + NA_BIAS_DR
NA_PAIR_ENTRIES = NA_PAIR_NONE + 1
NA_NBLK = GRID_ROWS // NA_QROWS

MOE_GB = 256
MOE_SM = 9 * MOE_GB
MOE_CH = 4 * MOE_GB
MOE_TF = 256
MOE_NST = (NL * TOP_K) // MOE_SM + N_EXPERTS
MOE_NSLOT = MOE_NST * MOE_SM
MOE_NGB = MOE_NSLOT // MOE_GB
MOE_NTT = NL // MOE_GB
MOE_CT = 1024
MOE_NCT = NL // MOE_CT
MOE_NPAIR = (NL * TOP_K) // MOE_GB + N_EXPERTS + N_EXPERTS * (MOE_NCT - 1)
MOE_EPILOGUE_ROWS = 128
ROUTER_LANES = 128


def _params(*sem):
    return pltpu.CompilerParams(dimension_semantics=sem, vmem_limit_bytes=VMEM_LIMIT)


def _dot(a, b):
    return jnp.dot(a, b, preferred_element_type=F32)


def _dot_nt(a, b):
    return lax.dot_general(a, b, (((1,), (1,)), ((), ())), preferred_element_type=F32)


def _dot_tn(a, b):
    return lax.dot_general(a, b, (((0,), (0,)), ((), ())), preferred_element_type=F32)


def _silu(x):
    return x * jax.nn.sigmoid(x)


def _layer_norm(x, g, b):
    mu = jnp.mean(x, axis=-1, keepdims=True)
    xc = x - mu
    var = jnp.mean(xc * xc, axis=-1, keepdims=True)
    return xc * lax.rsqrt(var + LN_EPS) * g + b


def _mod_row_of_tile(tile_rows):
    per_batch = SEQ // tile_rows
    return lambda i: jnp.minimum(i // per_batch, BATCH)


def _mod_spec(layer, k, row_of_tile):
    return pl.BlockSpec((None, 1, D_MODEL),
                        lambda i, *_: (layer * 48 + row_of_tile(i) * 6 + k, 0, 0))


def _ada_kernel(c_ref, w_ref, b_ref, o_ref):
    a = _silu(c_ref[...]).astype(BF)
    o_ref[...] = _dot(a, w_ref[...].astype(BF)) + b_ref[...]


def _ada_table(cvec, ada_w, ada_b):
    tn = 1024
    out = pl.pallas_call(
        _ada_kernel,
        out_shape=jax.ShapeDtypeStruct((DEPTH, 8, 6 * D_MODEL), F32),
        grid=(DEPTH, 6 * D_MODEL // tn),
        in_specs=[pl.BlockSpec((8, D_MODEL), lambda l, j: (0, 0)),
                  pl.BlockSpec((None, D_MODEL, tn), lambda l, j: (l, 0, j)),
                  pl.BlockSpec((None, 1, tn), lambda l, j: (l, 0, j))],
        out_specs=pl.BlockSpec((None, 8, tn), lambda l, j: (l, 0, j)),
        compiler_params=_params("parallel", "parallel"),
        name="ada_table",
    )(cvec, ada_w, ada_b.reshape(DEPTH, 1, 6 * D_MODEL))
    return out.reshape(DEPTH * 8 * 6, 1, D_MODEL)


def _select_rows(i, n_lat_tiles, xl_ref, xc_ref):
    return jnp.where(i < n_lat_tiles, xl_ref[...], xc_ref[...])


def _split_row_specs(tm):
    n_lat = NL // tm
    return (pl.BlockSpec((tm, D_MODEL), lambda i: (jnp.minimum(i, n_lat - 1), 0)),
            pl.BlockSpec((tm, D_MODEL), lambda i: (jnp.maximum(i - n_lat, 0), 0)))


def _modulate_kernel(xl_ref, xc_ref, sh_ref, sc_ref, o_ref, *, n_lat_tiles):
    x = _select_rows(pl.program_id(0), n_lat_tiles, xl_ref, xc_ref)
    o_ref[...] = (x * (1.0 + sc_ref[...]) + sh_ref[...]).astype(BF)


def _modulate(x_lat, x_ctx, mod, layer):
    tm = 512
    rot = _mod_row_of_tile(tm)
    return pl.pallas_call(
        functools.partial(_modulate_kernel, n_lat_tiles=NL // tm),
        out_shape=jax.ShapeDtypeStruct((NT, D_MODEL), BF),
        grid=(NT // tm,),
        in_specs=[*_split_row_specs(tm), _mod_spec(layer, 0, rot), _mod_spec(layer, 1, rot)],
        out_specs=pl.BlockSpec((tm, D_MODEL), lambda i: (i, 0)),
        compiler_params=_params("parallel"),
        name="modulate",
    )(x_lat, x_ctx, mod, mod)


PROJ_TM = NT // 8


def _mm_wt_kernel(a_ref, wt_ref, o_ref, wb_ref, *, first_tile_scale):
    @pl.when(pl.program_id(1) == 0)
    def _():
        wb_ref[...] = wt_ref[0].astype(BF)

    acc = _dot_nt(a_ref[...], wb_ref[...])
    if first_tile_scale is not None:
        acc = acc * jnp.where(pl.program_id(0) == 0, first_tile_scale, 1.0)
    o_ref[...] = acc.astype(o_ref.dtype)


def _project(a, wt_stack, layer, col0, n, tm, tn, name, out_dtype=F32, first_tile_scale=None):
    m, k = a.shape
    assert col0 % 8 == 0
    return pl.pallas_call(
        functools.partial(_mm_wt_kernel, first_tile_scale=first_tile_scale),
        out_shape=jax.ShapeDtypeStruct((m, n), out_dtype),
        grid=(n // tn, m // tm),
        in_specs=[pl.BlockSpec((tm, k), lambda j, i: (i, 0)),
                  pl.BlockSpec((pl.Element(1), pl.Element(tn), pl.Element(k)),
                               lambda j, i: (layer, pl.multiple_of(col0 + j * tn, 8), 0))],
        out_specs=pl.BlockSpec((tm, tn), lambda j, i: (i, j)),
        scratch_shapes=[pltpu.VMEM((tn, k), BF)],
        compiler_params=_params("arbitrary", "arbitrary"),
        name=name,
    )(a, wt_stack)


CONV_TM = 256


def _conv_kernel(b_ref, c_ref, x_ref, cp_ref, xp_ref, cn_ref, xn_ref, w_ref, o_ref):
    i = pl.program_id(0)
    tiles_per_seq = SEQ // CONV_TM
    is_ctx = i >= NL // CONV_TM
    is_start = jnp.logical_or(i % tiles_per_seq == 0, is_ctx)
    is_end = jnp.logical_or(i % tiles_per_seq == tiles_per_seq - 1, is_ctx)
    z = c_ref[...] * x_ref[...]
    zp = cp_ref[...] * xp_ref[...]
    zn = cn_ref[...] * xn_ref[...]
    prev_row = jnp.where(is_start, 0.0, zp[7:8, :])
    next_row = jnp.where(is_end, 0.0, zn[0:1, :])
    rows = lax.broadcasted_iota(jnp.int32, z.shape, 0)
    z_prev = jnp.where(rows == 0, prev_row, pltpu.roll(z, 1, 0))
    z_next = jnp.where(rows == CONV_TM - 1, next_row, pltpu.roll(z, CONV_TM - 1, 0))
    w = w_ref[...]
    y = w[0:1, :] * z_prev + w[1:2, :] * z + w[2:3, :] * z_next
    o_ref[...] = (b_ref[...] * y).astype(BF)


def _short_conv(u, conv_w, layer):
    tm = CONV_TM
    n8 = tm // 8
    last8 = NT // 8 - 1
    return pl.pallas_call(
        _conv_kernel,
        out_shape=jax.ShapeDtypeStruct((NT, CONV_DIM), BF),
        grid=(NT // tm,),
        in_specs=[pl.BlockSpec((tm, CONV_DIM), lambda i: (i, U_CONV_B // CONV_DIM)),
                  pl.BlockSpec((tm, CONV_DIM), lambda i: (i, U_CONV_C // CONV_DIM)),
                  pl.BlockSpec((tm, CONV_DIM), lambda i: (i, U_CONV_X // CONV_DIM)),
                  pl.BlockSpec((8, CONV_DIM), lambda i: (jnp.maximum(i * n8 - 1, 0), U_CONV_C // CONV_DIM)),
                  pl.BlockSpec((8, CONV_DIM), lambda i: (jnp.maximum(i * n8 - 1, 0), U_CONV_X // CONV_DIM)),
                  pl.BlockSpec((8, CONV_DIM), lambda i: (jnp.minimum(i * n8 + n8, last8), U_CONV_C // CONV_DIM)),
                  pl.BlockSpec((8, CONV_DIM), lambda i: (jnp.minimum(i * n8 + n8, last8), U_CONV_X // CONV_DIM)),
                  pl.BlockSpec((None, CONV_K, CONV_DIM), lambda i: (layer, 0, 0))],
        out_specs=pl.BlockSpec((tm, CONV_DIM), lambda i: (i, 0)),
        compiler_params=_params("parallel"),
        name="short_conv",
    )(u, u, u, u, u, u, u, conv_w)


GLA_STEPS = (CTX_LEN + SEQ) // GLA_CHUNK
GLA_CTX_STEPS = CTX_LEN // GLA_CHUNK


def _log_sigmoid(z):
    return -(jnp.maximum(-z, 0.0) + jnp.log(1.0 + jnp.exp(-jnp.abs(z))))


def _rope(x, cs, sn):
    lane = lax.broadcasted_iota(jnp.int32, x.shape, 1)
    swapped = jnp.where(lane % 2 == 0, pltpu.roll(x, GLA_DK - 1, 1), pltpu.roll(x, 1, 1))
    return x * cs + swapped * sn


def _gla_decayed_operands(q_ref, k_ref, l_ref, r_ref, w_ref, b_ref, reverse):
    c = GLA_CHUNK
    z = _dot(l_ref[...].astype(BF), w_ref[...]) + b_ref[...]
    g = _log_sigmoid(z) * (1.0 / GLA_TAU)
    t_i = lax.broadcasted_iota(jnp.int32, (c, c), 0)
    s_i = lax.broadcasted_iota(jnp.int32, (c, c), 1)
    keep = (s_i >= t_i) if reverse else (s_i <= t_i)
    tri = jnp.where(keep, 1.0, 0.0).astype(BF)
    g1 = g.astype(BF)
    r1 = g - g1.astype(F32)
    g2 = r1.astype(BF)
    g3 = (r1 - g2.astype(F32)).astype(BF)
    b = _dot(tri, g1) + _dot(tri, g2) + _dot(tri, g3)
    total = b[0:1, :] if reverse else b[c - 1:c, :]
    cs = r_ref[:, :GLA_DK]
    sn = r_ref[:, GLA_DK:]
    heads = [slice(h * GLA_DK, (h + 1) * GLA_DK) for h in range(GLA_HEADS)]
    qs = jnp.concatenate([_rope(q_ref[:, hd] * (GLA_DK ** -0.5), cs, sn) for hd in heads], axis=1)
    kr = jnp.concatenate([_rope(k_ref[:, hd], cs, sn) for hd in heads], axis=1)
    q_dec = (qs * jnp.exp(b)).astype(BF)
    k_inv = (kr * jnp.exp(-b)).astype(BF)
    k_end = (kr * jnp.exp(total - b)).astype(BF)
    return keep, q_dec, k_inv, k_end, jnp.exp(total)


GLA_STREAM_REFS = 4 * BATCH + 1


def _gla_kernel(*refs):
    fwd, bwd = refs[:GLA_STREAM_REFS], refs[GLA_STREAM_REFS:2 * GLA_STREAM_REFS]
    wf_ref, bf_ref, wb_ref, bb_ref, of_ref, ob_ref, stf_ref, stb_ref = refs[2 * GLA_STREAM_REFS:]

    @pl.when(pl.program_id(0) == 0)
    def _():
        stf_ref[...] = jnp.zeros_like(stf_ref)
        stb_ref[...] = jnp.zeros_like(stb_ref)

    def stream_refs(stream, b):
        return stream[4 * b:4 * b + 4]

    pre, v_refs = {}, {}
    for d, (stream, w_ref, b_ref, reverse) in enumerate(((fwd, wf_ref, bf_ref, False), (bwd, wb_ref, bb_ref, True))):
        for b in range(BATCH):
            q_ref, k_ref, v_ref, l_ref = stream_refs(stream, b)
            pre[d, b] = _gla_decayed_operands(q_ref, k_ref, l_ref, stream[-1], w_ref, b_ref, reverse)
            v_refs[d, b] = v_ref
    chains = [(d, b, h) for d in range(2) for b in range(BATCH) for h in range(GLA_HEADS)]
    o_refs, st_refs = (of_ref, ob_ref), (stf_ref, stb_ref)

    def head(x, h):
        return x[:, h * GLA_DK:(h + 1) * GLA_DK]

    v = [v_refs[d, b][:, h * GLA_DV:(h + 1) * GLA_DV].astype(BF) for d, b, h in chains]
    st = [st_refs[d][b, h] for d, b, h in chains]
    att = [jnp.where(pre[d, b][0], _dot_nt(head(pre[d, b][1], h), head(pre[d, b][2], h)), 0.0).astype(BF)
           for d, b, h in chains]
    o = [_dot(att[i], v[i]) + _dot_nt(head(pre[d, b][1], h), st[i].astype(BF))
         for i, (d, b, h) in enumerate(chains)]
    for i in range(0, len(chains), GLA_HEADS):
        d, b, _ = chains[i]
        o_refs[d][b] = jnp.concatenate(o[i:i + GLA_HEADS], axis=1)
    for i, (d, b, h) in enumerate(chains):
        st_refs[d][b, h] = st[i] * head(pre[d, b][4], h) + _dot_tn(v[i], head(pre[d, b][3], h))


def _gla(u, lr, rope_tab, wpad_f, bg_f, wpad_b, bg_b):
    c = GLA_CHUNK
    lat_chunks = SEQ // c
    ctx_base = NL // c

    def row_f(b, s):
        return jnp.where(s < GLA_CTX_STEPS, ctx_base + b * GLA_CTX_STEPS + s, b * lat_chunks + s - GLA_CTX_STEPS)

    def row_b(b, s):
        return jnp.where(s < GLA_CTX_STEPS, ctx_base + b * GLA_CTX_STEPS + GLA_CTX_STEPS - 1 - s,
                         b * lat_chunks + GLA_STEPS - 1 - s)

    def pos_f(s):
        return jnp.where(s < GLA_CTX_STEPS, lat_chunks + s, s - GLA_CTX_STEPS)

    def pos_b(s):
        return jnp.where(s < GLA_CTX_STEPS, lat_chunks + GLA_CTX_STEPS - 1 - s, GLA_STEPS - 1 - s)

    def rope_chunk(pos):
        return jnp.minimum(pos, lat_chunks)

    def stream(row, pos):
        specs, args = [], []
        for b in range(BATCH):
            specs += [pl.BlockSpec((c, GLA_QK), lambda s, b=b: (row(b, s), U_GLA_Q // GLA_QK)),
                      pl.BlockSpec((c, GLA_QK), lambda s, b=b: (row(b, s), U_GLA_K // GLA_QK)),
                      pl.BlockSpec((c, GLA_V), lambda s, b=b: (row(b, s), U_GLA_V // GLA_V)),
                      pl.BlockSpec((c, LR_COLS), lambda s, b=b: (row(b, s), 0))]
            args += [u, u, u, lr]
        specs.append(pl.BlockSpec((c, 2 * GLA_DK), lambda s: (rope_chunk(pos(s)), 0)))
        args.append(rope_tab)
        return specs, args

    specs_f, args_f = stream(row_f, pos_f)
    specs_b, args_b = stream(row_b, pos_b)
    head_w = pl.BlockSpec((LR_COLS, GLA_QK), lambda s: (0, 0))
    head_b = pl.BlockSpec((1, GLA_QK), lambda s: (0, 0))
    state = pltpu.VMEM((BATCH, GLA_HEADS, GLA_DV, GLA_DK), F32)
    out = jax.ShapeDtypeStruct((BATCH, SEQ + CTX_LEN, GLA_V), F32)
    return pl.pallas_call(
        _gla_kernel,
        out_shape=(out, out),
        grid=(GLA_STEPS,),
        in_specs=specs_f + specs_b + [head_w, head_b, head_w, head_b],
        out_specs=(pl.BlockSpec((BATCH, c, GLA_V), lambda s: (0, pos_f(s), 0)),
                   pl.BlockSpec((BATCH, c, GLA_V), lambda s: (0, pos_b(s), 0))),
        scratch_shapes=[state, state],
        compiler_params=_params("arbitrary"),
        name="gla",
    )(*args_f, *args_b, wpad_f, bg_f, wpad_b, bg_b)


def _rope_table():
    t = jnp.arange(SEQ)
    rows = (t // GRID_W).astype(F32)
    cols = (t % GRID_W).astype(F32)
    n_freq = GLA_DK // 4
    inv = 1.0 / (ROPE_BASE ** (jnp.arange(n_freq, dtype=F32) / n_freq))
    ang = jnp.concatenate([rows[:, None] * inv, cols[:, None] * inv], -1)
    cos = jnp.repeat(jnp.cos(ang), 2, axis=-1)
    sin = jnp.repeat(jnp.sin(ang), 2, axis=-1)
    sign = jnp.tile(jnp.array([-1.0, 1.0], F32), GLA_DK // 2)
    cos = jnp.concatenate([cos, jnp.ones((GRID_W, GLA_DK), F32)], 0)
    sin = jnp.concatenate([sin * sign, jnp.zeros((GRID_W, GLA_DK), F32)], 0)
    return jnp.concatenate([cos, sin], axis=1)


def _decay_weights(wg, bg, lane_offset):
    w = jnp.pad(wg, ((lane_offset, LR_COLS - GLA_RANK - lane_offset), (0, 0))).astype(BF)
    return w, bg.reshape(1, GLA_QK)


NA_HPS = 4


def _na_softmax_out(parts):
    m = functools.reduce(jnp.maximum, [jnp.max(s, axis=-1, keepdims=True) for s, _ in parts])
    ps = [jnp.exp(s - m) for s, _ in parts]
    l = functools.reduce(jnp.add, [jnp.sum(p, axis=-1, keepdims=True) for p in ps])
    o = functools.reduce(jnp.add, [_dot(p.astype(BF), v) for p, (_, v) in zip(ps, parts)])
    return (o / l).astype(BF)


def _na_kernel(q_ref, k_ref, v_ref, kc_ref, vc_ref, bias_ref, o_ref):
    j = pl.program_id(2)
    head_cols = [slice(h * NA_DH, (h + 1) * NA_DH) for h in range(NA_HPS)]

    @pl.when(j < NA_NBLK)
    def _():
        base = jnp.clip(NA_QROWS * j - WIN_R // 2, 0, GRID_ROWS - NA_KROWS)
        start = pl.multiple_of(base * GRID_W, GRID_W)

        def pair_entry(qr, kp):
            r = NA_QROWS * j + qr
            rs = jnp.clip(r - WIN_R // 2, 0, GRID_ROWS - WIN_R)
            a0 = base + 2 * kp
            dr0 = a0 - r + WIN_R - 1
            in0 = jnp.logical_and(a0 >= rs, a0 < rs + WIN_R)
            in1 = jnp.logical_and(a0 + 1 >= rs, a0 + 1 < rs + WIN_R)
            entry = jnp.where(jnp.logical_and(in0, in1), NA_PAIR_BOTH + dr0,
                              jnp.where(in0, NA_PAIR_LEFT + dr0,
                                        jnp.where(in1, NA_PAIR_RIGHT + dr0 + 1, NA_PAIR_NONE)))
            return jnp.clip(entry, 0, NA_PAIR_ENTRIES - 1)

        entries = [[pair_entry(qr, kp) for kp in range(NA_KPAIRS)] for qr in range(NA_QROWS)]
        for h, hc in enumerate(head_cols):
            q = q_ref[:, hc]
            vl = v_ref[pl.ds(start, NA_KBLK), hc]
            bias = jnp.concatenate(
                [jnp.concatenate([bias_ref[h, entries[qr][kp]] for kp in range(NA_KPAIRS)], axis=1)
                 for qr in range(NA_QROWS)], axis=0)
            s_loc = _dot_nt(q, k_ref[pl.ds(start, NA_KBLK), hc]) + bias
            s_ctx = _dot_nt(q, kc_ref[:, hc])
            o_ref[:, hc] = _na_softmax_out([(s_loc, vl), (s_ctx, vc_ref[:, hc])])

    @pl.when(j == NA_NBLK)
    def _():
        for hc in head_cols:
            s_ctx = _dot_nt(q_ref[:, hc], kc_ref[:, hc])
            o_ref[:, hc] = _na_softmax_out([(s_ctx, vc_ref[:, hc])])


def _na_bias_pair_tables(rpb_all):
    qc = np.arange(GRID_W)
    cs = np.clip(qc - WIN_C // 2, 0, GRID_W - WIN_C)
    kc = np.arange(GRID_W)
    col_ok = (kc[None, :] >= cs[:, None]) & (kc[None, :] < cs[:, None] + WIN_C)
    dc = kc[None, :] - qc[:, None] + WIN_C - 1
    col_sel = ((dc[None] == np.arange(2 * WIN_C - 1)[:, None, None]) & col_ok[None]).astype(np.float32)
    t = jnp.einsum('lhrd,dxy->lhrxy', rpb_all, jnp.asarray(col_sel), precision=lax.Precision.HIGHEST)
    t = jnp.where(jnp.asarray(col_ok), t, NEG_INF)
    masked = jnp.full_like(t, NEG_INF)
    both = jnp.concatenate([t[:, :, :-1], t[:, :, 1:]], axis=-1)
    left = jnp.concatenate([t, masked], axis=-1)
    right = jnp.concatenate([masked, t], axis=-1)
    none = jnp.concatenate([masked[:, :, :1], masked[:, :, :1]], axis=-1)
    return jnp.concatenate([both, left, right, none], axis=2)


def _neighbourhood_attention(un, bias_pairs, layer, with_ctx):
    steps = NA_NBLK + (1 if with_ctx else 0)
    ctx_q = NL // NA_QBLK
    w = NA_HPS * NA_DH

    def q_row(b, j):
        return jnp.where(j < NA_NBLK, b * NA_NBLK + j, ctx_q + b)

    ctx_blk = NL // CTX_LEN
    return pl.pallas_call(
        _na_kernel,
        out_shape=jax.ShapeDtypeStruct((NT, NA_W), BF),
        grid=(BATCH, NA_HEADS // NA_HPS, steps),
        in_specs=[pl.BlockSpec((NA_QBLK, w), lambda b, h, j: (q_row(b, j), UN_Q // w + h)),
                  pl.BlockSpec((SEQ, w), lambda b, h, j: (b, UN_K // w + h)),
                  pl.BlockSpec((SEQ, w), lambda b, h, j: (b, UN_V // w + h)),
                  pl.BlockSpec((CTX_LEN, w), lambda b, h, j: (ctx_blk + b, UN_K // w + h)),
                  pl.BlockSpec((CTX_LEN, w), lambda b, h, j: (ctx_blk + b, UN_V // w + h)),
                  pl.BlockSpec((None, NA_HPS, NA_PAIR_ENTRIES, GRID_W, 2 * GRID_W),
                               lambda b, h, j: (layer, h, 0, 0, 0))],
        out_specs=pl.BlockSpec((NA_QBLK, w), lambda b, h, j: (q_row(b, j), h)),
        compiler_params=_params("parallel", "parallel", "arbitrary"),
        name="neighbourhood_attention",
    )(un, un, un, un, un, bias_pairs)


MERGE_TM = 256


def _merge_kernel(yc_ref, of_ref, ob_ref, r_ref, yn_ref, gt_ref, xl_ref, xc_ref,
                  ng_ref, gb_ref, wc_ref, wg_ref, wn_ref, wo_ref,
                  g1_ref, lg_ref, lb_ref, sh2_ref, sc2_ref, xo_ref, ho_ref):
    x = _select_rows(pl.program_id(0), NL // MERGE_TM, xl_ref, xc_ref)
    o = of_ref[...] + ob_ref[...]
    r = r_ref[...]
    ng = ng_ref[...]
    heads = []
    for h in range(GLA_HEADS):
        oh = o[:, h * GLA_DV:(h + 1) * GLA_DV]
        oh = oh * lax.rsqrt(jnp.mean(oh * oh, axis=-1, keepdims=True) + LN_EPS)
        heads.append(oh * ng * _silu(r[:, h * GLA_DV:(h + 1) * GLA_DV]))
    y_gla = jnp.concatenate(heads, axis=-1).astype(BF)
    g = jax.nn.sigmoid(gt_ref[...] + gb_ref[...])
    z = (g[:, :D_MODEL] * _dot(yc_ref[...], wc_ref[...])
         + g[:, D_MODEL:2 * D_MODEL] * _dot(y_gla, wg_ref[...])
         + g[:, 2 * D_MODEL:] * _dot(yn_ref[...], wn_ref[...]))
    y = _dot(z.astype(BF), wo_ref[...])
    xn = _layer_norm(DEEPNORM_ALPHA * x + g1_ref[...] * y, lg_ref[...], lb_ref[...])
    xo_ref[...] = xn
    ho_ref[...] = (xn * (1.0 + sc2_ref[...]) + sh2_ref[...]).astype(BF)


def _merge(y_conv, o_f, o_b, u, ug, y_na, x_lat, x_ctx, mod, layer, rows, norm_g, gate_b, w_conv, w_gla, w_na,
           w_out, ln_g, ln_b):
    tm = MERGE_TM
    rot = _mod_row_of_tile(tm)
    row = lambda width: pl.BlockSpec((tm, width), lambda i: (i, 0))
    const = lambda shape: pl.BlockSpec(shape, lambda i: (0,) * len(shape), pipeline_mode=pl.Buffered(1))
    lat_tiles, seq_tiles = NL // tm, SEQ // tm
    gla_rows = pl.BlockSpec((None, tm, GLA_V),
                            lambda i: (jnp.where(i < lat_tiles, i // seq_tiles, i - lat_tiles),
                                       jnp.where(i < lat_tiles, i % seq_tiles, seq_tiles), 0))
    assert CTX_LEN == tm
    return pl.pallas_call(
        _merge_kernel,
        out_shape=(jax.ShapeDtypeStruct((rows, D_MODEL), F32), jax.ShapeDtypeStruct((rows, D_MODEL), BF)),
        grid=(rows // tm,),
        in_specs=[row(CONV_DIM), gla_rows, gla_rows,
                  pl.BlockSpec((tm, GLA_V), lambda i: (i, U_GLA_R // GLA_V)),
                  row(NA_W), row(N_BRANCH * D_MODEL), *_split_row_specs(tm),
                  const((1, GLA_DV)), const((1, N_BRANCH * D_MODEL)),
                  const((CONV_DIM, D_MODEL)), const((GLA_V, D_MODEL)), const((NA_W, D_MODEL)),
                  const((D_MODEL, D_MODEL)),
                  _mod_spec(layer, 2, rot), const((1, D_MODEL)), const((1, D_MODEL)),
                  _mod_spec(layer, 3, rot), _mod_spec(layer, 4, rot)],
        out_specs=(row(D_MODEL), row(D_MODEL)),
        compiler_params=_params("parallel"),
        name="merge",
    )(y_conv, o_f, o_b, u, y_na, ug, x_lat, x_ctx, norm_g, gate_b, w_conv, w_gla, w_na, w_out,
      mod, ln_g, ln_b, mod, mod)


FFN_TM_LATENT = 1024
FFN_TM_CONTEXT = NC
FFN_TF = 256
FFN_EPILOGUE_ROWS = 128


def _ffn_kernel(h_ref, wg_ref, wu_ref, wd_ref, x_ref, g2_ref, lg_ref, lb_ref, shn_ref, scn_ref, *refs):
    xo_ref, ho_ref, acc_ref = refs[-3:]
    j = pl.program_id(1)

    @pl.when(j == 0)
    def _():
        acc_ref[...] = jnp.zeros_like(acc_ref)

    h = h_ref[...]
    a = (_silu(_dot(h, wg_ref[...].astype(BF))) * _dot(h, wu_ref[...].astype(BF))).astype(BF)
    acc_ref[...] += _dot(a, wd_ref[...].astype(BF))

    @pl.when(j == pl.num_programs(1) - 1)
    def _():
        def norm_rows(c, carry):
            rows = pl.ds(pl.multiple_of(c * FFN_EPILOGUE_ROWS, FFN_EPILOGUE_ROWS), FFN_EPILOGUE_ROWS)
            xn = _layer_norm(DEEPNORM_ALPHA * x_ref[rows, :] + g2_ref[...] * acc_ref[rows, :],
                             lg_ref[...], lb_ref[...])
            xo_ref[rows, :] = xn
            ho_ref[rows, :] = (xn * (1.0 + scn_ref[...]) + shn_ref[...]).astype(BF)
            return carry

        lax.fori_loop(0, acc_ref.shape[0] // FFN_EPILOGUE_ROWS, norm_rows, 0)


def _dense_ffn_rows(h2, x_all, mod, layer, wg, wu, wd, ffn_layer, ln_g, ln_b, row0, n_rows, tm, prev):
    tile0 = row0 // tm
    rot = _mod_row_of_tile(tm)
    mod_spec = lambda lyr, k: pl.BlockSpec((None, 1, D_MODEL),
                                           lambda i, j: (lyr * 48 + rot(tile0 + i) * 6 + k, 0, 0))
    row = lambda **kw: pl.BlockSpec((tm, D_MODEL), lambda i, j: (tile0 + i, 0), **kw)
    const = lambda: pl.BlockSpec((1, D_MODEL), lambda i, j: (0, 0))
    once = dict(pipeline_mode=pl.Buffered(1))
    in_specs = [row(),
                pl.BlockSpec((None, D_MODEL, FFN_TF), lambda i, j: (ffn_layer, 0, j)),
                pl.BlockSpec((None, D_MODEL, FFN_TF), lambda i, j: (ffn_layer, 0, j)),
                pl.BlockSpec((None, FFN_TF, D_MODEL), lambda i, j: (ffn_layer, j, 0)),
                row(**once), mod_spec(layer, 5), const(), const(),
                mod_spec(layer + 1, 0), mod_spec(layer + 1, 1)]
    args = [h2, wg, wu, wd, x_all, mod, ln_g, ln_b, mod, mod]
    aliases = {}
    if prev is not None:
        aliases = {len(args): 0, len(args) + 1: 1}
        in_specs += [pl.BlockSpec(memory_space=pl.ANY), pl.BlockSpec(memory_space=pl.ANY)]
        args += list(prev)
    return pl.pallas_call(
        _ffn_kernel,
        out_shape=(jax.ShapeDtypeStruct((NT, D_MODEL), F32), jax.ShapeDtypeStruct((NT, D_MODEL), BF)),
        grid=(n_rows // tm, D_FF // FFN_TF),
        in_specs=in_specs,
        out_specs=(row(**once), row(**once)),
        scratch_shapes=[pltpu.VMEM((tm, D_MODEL), F32)],
        input_output_aliases=aliases,
        compiler_params=_params("parallel", "arbitrary"),
        name="dense_ffn",
    )(*args)


def _dense_ffn(h2, x_all, mod, layer, wg, wu, wd, ffn_layer, ln_g, ln_b):
    out = _dense_ffn_rows(h2, x_all, mod, layer, wg, wu, wd, ffn_layer, ln_g, ln_b, 0, NL, FFN_TM_LATENT, None)
    return _dense_ffn_rows(h2, x_all, mod, layer, wg, wu, wd, ffn_layer, ln_g, ln_b, NL, NC, FFN_TM_CONTEXT, out)


def _router_kernel(h_ref, w_ref, idx_ref, wt_ref):
    logits = _dot(h_ref[...], w_ref[...])
    lane = lax.broadcasted_iota(jnp.int32, logits.shape, 1)
    logits = jnp.where(lane < N_EXPERTS, logits, -jnp.inf)
    m1 = jnp.max(logits, axis=-1, keepdims=True)
    i1 = jnp.min(jnp.where(logits == m1, lane, ROUTER_LANES), axis=-1, keepdims=True)
    rest = jnp.where(lane == i1, -jnp.inf, logits)
    m2 = jnp.max(rest, axis=-1, keepdims=True)
    i2 = jnp.min(jnp.where(rest == m2, lane, ROUTER_LANES), axis=-1, keepdims=True)
    e2 = jnp.exp(m2 - m1)
    w1 = 1.0 / (1.0 + e2)
    w2 = e2 / (1.0 + e2)
    idx_ref[...] = jnp.where(lane == 0, i1, jnp.where(lane == 1, i2, 0))
    wt_ref[...] = jnp.where(lane == 0, w1, jnp.where(lane == 1, w2, 0.0))


def _router(h2, router_w):
    tm = 512
    return pl.pallas_call(
        _router_kernel,
        out_shape=(jax.ShapeDtypeStruct((NL, ROUTER_LANES), jnp.int32),
                   jax.ShapeDtypeStruct((NL, ROUTER_LANES), F32)),
        grid=(NL // tm,),
        in_specs=[pl.BlockSpec((tm, D_MODEL), lambda i: (i, 0)),
                  pl.BlockSpec((D_MODEL, ROUTER_LANES), lambda i: (0, 0))],
        out_specs=(pl.BlockSpec((tm, ROUTER_LANES), lambda i: (i, 0)),
                   pl.BlockSpec((tm, ROUTER_LANES), lambda i: (i, 0))),
        compiler_params=_params("parallel"),
        name="moe_router",
    )(h2, router_w)


def _dispatch_plan(idx):
    i32 = jnp.int32
    experts = jnp.arange(N_EXPERTS, dtype=i32)
    oh_a = (idx[:, 0:1] == experts).astype(i32)
    oh_b = (idx[:, 1:2] == experts).astype(i32)
    cum = jnp.cumsum(oh_a + oh_b, axis=0)
    counts = cum[-1]
    n_st = (counts + MOE_SM - 1) // MOE_SM
    st_end = jnp.cumsum(n_st)
    st_start = st_end - n_st
    g_start = st_start * MOE_SM
    g_end = st_end * MOE_SM
    slot_table = g_start[None, :] + cum - 1
    slot_a = jnp.sum(oh_a * slot_table, axis=1)
    slot_b = jnp.sum(oh_b * slot_table, axis=1)

    n_used = st_end[-1]
    st_ids = jnp.minimum(jnp.arange(MOE_NST, dtype=i32), n_used - 1)
    st_expert = jnp.minimum(jnp.sum((st_ids[:, None] >= st_end[None, :]).astype(i32), axis=1), N_EXPERTS - 1)
    st_left = counts[st_expert] - (st_ids - st_start[st_expert]) * MOE_SM
    st_rows = ((jnp.clip(st_left, 0, MOE_SM) + MOE_GB - 1) // MOE_GB) * MOE_GB
    tile_plan = (st_expert, st_rows, n_used.reshape(1))

    pair_ids = jnp.arange(MOE_NPAIR, dtype=i32)
    tile_last = jnp.arange(MOE_NTT, dtype=i32) * MOE_GB + MOE_GB - 1
    hi = cum[tile_last]
    lo = jnp.concatenate([jnp.zeros((1, N_EXPERTS), i32), hi[:-1]], axis=0)

    blk_ids = jnp.arange(MOE_NGB, dtype=i32)
    blk_start = blk_ids * MOE_GB
    blk_e = jnp.minimum(jnp.sum((blk_start[:, None] >= g_end[None, :]).astype(i32), axis=1), N_EXPERTS - 1)
    r0 = blk_start - g_start[blk_e]
    r1 = jnp.minimum(r0 + MOE_GB, counts[blk_e])
    has = (blk_start < g_end[-1]) & (r1 > r0)
    hi_e = hi[:, blk_e]
    t_lo = jnp.sum((hi_e <= r0[None, :]).astype(i32), axis=0)
    t_hi = jnp.sum((hi_e < r1[None, :]).astype(i32), axis=0)
    t_lo = jnp.where(has, t_lo, 0)
    t_end = jnp.where(has, t_hi + 1, 0)
    blk_map = lax.cummax(jnp.where(has, blk_ids, 0), axis=0)
    dispatch = (t_lo, t_end, blk_map, has.astype(i32))

    c_hi = cum[jnp.arange(MOE_NCT, dtype=i32) * MOE_CT + MOE_CT - 1]
    c_lo = jnp.concatenate([jnp.zeros((1, N_EXPERTS), i32), c_hi[:-1]], axis=0)
    b_lo = (g_start[None, :] + c_lo) // MOE_GB
    b_hi = (g_start[None, :] + c_hi - 1) // MOE_GB
    n_c = jnp.where(c_hi > c_lo, b_hi - b_lo + 1, 0).reshape(-1)
    b_lo = b_lo.reshape(-1)
    c_end = jnp.cumsum(n_c)
    c_total = c_end[-1]
    c_idx = jnp.minimum(jnp.sum((pair_ids[:, None] >= c_end[None, :]).astype(i32), axis=1),
                        MOE_NCT * N_EXPERTS - 1)
    c_blk = jnp.clip(b_lo[c_idx] + pair_ids - (c_end[c_idx] - n_c[c_idx]), 0, MOE_NGB - 1)
    c_tile = c_idx // N_EXPERTS
    c_valid = pair_ids < c_total
    c_tile = jnp.where(c_valid, c_tile, MOE_NCT - 1)
    c_blk = jnp.where(c_valid, c_blk, c_blk[jnp.maximum(c_total - 1, 0)])
    c_prev = jnp.concatenate([jnp.full((1,), -1, i32), c_tile[:-1]])
    c_next = jnp.concatenate([c_tile[1:], jnp.full((1,), -1, i32)])
    c_first = c_valid & (c_tile != c_prev)
    c_last = c_valid & ((c_tile != c_next) | (pair_ids == c_total - 1))
    combine = (c_tile, c_blk, c_first.astype(i32), c_last.astype(i32), c_valid.astype(i32))
    return slot_a, slot_b, tile_plan, dispatch, combine


def _dispatch_kernel(lo_ref, end_ref, map_ref, has_ref, h_ref, sa_ref, sb_ref, wa_ref, wb_ref, hs_ref, ws_ref,
                     acc_ref, wacc_ref):
    blk = pl.program_id(0)

    @pl.when(has_ref[blk] == 1)
    def _():
        acc_ref[...] = jnp.zeros_like(acc_ref)
        wacc_ref[...] = jnp.zeros_like(wacc_ref)
        slots = blk * MOE_GB + lax.broadcasted_iota(jnp.int32, (MOE_GB, MOE_GB), 0)

        def scan_tile(t, carry):
            rows = pl.ds(pl.multiple_of(t * MOE_GB, MOE_GB), MOE_GB)
            hit_a = slots == sa_ref[t]
            hit_b = slots == sb_ref[t]
            onehot = jnp.where(hit_a, 1.0, jnp.where(hit_b, 1.0, 0.0)).astype(BF)
            acc_ref[...] += _dot(onehot, h_ref[rows, :])
            w = jnp.where(hit_a, wa_ref[t], jnp.where(hit_b, wb_ref[t], 0.0))
            wacc_ref[...] += jnp.broadcast_to(jnp.sum(w, axis=1, keepdims=True), wacc_ref.shape)
            return carry

        lax.fori_loop(lo_ref[blk], end_ref[blk], scan_tile, 0)
        hs_ref[...] = acc_ref[...].astype(BF)
        ws_ref[...] = wacc_ref[...]


def _dispatch(h2, slot_a, slot_b, w_a, w_b, plan):
    used = lambda i, lo, end, blk_map, has: (blk_map[i], 0)
    resident = lambda shape: pl.BlockSpec(shape, lambda i, *_: (0,) * len(shape), pipeline_mode=pl.Buffered(1))
    shape3 = (MOE_NTT, 1, MOE_GB)
    grid_spec = pltpu.PrefetchScalarGridSpec(
        num_scalar_prefetch=4,
        grid=(MOE_NGB,),
        in_specs=[resident((NL, D_MODEL)), resident(shape3), resident(shape3), resident(shape3),
                  resident(shape3)],
        out_specs=(pl.BlockSpec((MOE_GB, D_MODEL), used), pl.BlockSpec((MOE_GB, ROUTER_LANES), used)),
        scratch_shapes=[pltpu.VMEM((MOE_GB, D_MODEL), F32), pltpu.VMEM((MOE_GB, ROUTER_LANES), F32)])
    return pl.pallas_call(
        _dispatch_kernel,
        out_shape=(jax.ShapeDtypeStruct((MOE_NSLOT, D_MODEL), BF),
                   jax.ShapeDtypeStruct((MOE_NSLOT, ROUTER_LANES), F32)),
        grid_spec=grid_spec,
        compiler_params=_params("arbitrary"),
        name="moe_dispatch",
    )(*plan, h2, slot_a.reshape(shape3), slot_b.reshape(shape3), w_a.reshape(shape3), w_b.reshape(shape3))


def _expert_kernel(e_ref, rows_ref, n_ref, hs_ref, wg_ref, wu_ref, wd_ref, ws_ref, y_ref, acc_ref):
    s = pl.program_id(0)
    j = pl.program_id(1)
    last = pl.num_programs(1) - 1

    @pl.when(s < n_ref[0])
    def _():
        n_rows = rows_ref[s]

        def for_row_chunks(fn):
            n_full = n_rows // MOE_CH

            def full_chunk(c, carry):
                fn(pl.ds(pl.multiple_of(c * MOE_CH, MOE_CH), MOE_CH))
                return carry

            lax.fori_loop(0, n_full, full_chunk, 0)
            rest = n_rows - n_full * MOE_CH
            half = MOE_CH // 2

            @pl.when(rest >= half)
            def _():
                fn(pl.ds(pl.multiple_of(n_full * MOE_CH, MOE_GB), half))

            @pl.when(rest % half != 0)
            def _():
                fn(pl.ds(pl.multiple_of(n_rows - MOE_GB, MOE_GB), MOE_GB))

        @pl.when(j == 0)
        def _():
            acc_ref[...] = jnp.zeros_like(acc_ref)

        def accumulate(rows):
            h = hs_ref[rows, :]
            a = (_silu(_dot(h, wg_ref[...].astype(BF))) * _dot(h, wu_ref[...].astype(BF))).astype(BF)
            acc_ref[rows, :] += _dot(a, wd_ref[...].astype(BF))

        for_row_chunks(accumulate)

        @pl.when(j == last)
        def _():
            def emit(rows):
                y_ref[rows, :] = (acc_ref[rows, :] * ws_ref[rows, 0:1]).astype(BF)

            for_row_chunks(emit)


def _experts(hs, tile_plan, wg, wu, wd, ws, moe_layer):
    n_ff = D_FF // MOE_TF

    def used(s, n):
        return jnp.minimum(s, n[0] - 1)

    def ff_tile(s, j, n):
        return jnp.where(s < n[0], j, n_ff - 1)

    rows_spec = lambda width, **kw: pl.BlockSpec((MOE_SM, width), lambda s, j, e, r, n: (used(s, n), 0), **kw)
    up_spec = pl.BlockSpec((None, None, D_MODEL, MOE_TF),
                           lambda s, j, e, r, n: (moe_layer, e[s], 0, ff_tile(s, j, n)))
    grid_spec = pltpu.PrefetchScalarGridSpec(
        num_scalar_prefetch=3,
        grid=(MOE_NST, n_ff),
        in_specs=[rows_spec(D_MODEL, pipeline_mode=pl.Buffered(1)), up_spec, up_spec,
                  pl.BlockSpec((None, None, MOE_TF, D_MODEL),
                               lambda s, j, e, r, n: (moe_layer, e[s], ff_tile(s, j, n), 0)),
                  rows_spec(ROUTER_LANES)],
        out_specs=rows_spec(D_MODEL, pipeline_mode=pl.Buffered(1)),
        scratch_shapes=[pltpu.VMEM((MOE_SM, D_MODEL), F32)])
    return pl.pallas_call(
        _expert_kernel,
        out_shape=jax.ShapeDtypeStruct((MOE_NSLOT, D_MODEL), BF),
        grid_spec=grid_spec,
        compiler_params=_params("arbitrary", "arbitrary"),
        name="moe_experts",
    )(*tile_plan, hs, wg, wu, wd, ws)


def _combine_kernel(tile_ref, blk_ref, first_ref, last_ref, valid_ref,
                    y_ref, sa_ref, sb_ref, x_ref, g2_ref, lg_ref, lb_ref, o_ref, acc_ref):
    p = pl.program_id(0)

    @pl.when(first_ref[p] == 1)
    def _():
        acc_ref[...] = jnp.zeros_like(acc_ref)

    @pl.when(valid_ref[p] == 1)
    def _():
        slots = blk_ref[p] * MOE_GB + lax.broadcasted_iota(jnp.int32, (MOE_CT, MOE_GB), 1)
        onehot = jnp.where(slots == sa_ref[...], 1.0, jnp.where(slots == sb_ref[...], 1.0, 0.0)).astype(BF)
        acc_ref[...] += _dot(onehot, y_ref[...])

    @pl.when(last_ref[p] == 1)
    def _():
        def norm_rows(c, carry):
            rows = pl.ds(pl.multiple_of(c * MOE_EPILOGUE_ROWS, MOE_EPILOGUE_ROWS), MOE_EPILOGUE_ROWS)
            o_ref[rows, :] = _layer_norm(DEEPNORM_ALPHA * x_ref[rows, :] + g2_ref[...] * acc_ref[rows, :],
                                         lg_ref[...], lb_ref[...])
            return carry

        lax.fori_loop(0, MOE_CT // MOE_EPILOGUE_ROWS, norm_rows, 0)


def _combine(y, slot_a, slot_b, x_lat, mod, layer, ln_g, ln_b, plan):
    per_batch = SEQ // MOE_CT
    slot_blk = lambda p, tile, blk, *_: (blk[p], 0)
    tok_blk = lambda p, tile, *_: (tile[p], 0)
    grid_spec = pltpu.PrefetchScalarGridSpec(
        num_scalar_prefetch=5,
        grid=(MOE_NPAIR,),
        in_specs=[pl.BlockSpec((MOE_GB, D_MODEL), slot_blk),
                  pl.BlockSpec((MOE_CT, 1), tok_blk), pl.BlockSpec((MOE_CT, 1), tok_blk),
                  pl.BlockSpec((MOE_CT, D_MODEL), tok_blk),
                  pl.BlockSpec((None, 1, D_MODEL),
                               lambda p, tile, *_: (layer * 48 + (tile[p] // per_batch) * 6 + 5, 0, 0)),
                  pl.BlockSpec((1, D_MODEL), lambda p, *_: (0, 0)),
                  pl.BlockSpec((1, D_MODEL), lambda p, *_: (0, 0))],
        out_specs=pl.BlockSpec((MOE_CT, D_MODEL), tok_blk),
        scratch_shapes=[pltpu.VMEM((MOE_CT, D_MODEL), F32)])
    return pl.pallas_call(
        _combine_kernel,
        out_shape=jax.ShapeDtypeStruct((NL, D_MODEL), F32),
        grid_spec=grid_spec,
        compiler_params=_params("arbitrary"),
        name="moe_combine",
    )(*plan, y, slot_a.reshape(NL, 1), slot_b.reshape(NL, 1), x_lat, mod, ln_g, ln_b)


def _moe_ffn(h2, x_lat, mod, layer, router_w, wg, wu, wd, moe_layer, ln_g, ln_b):
    router_pad = jnp.pad(router_w, ((0, 0), (0, ROUTER_LANES - N_EXPERTS))).astype(BF)
    idx, wts = _router(h2, router_pad)
    slot_a, slot_b, tile_plan, d_plan, c_plan = _dispatch_plan(idx[:, :TOP_K])
    hs, ws = _dispatch(h2, slot_a, slot_b, wts[:, 0], wts[:, 1], d_plan)
    y = _experts(hs, tile_plan, wg, wu, wd, ws, moe_layer)
    return _combine(y, slot_a, slot_b, x_lat, mod, layer, ln_g, ln_b, c_plan)


def kernel(x, c, ctx, c_ctx, ada_w, ada_b, w_in, conv_w, gla_wg_f, gla_bg_f, gla_wg_b, gla_bg_b, gla_norm_g, na_rpb, w_br_conv, w_br_gla, w_br_na, gate_b, w_out, ln1_g, ln1_b, ln2_g, ln2_b, ffn_w_gate, ffn_w_up, ffn_w_down, moe_router, moe_w_gate, moe_w_up, moe_w_down):
    assert DEPTH == 2 and x.shape == (BATCH, SEQ, D_MODEL) and ctx.shape == (BATCH, CTX_LEN, D_MODEL)
    cvec = jnp.concatenate([c, c_ctx[None, :], jnp.zeros((8 - BATCH - 1, D_MODEL), F32)], axis=0)
    mod = _ada_table(cvec, ada_w, ada_b)
    rope_tab = _rope_table()
    x_lat, x_ctx = x.reshape(NL, D_MODEL), ctx.reshape(NC, D_MODEL)
    h1 = _modulate(x_lat, x_ctx, mod, 0)
    w_in_t = jnp.swapaxes(w_in, 1, 2)
    na_bias = _na_bias_pair_tables(na_rpb)

    for layer in range(DEPTH):
        last = layer == DEPTH - 1
        u = _project(h1, w_in_t, layer, 0, U_COLS, PROJ_TM, 1024, "proj_in")
        lr = _project(h1, w_in_t, layer, W_IN_LR, LR_COLS, PROJ_TM, LR_COLS, "proj_decay")
        un = _project(h1, w_in_t, layer, W_IN_NA, UN_COLS, PROJ_TM, NA_W, "proj_na", out_dtype=BF,
                      first_tile_scale=NA_DH ** -0.5)
        ug = _project(h1, w_in_t, layer, W_IN_GATES, N_BRANCH * D_MODEL, PROJ_TM, 1024, "proj_gates")

        y_conv = _short_conv(u, conv_w, layer)
        wpad_f, bg_f = _decay_weights(gla_wg_f[layer], gla_bg_f[layer], 0)
        wpad_b, bg_b = _decay_weights(gla_wg_b[layer], gla_bg_b[layer], GLA_RANK)
        o_f, o_b = _gla(u, lr, rope_tab, wpad_f, bg_f, wpad_b, bg_b)
        y_na = _neighbourhood_attention(un, na_bias, layer, with_ctx=not last)

        rows = NL if last else NT
        x_mid, h2 = _merge(y_conv, o_f, o_b, u, ug, y_na, x_lat, x_ctx, mod, layer, rows,
                           gla_norm_g[layer].reshape(1, GLA_DV), gate_b[layer].reshape(1, N_BRANCH * D_MODEL),
                           w_br_conv[layer].astype(BF), w_br_gla[layer].astype(BF), w_br_na[layer].astype(BF),
                           w_out[layer].astype(BF), ln1_g[layer].reshape(1, D_MODEL),
                           ln1_b[layer].reshape(1, D_MODEL))
        ln_g = ln2_g[layer].reshape(1, D_MODEL)
        ln_b = ln2_b[layer].reshape(1, D_MODEL)
        if layer % 2 == 0:
            j = layer // 2
            x_lat, h1 = _dense_ffn(h2, x_mid, mod, layer, ffn_w_gate, ffn_w_up, ffn_w_down, j, ln_g, ln_b)
            x_ctx = x_lat
        else:
            x_lat = _moe_ffn(h2, x_mid, mod, layer, moe_router[layer // 2], moe_w_gate, moe_w_up, moe_w_down,
                             layer // 2, ln_g, ln_b)
    return x_lat.reshape(BATCH, SEQ, D_MODEL)
```
